```python
import jax, jax.numpy as jnp
from jax import lax
import numpy as np

D_MODEL = 2048
BATCH = 8
SEQ = 2048
DEPTH = 2
DEC_BATCH = 1
DEC_SEQ = 16384
PAST_LEN = 128

HEAD_DIM = 128
MIX_W = D_MODEL // 2
N_HEADS = MIX_W // HEAD_DIM
N_BRANCH = 3
CHUNK = 128
ROPE_BASE = 10000.0
GRID_W = 64
WIN_ROWS = 8
WIN_COLS = 16
Q_CB = 16
K_CB = Q_CB + WIN_COLS
ML_CONV_W = 3
N_GROUPS = 4
EXPERTS_PER_GROUP = 8
N_EXPERTS = N_GROUPS * EXPERTS_PER_GROUP
TOP_K_INNER = 2
D_EXPERT = D_MODEL // 2
MOE_BLOCK = 128
EPS = 1e-6
NEG = -1e30
N_IN = 4 * MIX_W + 3 * MIX_W + 4 * MIX_W + 4 * N_HEADS + N_BRANCH * D_MODEL

kernel_name = "hybrid_bidir_retention_natten_mlstm_hmoe"

F32 = jnp.float32


def rms_norm(x, w):
    xf = x.astype(F32)
    y = xf * lax.rsqrt(jnp.mean(xf * xf, -1, keepdims=True) + EPS)
    return (y * w.astype(F32)).astype(x.dtype)


def head_rms(x, w):
    y = x * lax.rsqrt(jnp.mean(x * x, -1, keepdims=True) + EPS)
    return y * w.astype(F32)


def rope(x):
    L, d = x.shape[1], x.shape[-1]
    freqs = ROPE_BASE ** (-jnp.arange(0, d, 2, dtype=F32) / d)
    ang = jnp.arange(L, dtype=F32)[:, None] * freqs[None]
    cos = jnp.cos(ang)[None, :, None]
    sin = jnp.sin(ang)[None, :, None]
    x1, x2 = x[..., : d // 2], x[..., d // 2:]
    return jnp.concatenate([x1 * cos - x2 * sin, x1 * sin + x2 * cos], -1)


def split_points():
    sizes = [MIX_W] * 11 + [2 * N_HEADS, 2 * N_HEADS] + [D_MODEL] * N_BRANCH
    return np.cumsum(sizes)[:-1].tolist()


def retention_scan(q, k, v, log_gamma, include_diag):
    B, H, L, d = q.shape
    n = L // CHUNK
    q, k, v = (t.reshape(B, H, n, CHUNK, d) for t in (q, k, v))
    idx = jnp.arange(CHUNK, dtype=F32)
    diff = idx[:, None] - idx[None, :]
    mask = (diff >= 0) if include_diag else (diff > 0)
    decay = jnp.where(mask, jnp.exp(jnp.maximum(diff, 0.0) * log_gamma[:, None, None]), 0.0)
    scores = jnp.einsum('bhncd,bhnsd->bhncs', q, k) * decay[None, :, None]
    o_inner = jnp.einsum('bhncs,bhnse->bhnce', scores, v)
    k_end = k * jnp.exp((CHUNK - 1.0 - idx)[None, :] * log_gamma[:, None])[None, :, None, :, None]
    kv = jnp.einsum('bhncd,bhnce->bhnde', k_end, v)
    chunk_decay = jnp.exp(CHUNK * log_gamma)[None, :, None, None]

    def step(state, kv_n):
        return chunk_decay * state + kv_n, state

    _, s_prev = lax.scan(step, jnp.zeros((B, H, d, d), F32), jnp.moveaxis(kv, 2, 0))
    s_prev = jnp.moveaxis(s_prev, 0, 2)
    q_dec = q * jnp.exp((idx + 1.0)[None, :] * log_gamma[:, None])[None, :, None, :, None]
    o_cross = jnp.einsum('bhncd,bhnde->bhnce', q_dec, s_prev)
    return (o_inner + o_cross).reshape(B, H, L, d)


def retention_mixer(q, k, v, g, decay_logit, norm_w):
    B, L, _ = q.shape
    sh = (B, L, N_HEADS, HEAD_DIM)
    qf = rope(q.reshape(sh).astype(F32))
    kf = rope(k.reshape(sh).astype(F32)) * HEAD_DIM ** -0.5
    vf = v.reshape(sh).astype(F32)
    qf, kf, vf = (jnp.swapaxes(t, 1, 2) for t in (qf, kf, vf))
    lg = jax.nn.log_sigmoid(decay_logit.astype(F32))
    fwd = retention_scan(qf, kf, vf, lg[0], True)
    bwd = jnp.flip(retention_scan(jnp.flip(qf, 2), jnp.flip(kf, 2), jnp.flip(vf, 2), lg[1], False), 2)
    o = head_rms(jnp.swapaxes(fwd + bwd, 1, 2), norm_w) * jax.nn.silu(g.astype(F32)).reshape(sh)
    return o.reshape(B, L, MIX_W).astype(q.dtype)


def na_col_tables():
    n_cb = GRID_W // Q_CB
    c0 = np.arange(n_cb) * Q_CB
    kstart = np.clip(c0 - WIN_COLS // 2, 0, GRID_W - K_CB)
    kcols = kstart[:, None] + np.arange(K_CB)[None]
    qcols = c0[:, None] + np.arange(Q_CB)[None]
    cstart = np.clip(qcols - WIN_COLS // 2, 0, GRID_W - WIN_COLS)
    valid = (kcols[:, None, :] >= cstart[:, :, None]) & (kcols[:, None, :] < cstart[:, :, None] + WIN_COLS)
    dc = np.clip(kcols[:, None, :] - qcols[:, :, None] + WIN_COLS - 1, 0, 2 * WIN_COLS - 2)
    return kcols, valid, dc


def neighborhood_mixer(q, k, v, q_norm, k_norm, rpb):
    B, L, _ = q.shape
    rows = L // GRID_W
    wr = min(WIN_ROWS, rows)
    n_cb = GRID_W // Q_CB
    sh = (B, rows, GRID_W, N_HEADS, HEAD_DIM)
    qf = head_rms(q.reshape(sh).astype(F32), q_norm) * HEAD_DIM ** -0.5
    kf = head_rms(k.reshape(sh).astype(F32), k_norm)
    vf = v.reshape(sh).astype(F32)
    kcols, valid, dc = na_col_tables()
    valid = jnp.asarray(valid)[None, None, :, :, None, :]
    rpb = rpb.astype(F32)

    def one_row(r):
        rs = jnp.clip(r - wr // 2, 0, rows - wr)
        q_r = lax.dynamic_index_in_dim(qf, r, axis=1, keepdims=False).reshape(B, n_cb, Q_CB, N_HEADS, HEAD_DIM)
        k_blk = lax.dynamic_slice_in_dim(kf, rs, wr, axis=1)[:, :, kcols]
        v_blk = lax.dynamic_slice_in_dim(vf, rs, wr, axis=1)[:, :, kcols]
        s = jnp.einsum('bnqhd,bjnkhd->bhnqjk', q_r, k_blk)
        dr = rs + jnp.arange(wr) - r + (WIN_ROWS - 1)
        bias = rpb[:, dr][:, :, dc]
        s = s + jnp.transpose(bias, (0, 2, 3, 1, 4))[None]
        s = jnp.where(valid, s, NEG)
        p = jax.nn.softmax(s.reshape(B, N_HEADS, n_cb, Q_CB, wr * K_CB), -1).reshape(s.shape)
        o = jnp.einsum('bhnqjk,bjnkhe->bnqhe', p, v_blk)
        return o.reshape(B, GRID_W, N_HEADS, HEAD_DIM)

    o = lax.map(one_row, jnp.arange(rows))
    return jnp.moveaxis(o, 0, 1).reshape(B, L, MIX_W).astype(q.dtype)


def short_conv(x, w):
    pad = ML_CONV_W // 2
    L = x.shape[1]
    xp = jnp.pad(x, ((0, 0), (pad, ML_CONV_W - 1 - pad), (0, 0)))
    out = xp[:, 0:L] * w[0]
    for j in range(1, ML_CONV_W):
        out = out + xp[:, j:j + L] * w[j]
    return out


def mlstm_scan(q, k, v, i_pre, f_pre):
    B, H, L, d = q.shape
    n = L // CHUNK
    q, k, v = (t.reshape(B, H, n, CHUNK, d) for t in (q, k, v))
    ig = i_pre.reshape(B, H, n, CHUNK)
    b = jnp.cumsum(jax.nn.log_sigmoid(f_pre).reshape(B, H, n, CHUNK), -1)
    g = b[..., -1]
    a = g[..., None] - b + ig
    m_loc = jnp.max(a, -1)
    kw = k * jnp.exp(a - m_loc[..., None])[..., None]
    c_loc = jnp.einsum('bhncd,bhnce->bhnde', kw, v)
    n_loc = jnp.sum(kw, 3)

    def step(carry, inp):
        c_p, n_p, m_p = carry
        c_l, n_l, m_l, g_n = inp
        m_new = jnp.maximum(g_n + m_p, m_l)
        sp = jnp.exp(g_n + m_p - m_new)
        sl = jnp.exp(m_l - m_new)
        c_new = sp[..., None, None] * c_p + sl[..., None, None] * c_l
        n_new = sp[..., None] * n_p + sl[..., None] * n_l
        return (c_new, n_new, m_new), (c_p, n_p, m_p)

    init = (jnp.zeros((B, H, d, d), F32), jnp.zeros((B, H, d), F32), jnp.full((B, H), NEG, F32))
    mv = lambda t: jnp.moveaxis(t, 2, 0)
    _, (c_prev, n_prev, m_prev) = lax.scan(step, init, (mv(c_loc), mv(n_loc), mv(m_loc), mv(g)))
    c_prev, n_prev, m_prev = (jnp.moveaxis(t, 0, 2) for t in (c_prev, n_prev, m_prev))
    idx = jnp.arange(CHUNK)
    causal = idx[:, None] >= idx[None, :]
    dmat = jnp.where(causal, b[..., :, None] - b[..., None, :] + ig[..., None, :], NEG)
    inter = b + m_prev[..., None]
    m_row = jnp.maximum(jnp.max(dmat, -1), inter)
    s = jnp.einsum('bhncd,bhnsd->bhncs', q, k) * jnp.exp(dmat - m_row[..., None])
    e_int = jnp.exp(inter - m_row)
    num = jnp.einsum('bhncs,bhnse->bhnce', s, v) + e_int[..., None] * jnp.einsum('bhncd,bhnde->bhnce', q, c_prev)
    den = jnp.sum(s, -1) + e_int * jnp.einsum('bhncd,bhnd->bhnc', q, n_prev)
    h = num / jnp.maximum(jnp.abs(den), jnp.exp(-m_row))[..., None]
    return h.reshape(B, H, L, d)


def mlstm_mixer(q, k, v, o, i_pre, f_pre, conv_w, ig_b, fg_b, norm_w):
    B, L, _ = q.shape
    sh = (B, L, N_HEADS, HEAD_DIM)
    qk = jax.nn.silu(short_conv(jnp.concatenate([q, k], -1).astype(F32), conv_w.astype(F32)))
    qf = qk[..., :MIX_W].reshape(sh)
    kf = qk[..., MIX_W:].reshape(sh) * HEAD_DIM ** -0.5
    vf = v.reshape(sh).astype(F32)
    qf, kf, vf = (jnp.swapaxes(t, 1, 2) for t in (qf, kf, vf))
    ig = jnp.transpose(i_pre.astype(F32).reshape(B, L, 2, N_HEADS) + ig_b.astype(F32), (2, 0, 3, 1))
    fg = jnp.transpose(f_pre.astype(F32).reshape(B, L, 2, N_HEADS) + fg_b.astype(F32), (2, 0, 3, 1))
    flip = lambda t: jnp.flip(t, 2)
    fwd = mlstm_scan(qf, kf, vf, ig[0], fg[0])
    bwd = flip(mlstm_scan(flip(qf), flip(kf), flip(vf), flip(ig[1]), flip(fg[1])))
    h = head_rms(jnp.swapaxes(fwd + bwd, 1, 2), norm_w) * jax.nn.sigmoid(o.astype(F32)).reshape(sh)
    return h.reshape(B, L, MIX_W).astype(q.dtype)


def hierarchical_moe(x, rg_w, rg_b, re_w, re_b, w_gate, w_up, w_down):
    T, D = x.shape
    p_grp = jax.nn.softmax((x @ rg_w + rg_b).astype(F32), -1)
    p_top, grp = lax.top_k(p_grp, 1)
    le = (x @ re_w + re_b).astype(F32).reshape(T, N_GROUPS, EXPERTS_PER_GROUP)
    le = le[jnp.arange(T), grp[:, 0]]
    p_in, e_in = lax.top_k(jax.nn.softmax(le, -1), TOP_K_INNER)
    gate = p_top * p_in / jnp.sum(p_in, -1, keepdims=True)
    expert = grp * EXPERTS_PER_GROUP + e_in
    n_assign = T * TOP_K_INNER
    flat_e = expert.reshape(-1)
    flat_t = jnp.repeat(jnp.arange(T), TOP_K_INNER)
    order = jnp.argsort(flat_e)
    se, st, sw = flat_e[order], flat_t[order], gate.reshape(-1)[order]
    counts = jnp.bincount(flat_e, length=N_EXPERTS)
    padded = (counts + MOE_BLOCK - 1) // MOE_BLOCK * MOE_BLOCK
    start = jnp.cumsum(counts) - counts
    pend = jnp.cumsum(padded)
    pstart = pend - padded
    dest = pstart[se] + jnp.arange(n_assign) - start[se]
    n_blocks = (n_assign + N_EXPERTS * (MOE_BLOCK - 1) + MOE_BLOCK - 1) // MOE_BLOCK
    xb = jnp.zeros((n_blocks * MOE_BLOCK, D), x.dtype).at[dest].set(x[st])
    blk_e = jnp.minimum(jnp.searchsorted(pend, jnp.arange(n_blocks) * MOE_BLOCK, side='right'), N_EXPERTS - 1)

    def expert_block(args):
        xblk, e = args
        hid = jax.nn.silu(xblk @ w_gate[e]) * (xblk @ w_up[e])
        return hid @ w_down[e]

    yb = lax.map(expert_block, (xb.reshape(n_blocks, MOE_BLOCK, D), blk_e))
    y = yb.reshape(-1, D)[dest].astype(F32) * sw[:, None]
    return jax.ops.segment_sum(y, st, num_segments=T).astype(x.dtype)


def trunk_layer(x, norm1_w, w_in, ret_decay, ret_norm_w, na_q_norm, na_k_norm, na_rpb,
                ml_conv, ml_igate_b, ml_fgate_b, ml_norm_w, w_branch, branch_gate_b, w_out,
                norm2_w, router_g_w, router_g_b, router_e_w, router_e_b, exp_w_gate, exp_w_up, exp_w_down):
    B, L, D = x.shape
    xn = rms_norm(x, norm1_w)
    proj = xn @ w_in
    (rq, rk, rv, rg, nq, nk, nv, mq, mk, mvv, mo, mi, mf, ga, gb, gc) = jnp.split(proj, split_points(), axis=-1)
    o_ret = retention_mixer(rq, rk, rv, rg, ret_decay, ret_norm_w)
    o_na = neighborhood_mixer(nq, nk, nv, na_q_norm, na_k_norm, na_rpb)
    o_ml = mlstm_mixer(mq, mk, mvv, mo, mi, mf, ml_conv, ml_igate_b, ml_fgate_b, ml_norm_w)

    def gated(z, bias, o, w):
        return jax.nn.sigmoid((z + bias).astype(F32)) * (o @ w).astype(F32)

    merged = (gated(ga, branch_gate_b[0], o_ret, w_branch[0])
              + gated(gb, branch_gate_b[1], o_na, w_branch[1])
              + gated(gc, branch_gate_b[2], o_ml, w_branch[2]))
    h = x + merged.astype(x.dtype) @ w_out
    hn = rms_norm(h, norm2_w).reshape(B * L, D)
    y = hierarchical_moe(hn, router_g_w, router_g_b, router_e_w, router_e_b, exp_w_gate, exp_w_up, exp_w_down)
    return h + y.reshape(B, L, D)


def setup_inputs(seed: int = 0) -> dict:
    key = jax.random.key(seed)
    ks = jax.random.split(key, 26)
    nrm = lambda k, shape, scale: jax.random.normal(k, shape, F32) * scale
    H = N_HEADS
    ret_init = jnp.log(2.0 ** (5.0 + jnp.arange(H, dtype=F32)) - 1.0)
    fg_init = jnp.linspace(3.0, 6.0, H, dtype=F32)
    return {
        "x_prompt": nrm(ks[0], (BATCH, SEQ, D_MODEL), 1.0),
        "x_sample": nrm(ks[1], (DEC_BATCH, DEC_SEQ, D_MODEL), 1.0),
        "norm1_w": 1.0 + nrm(ks[2], (DEPTH, D_MODEL), 0.02),
        "w_in": nrm(ks[3], (DEPTH, D_MODEL, N_IN), D_MODEL ** -0.5),
        "ret_decay": ret_init[None, None, :] + nrm(ks[4], (DEPTH, 2, H), 0.1),
        "ret_norm_w": 1.0 + nrm(ks[5], (DEPTH, H, HEAD_DIM), 0.02),
        "na_q_norm": 1.0 + nrm(ks[6], (DEPTH, HEAD_DIM), 0.02),
        "na_k_norm": 1.0 + nrm(ks[7], (DEPTH, HEAD_DIM), 0.02),
        "na_rpb": nrm(ks[8], (DEPTH, H, 2 * WIN_ROWS - 1, 2 * WIN_COLS - 1), 0.1),
        "ml_conv": nrm(ks[9], (DEPTH, ML_CONV_W, 2 * MIX_W), ML_CONV_W ** -0.5),
        "ml_igate_b": nrm(ks[10], (DEPTH, 2, H), 0.1),
        "ml_fgate_b": fg_init[None, None, :] + nrm(ks[11], (DEPTH, 2, H), 0.1),
        "ml_norm_w": 1.0 + nrm(ks[12], (DEPTH, H, HEAD_DIM), 0.02),
        "w_branch": nrm(ks[13], (DEPTH, N_BRANCH, MIX_W, D_MODEL), MIX_W ** -0.5),
        "branch_gate_b": nrm(ks[14], (DEPTH, N_BRANCH, D_MODEL), 0.02),
        "w_out": nrm(ks[15], (DEPTH, D_MODEL, D_MODEL), D_MODEL ** -0.5),
        "norm2_w": 1.0 + nrm(ks[16], (DEPTH, D_MODEL), 0.02),
        "router_g_w": nrm(ks[17], (DEPTH, D_MODEL, N_GROUPS), D_MODEL ** -0.5),
        "router_g_b": nrm(ks[18], (DEPTH, N_GROUPS), 0.01),
        "router_e_w": nrm(ks[19], (DEPTH, D_MODEL, N_EXPERTS), D_MODEL ** -0.5),
        "router_e_b": nrm(ks[20], (DEPTH, N_EXPERTS), 0.01),
        "exp_w_gate": nrm(ks[21], (DEPTH, N_EXPERTS, D_MODEL, D_EXPERT), D_MODEL ** -0.5),
        "exp_w_up": nrm(ks[22], (DEPTH, N_EXPERTS, D_MODEL, D_EXPERT), D_MODEL ** -0.5),
        "exp_w_down": nrm(ks[23], (DEPTH, N_EXPERTS, D_EXPERT, D_MODEL), D_EXPERT ** -0.5),
    }


def reference(x_prompt, x_sample, norm1_w, w_in, ret_decay, ret_norm_w, na_q_norm, na_k_norm, na_rpb,
              ml_conv, ml_igate_b, ml_fgate_b, ml_norm_w, w_branch, branch_gate_b, w_out,
              norm2_w, router_g_w, router_g_b, router_e_w, router_e_b, exp_w_gate, exp_w_up, exp_w_down):
    def run(x):
        for l in range(DEPTH):
            x = trunk_layer(x, norm1_w[l], w_in[l], ret_decay[l], ret_norm_w[l], na_q_norm[l], na_k_norm[l],
                            na_rpb[l], ml_conv[l], ml_igate_b[l], ml_fgate_b[l], ml_norm_w[l], w_branch[l],
                            branch_gate_b[l], w_out[l], norm2_w[l], router_g_w[l], router_g_b[l],
                            router_e_w[l], router_e_b[l], exp_w_gate[l], exp_w_up[l], exp_w_down[l])
        return x

    y_prompt = run(x_prompt)
    y_sample = run(x_sample)
    return (y_prompt, y_sample)
```

```python
import functools

import numpy as np
import jax
import jax.numpy as jnp
from jax import lax
from jax.experimental import pallas as pl
from jax.experimental.pallas import tpu as pltpu

D_MODEL = 2048
HEAD_DIM = 128
MIX_W = D_MODEL // 2
N_HEADS = MIX_W // HEAD_DIM
N_BRANCH = 3
CHUNK = 128
ROPE_BASE = 10000.0
GRID_W = 64
WIN_ROWS = 8
WIN_COLS = 16
ML_CONV_W = 3
N_GROUPS = 4
EXPERTS_PER_GROUP = 8
N_EXPERTS = N_GROUPS * EXPERTS_PER_GROUP
TOP_K_INNER = 2
D_EXPERT = D_MODEL // 2
EPS = 1e-6
NEG = -1e30

F32 = jnp.float32
BF16 = jnp.bfloat16

LANES = 128
SEG_W = MIX_W
SEG_GA, SEG_GB, SEG_GC = 0, 2, 4
SEG_RQ, SEG_RK, SEG_RV, SEG_RG = 6, 7, 8, 9
SEG_NQ, SEG_NK, SEG_NV = 10, 11, 12
SEG_MQ, SEG_MK, SEG_MV, SEG_MO = 13, 14, 15, 16
N_SEG = 17
N_GATE_COLS = 4 * N_HEADS

NA_QROWS = 2
NA_KBLKS = 5
NA_KW = NA_KBLKS * CHUNK
MOE_ROWS = 256

VMEM_LIMIT = 56 * 1024 * 1024


def _cparams(sem):
    return pltpu.CompilerParams(dimension_semantics=sem, vmem_limit_bytes=VMEM_LIMIT)


def _hs(h):
    return slice(h * HEAD_DIM, (h + 1) * HEAD_DIM)


def _dot(a, b):
    return jnp.dot(a, b, preferred_element_type=F32)


def _dot_nt(a, b):
    return lax.dot_general(a, b, (((1,), (1,)), ((), ())), preferred_element_type=F32)


def _dot_tn(a, b):
    return lax.dot_general(a, b, (((0,), (0,)), ((), ())), preferred_element_type=F32)


def _inproj_kernel(x_ref, nw_ref, w_ref, wif_ref, wift_ref, cos_ref, sin_ref, qn_ref, kn_ref,
                   proj_ref, g_ref, gt_ref, xn_ref):
    j = pl.program_id(1)

    @pl.when(j == 0)
    def _():
        x = x_ref[...]
        y = x * lax.rsqrt(jnp.mean(x * x, -1, keepdims=True) + EPS) * nw_ref[...]
        xn = y.astype(BF16)
        xn_ref[...] = xn
        g_ref[...] = _dot(xn, wif_ref[...])
        gt_ref[...] = _dot_nt(wift_ref[...], xn)

    acc = _dot(xn_ref[...], w_ref[...])

    is_rope = (j == SEG_RQ) | (j == SEG_RK)
    is_norm = (j == SEG_NQ) | (j == SEG_NK)

    @pl.when(is_rope)
    def _():
        scale = jnp.where(j == SEG_RK, HEAD_DIM ** -0.5, 1.0).astype(F32)
        cos = cos_ref[...]
        sin = sin_ref[...]
        for h in range(N_HEADS):
            xh = acc[:, _hs(h)]
            r = xh * cos + pltpu.roll(xh, HEAD_DIM // 2, 1) * sin
            proj_ref[:, _hs(h)] = (r * scale).astype(BF16)

    @pl.when(is_norm)
    def _():
        w = jnp.where(j == SEG_NQ, qn_ref[...] * (HEAD_DIM ** -0.5), kn_ref[...])
        for h in range(N_HEADS):
            xh = acc[:, _hs(h)]
            y = xh * lax.rsqrt(jnp.mean(xh * xh, -1, keepdims=True) + EPS) * w
            proj_ref[:, _hs(h)] = y.astype(BF16)

    @pl.when(jnp.logical_not(is_rope | is_norm))
    def _():
        proj_ref[...] = acc.astype(BF16)


def _inproj(x2d, L, nw, w_main, wif, wift, cos_t, sin_t, qn, kn):
    T = x2d.shape[0]
    tm = min(1024, L)
    nlt = L // tm
    return pl.pallas_call(
        _inproj_kernel,
        grid=(T // tm, N_SEG),
        in_specs=[
            pl.BlockSpec((tm, D_MODEL), lambda i, j: (i, 0)),
            pl.BlockSpec((1, D_MODEL), lambda i, j: (0, 0)),
            pl.BlockSpec((D_MODEL, SEG_W), lambda i, j: (0, j)),
            pl.BlockSpec((D_MODEL, LANES), lambda i, j: (0, 0)),
            pl.BlockSpec((N_GATE_COLS, D_MODEL), lambda i, j: (0, 0)),
            pl.BlockSpec((tm, HEAD_DIM), lambda i, j: (i % nlt, 0)),
            pl.BlockSpec((tm, HEAD_DIM), lambda i, j: (i % nlt, 0)),
            pl.BlockSpec((1, HEAD_DIM), lambda i, j: (0, 0)),
            pl.BlockSpec((1, HEAD_DIM), lambda i, j: (0, 0)),
        ],
        out_specs=[
            pl.BlockSpec((tm, SEG_W), lambda i, j: (i, j)),
            pl.BlockSpec((tm, LANES), lambda i, j: (i, 0)),
            pl.BlockSpec((N_GATE_COLS, tm), lambda i, j: (0, i)),
        ],
        out_shape=[
            jax.ShapeDtypeStruct((T, N_SEG * SEG_W), BF16),
            jax.ShapeDtypeStruct((T, LANES), F32),
            jax.ShapeDtypeStruct((N_GATE_COLS, T), F32),
        ],
        scratch_shapes=[pltpu.VMEM((tm, D_MODEL), BF16)],
        compiler_params=_cparams(("arbitrary", "arbitrary")),
        name="inproj",
    )(x2d, nw, w_main, wif, wift, cos_t, sin_t, qn, kn)


def _ret_state_kernel(k_ref, v_ref, kb_ref, cdb_ref, sb_ref, s_scr):
    c = pl.program_id(1)

    @pl.when(c == 0)
    def _():
        s_scr[...] = jnp.zeros_like(s_scr)

    sb_ref[0] = s_scr[...].astype(BF16)
    kk = (k_ref[...].astype(F32) * kb_ref[...]).astype(BF16)
    v = v_ref[...]
    for h in range(N_HEADS):
        kv = _dot_tn(kk[:, _hs(h)], v[:, _hs(h)])
        s_scr[h] = s_scr[h] * cdb_ref[h:h + 1, :] + kv


def _ret_out_kernel(q_ref, k_ref, v_ref, g_ref, sb_ref, dmat_ref, qf_ref, qb_ref, kf_ref, cdf_ref, nw_ref,
                    o_ref, s_scr):
    c = pl.program_id(1)

    @pl.when(c == 0)
    def _():
        s_scr[...] = jnp.zeros_like(s_scr)

    q = q_ref[...]
    k = k_ref[...]
    v = v_ref[...]
    qf32 = q.astype(F32)
    q_fwd = (qf32 * qf_ref[...]).astype(BF16)
    q_bwd = (qf32 * qb_ref[...]).astype(BF16)
    k_end = (k.astype(F32) * kf_ref[...]).astype(BF16)
    g = g_ref[...].astype(F32)
    for h in range(N_HEADS):
        hs = _hs(h)
        s = _dot_nt(q[:, hs], k[:, hs]) * dmat_ref[h]
        o = _dot(s.astype(BF16), v[:, hs])
        o = o + _dot(q_fwd[:, hs], s_scr[h].astype(BF16))
        o = o + _dot(q_bwd[:, hs], sb_ref[0, h])
        s_scr[h] = s_scr[h] * cdf_ref[h:h + 1, :] + _dot_tn(k_end[:, hs], v[:, hs])
        y = o * lax.rsqrt(jnp.mean(o * o, -1, keepdims=True) + EPS) * nw_ref[h:h + 1, :]
        gh = g[:, hs]
        o_ref[:, hs] = (y * (gh * jax.nn.sigmoid(gh))).astype(BF16)


def _retention(proj, B, L, ret_decay, ret_norm_w):
    n = L // CHUNK
    T = B * L
    lg = jax.nn.log_sigmoid(ret_decay.astype(F32))
    idx = jnp.arange(CHUNK, dtype=F32)
    diff = idx[:, None] - idx[None, :]
    dmat = jnp.where(diff >= 0, jnp.exp(jnp.maximum(diff, 0.0) * lg[0][:, None, None]),
                     jnp.exp(jnp.maximum(-diff, 0.0) * lg[1][:, None, None]))

    def lane_tab(e):
        return jnp.repeat(jnp.exp(e).T, HEAD_DIM, axis=1)

    qf_tab = lane_tab((idx + 1.0)[None, :] * lg[0][:, None])
    qb_tab = lane_tab((CHUNK - idx)[None, :] * lg[1][:, None])
    kf_tab = lane_tab((CHUNK - 1.0 - idx)[None, :] * lg[0][:, None])
    kb_tab = lane_tab(idx[None, :] * lg[1][:, None])
    cdf = jnp.broadcast_to(jnp.exp(CHUNK * lg[0])[:, None], (N_HEADS, HEAD_DIM))
    cdb = jnp.broadcast_to(jnp.exp(CHUNK * lg[1])[:, None], (N_HEADS, HEAD_DIM))

    def seg(s, rev=False):
        if rev:
            return pl.BlockSpec((CHUNK, SEG_W), lambda b, c: (b * n + n - 1 - c, s))
        return pl.BlockSpec((CHUNK, SEG_W), lambda b, c: (b * n + c, s))

    full = lambda shape: pl.BlockSpec(shape, lambda b, c: (0,) * len(shape))
    st_shape = (1, N_HEADS, HEAD_DIM, HEAD_DIM)

    sb = pl.pallas_call(
        _ret_state_kernel,
        grid=(B, n),
        in_specs=[seg(SEG_RK, True), seg(SEG_RV, True), full((CHUNK, MIX_W)), full((N_HEADS, HEAD_DIM))],
        out_specs=pl.BlockSpec(st_shape, lambda b, c: (b * n + n - 1 - c, 0, 0, 0)),
        out_shape=jax.ShapeDtypeStruct((B * n, N_HEADS, HEAD_DIM, HEAD_DIM), BF16),
        scratch_shapes=[pltpu.VMEM((N_HEADS, HEAD_DIM, HEAD_DIM), F32)],
        compiler_params=_cparams(("arbitrary", "arbitrary")),
        name="ret_state",
    )(proj, proj, kb_tab, cdb)

    return pl.pallas_call(
        _ret_out_kernel,
        grid=(B, n),
        in_specs=[seg(SEG_RQ), seg(SEG_RK), seg(SEG_RV), seg(SEG_RG),
                  pl.BlockSpec(st_shape, lambda b, c: (b * n + c, 0, 0, 0)),
                  full((N_HEADS, CHUNK, CHUNK)), full((CHUNK, MIX_W)), full((CHUNK, MIX_W)),
                  full((CHUNK, MIX_W)), full((N_HEADS, HEAD_DIM)), full((N_HEADS, HEAD_DIM))],
        out_specs=pl.BlockSpec((CHUNK, MIX_W), lambda b, c: (b * n + c, 0)),
        out_shape=jax.ShapeDtypeStruct((T, MIX_W), BF16),
        scratch_shapes=[pltpu.VMEM((N_HEADS, HEAD_DIM, HEAD_DIM), F32)],
        compiler_params=_cparams(("arbitrary", "arbitrary")),
        name="ret_out",
    )(proj, proj, proj, proj, sb, dmat, qf_tab, qb_tab, kf_tab, cdf, ret_norm_w.astype(F32))


def _na_index_tables(nqb):
    rows = NA_QROWS * nqb
    qi = np.arange(CHUNK)
    ki = np.arange(NA_KW)

    def one(qb):
        kb = NA_QROWS * int(np.clip(qb - 2, 0, nqb - NA_KBLKS))
        r = (NA_QROWS * qb + qi // GRID_W)[:, None]
        qc = (qi % GRID_W)[:, None]
        kr = (kb + ki // GRID_W)[None, :]
        kc = (ki % GRID_W)[None, :]
        rs = np.clip(r - WIN_ROWS // 2, 0, rows - WIN_ROWS)
        cs = np.clip(qc - WIN_COLS // 2, 0, GRID_W - WIN_COLS)
        valid = (kr >= rs) & (kr < rs + WIN_ROWS) & (kc >= cs) & (kc < cs + WIN_COLS)
        dr = np.clip(kr - r + WIN_ROWS - 1, 0, 2 * WIN_ROWS - 2)
        dc = np.clip(kc - qc + WIN_COLS - 1, 0, 2 * WIN_COLS - 2)
        return dr + 0 * dc, dc + 0 * dr, valid

    reps = [0, 1, 2, nqb - 2, nqb - 1]
    tabs = [one(qb) for qb in reps]
    for qb in range(2, nqb - 2):
        t = one(qb)
        assert all(np.array_equal(a, b) for a, b in zip(t, tabs[2]))
    return tuple(np.stack([t[i] for t in tabs]) for i in range(3))


def _na_kernel(q_ref, k0, k1, k2, k3, k4, v0, v1, v2, v3, v4, bias_ref, o_ref):
    k_refs = (k0, k1, k2, k3, k4)
    v_refs = (v0, v1, v2, v3, v4)
    for h in range(N_HEADS):
        hs = _hs(h)
        q = q_ref[:, hs]
        s = [_dot_nt(q, k_refs[j][:, hs]) + bias_ref[0, h, :, j * CHUNK:(j + 1) * CHUNK]
             for j in range(NA_KBLKS)]
        m = jnp.max(s[0], -1, keepdims=True)
        for j in range(1, NA_KBLKS):
            m = jnp.maximum(m, jnp.max(s[j], -1, keepdims=True))
        l = jnp.zeros_like(m)
        o = jnp.zeros((CHUNK, HEAD_DIM), F32)
        for j in range(NA_KBLKS):
            p = jnp.exp(s[j] - m)
            l = l + jnp.sum(p, -1, keepdims=True)
            o = o + _dot(p.astype(BF16), v_refs[j][:, hs])
        o_ref[:, hs] = (o / l).astype(BF16)


def _neighborhood(proj, B, L, rpb):
    nqb = L // CHUNK
    assert nqb >= NA_KBLKS and L % (GRID_W * NA_QROWS) == 0
    T = B * L
    dr, dc, valid = _na_index_tables(nqb)
    bias = jnp.where(jnp.asarray(valid)[:, None], rpb.astype(F32)[:, dr, dc].transpose(1, 0, 2, 3), NEG)

    def kspec(s, j):
        return pl.BlockSpec((CHUNK, SEG_W),
                            lambda b, qb: (b * nqb + jnp.clip(qb - 2, 0, nqb - NA_KBLKS) + j, s))

    def cls(qb):
        return jnp.where(qb < 2, qb, jnp.where(qb >= nqb - 2, qb - (nqb - 5), 2))

    return pl.pallas_call(
        _na_kernel,
        grid=(B, nqb),
        in_specs=[pl.BlockSpec((CHUNK, SEG_W), lambda b, qb: (b * nqb + qb, SEG_NQ))]
        + [kspec(SEG_NK, j) for j in range(NA_KBLKS)]
        + [kspec(SEG_NV, j) for j in range(NA_KBLKS)]
        + [pl.BlockSpec((1, N_HEADS, CHUNK, NA_KW), lambda b, qb: (cls(qb), 0, 0, 0))],
        out_specs=pl.BlockSpec((CHUNK, MIX_W), lambda b, qb: (b * nqb + qb, 0)),
        out_shape=jax.ShapeDtypeStruct((T, MIX_W), BF16),
        compiler_params=_cparams(("arbitrary", "arbitrary")),
        name="neighborhood",
    )(*([proj] * (1 + 2 * NA_KBLKS)), bias)


HALO = 16


def _log_sigmoid(x):
    return jnp.minimum(x, 0.0) - jnp.log1p(jnp.exp(-jnp.abs(x)))


def _conv_silu(x_ref, prev_ref, next_ref, w_ref, col0, c, n):
    x = x_ref[...].astype(F32)
    row = lax.broadcasted_iota(jnp.int32, x.shape, 0)
    prev_row = prev_ref[HALO - 1:HALO, :].astype(F32) * (c > 0).astype(F32)
    next_row = next_ref[0:1, :].astype(F32) * (c < n - 1).astype(F32)
    x_prev = jnp.where(row == 0, prev_row, pltpu.roll(x, 1, 0))
    x_next = jnp.where(row == CHUNK - 1, next_row, pltpu.roll(x, CHUNK - 1, 0))
    cs = slice(col0, col0 + MIX_W)
    y = x_prev * w_ref[0:1, cs] + x * w_ref[1:2, cs] + x_next * w_ref[2:3, cs]
    return y * jax.nn.sigmoid(y)


def _tri():
    r = lax.broadcasted_iota(jnp.int32, (CHUNK, CHUNK), 0)
    c = lax.broadcasted_iota(jnp.int32, (CHUNK, CHUNK), 1)
    return (c <= r).astype(F32), (c >= r).astype(F32)


def _hp_dot(a, b):
    return jnp.dot(a, b, preferred_element_type=F32, precision=lax.Precision.HIGHEST)


def _gate_forms(gc_ref, gr_ref, bc_ref, br_ref):
    low, up = _tri()
    g_col = gc_ref[...] + bc_ref[...]
    g_row = gr_ref[...] + br_ref[...]
    lf_col = _log_sigmoid(g_col)
    lf_row = _log_sigmoid(g_row)
    b_col = _hp_dot(low, lf_col)
    rb_col = _hp_dot(up, lf_col)
    b_row = _hp_dot(lf_row, up)
    rb_row = _hp_dot(lf_row, low)
    return g_col, g_row, b_col, rb_col, b_row, rb_row


def _ml_state_update(kc_h, v_h, a_col, g_tot, c_scr, n_scr, m_scr, h):
    m_p = m_scr[h:h + 1, 0:1]
    m_loc = jnp.max(a_col, 0, keepdims=True)
    kw = kc_h * jnp.exp(a_col - m_loc)
    c_loc = _dot_tn(kw.astype(BF16), v_h)
    n_loc = jnp.sum(kw, 0, keepdims=True)
    m_new = jnp.maximum(g_tot + m_p, m_loc)
    sp = jnp.exp(g_tot + m_p - m_new)
    sl = jnp.exp(m_loc - m_new)
    c_scr[h] = sp * c_scr[h] + sl * c_loc
    n_scr[h:h + 1, :] = sp * n_scr[h:h + 1, :] + sl * n_loc
    m_scr[h:h + 1, :] = jnp.broadcast_to(m_new, (1, HEAD_DIM))


def _ml_init(c, c_scr, n_scr, m_scr):
    @pl.when(c == 0)
    def _():
        c_scr[...] = jnp.zeros_like(c_scr)
        n_scr[...] = jnp.zeros_like(n_scr)
        m_scr[...] = jnp.full_like(m_scr, NEG)


def _ml_state_kernel(k_ref, kp_ref, kn_ref, v_ref, gc_ref, gr_ref, cw_ref, bc_ref, br_ref,
                     cb_ref, nm_ref, c_scr, n_scr, m_scr, *, n):
    step = pl.program_id(1)
    _ml_init(step, c_scr, n_scr, m_scr)
    c = n - 1 - step
    cb_ref[0] = c_scr[...].astype(BF16)
    nm_ref[0, 0:N_HEADS, :] = n_scr[...]
    nm_ref[0, N_HEADS:2 * N_HEADS, :] = m_scr[...]

    kc = _conv_silu(k_ref, kp_ref, kn_ref, cw_ref, MIX_W, c, n) * (HEAD_DIM ** -0.5)
    v = v_ref[...]
    g_col, _, _, rb_col, _, _ = _gate_forms(gc_ref, gr_ref, bc_ref, br_ref)
    for h in range(N_HEADS):
        ci = N_HEADS + h
        cf = 3 * N_HEADS + h
        g_tot = rb_col[0:1, cf:cf + 1]
        a_col = g_tot - rb_col[:, cf:cf + 1] + g_col[:, ci:ci + 1]
        _ml_state_update(kc[:, _hs(h)], v[:, _hs(h)], a_col, g_tot, c_scr, n_scr, m_scr, h)


def _ml_direction(qk, qc_h, q_bf, v_h, mask, d_col, d_row, ig_row, m_p, c_prev, n_prev):
    dmat = jnp.where(mask, d_col - d_row + ig_row, NEG)
    inter = d_col + m_p
    m_row = jnp.maximum(jnp.max(dmat, -1, keepdims=True), inter)
    s = qk * jnp.exp(dmat - m_row)
    e_int = jnp.exp(inter - m_row)
    num = _dot(s.astype(BF16), v_h) + e_int * _dot(q_bf, c_prev)
    den = jnp.sum(s, -1, keepdims=True) + e_int * jnp.sum(qc_h * n_prev, -1, keepdims=True)
    return num / jnp.maximum(jnp.abs(den), jnp.exp(-m_row))


def _ml_out_kernel(q_ref, qp_ref, qn_ref, k_ref, kp_ref, kn_ref, v_ref, o_ref_in, gc_ref, gr_ref,
                   cb_ref, nm_ref, cw_ref, bc_ref, br_ref, nw_ref,
                   out_ref, c_scr, n_scr, m_scr, *, n):
    c = pl.program_id(1)
    _ml_init(c, c_scr, n_scr, m_scr)
    qc = _conv_silu(q_ref, qp_ref, qn_ref, cw_ref, 0, c, n)
    kc = _conv_silu(k_ref, kp_ref, kn_ref, cw_ref, MIX_W, c, n) * (HEAD_DIM ** -0.5)
    q_bf = qc.astype(BF16)
    k_bf = kc.astype(BF16)
    v = v_ref[...]
    og = o_ref_in[...].astype(F32)
    g_col, g_row, b_col, rb_col, b_row, rb_row = _gate_forms(gc_ref, gr_ref, bc_ref, br_ref)
    r = lax.broadcasted_iota(jnp.int32, (CHUNK, CHUNK), 0)
    cc = lax.broadcasted_iota(jnp.int32, (CHUNK, CHUNK), 1)
    causal = cc <= r
    anti = cc >= r
    for h in range(N_HEADS):
        hs = _hs(h)
        v_h = v[:, hs]
        qk = _dot_nt(q_bf[:, hs], k_bf[:, hs])
        f0 = 2 * N_HEADS + h
        f1 = 3 * N_HEADS + h
        i0 = h
        i1 = N_HEADS + h
        h_f = _ml_direction(qk, qc[:, hs], q_bf[:, hs], v_h, causal,
                            b_col[:, f0:f0 + 1], b_row[f0:f0 + 1, :], g_row[i0:i0 + 1, :],
                            m_scr[h:h + 1, 0:1], c_scr[h].astype(BF16), n_scr[h:h + 1, :])
        h_b = _ml_direction(qk, qc[:, hs], q_bf[:, hs], v_h, anti,
                            rb_col[:, f1:f1 + 1], rb_row[f1:f1 + 1, :], g_row[i1:i1 + 1, :],
                            nm_ref[0, N_HEADS + h:N_HEADS + h + 1, 0:1], cb_ref[0, h], nm_ref[0, h:h + 1, :])
        g_tot = b_col[CHUNK - 1:CHUNK, f0:f0 + 1]
        a_col = g_tot - b_col[:, f0:f0 + 1] + g_col[:, i0:i0 + 1]
        _ml_state_update(kc[:, hs], v_h, a_col, g_tot, c_scr, n_scr, m_scr, h)
        o = h_f + h_b
        y = o * lax.rsqrt(jnp.mean(o * o, -1, keepdims=True) + EPS) * nw_ref[h:h + 1, :]
        out_ref[:, hs] = (y * jax.nn.sigmoid(og[:, hs])).astype(BF16)


def _mlstm(proj, gates, gates_t, B, L, conv_w, ig_b, fg_b, norm_w):
    n = L // CHUNK
    T = B * L
    hb = CHUNK // HALO
    bias = jnp.concatenate([ig_b.astype(F32).reshape(-1), fg_b.astype(F32).reshape(-1)])
    bias_col = jnp.zeros((1, LANES), F32).at[0, :N_GATE_COLS].set(bias)
    bias_row = jnp.broadcast_to(bias[:, None], (N_GATE_COLS, CHUNK))
    cw = conv_w.astype(F32)

    def seg(s, rev):
        if rev:
            return pl.BlockSpec((CHUNK, SEG_W), lambda b, c: (b * n + n - 1 - c, s))
        return pl.BlockSpec((CHUNK, SEG_W), lambda b, c: (b * n + c, s))

    def chunk_of(c, rev):
        return n - 1 - c if rev else c

    def prev_spec(s, rev):
        return pl.BlockSpec((HALO, SEG_W),
                            lambda b, c: (b * n * hb + jnp.maximum(chunk_of(c, rev) * hb - 1, 0), s))

    def next_spec(s, rev):
        return pl.BlockSpec((HALO, SEG_W),
                            lambda b, c: (b * n * hb + jnp.minimum((chunk_of(c, rev) + 1) * hb, n * hb - 1), s))

    def gcol_spec(rev):
        return pl.BlockSpec((CHUNK, LANES), lambda b, c: (b * n + chunk_of(c, rev), 0))

    def grow_spec(rev):
        return pl.BlockSpec((N_GATE_COLS, CHUNK), lambda b, c: (0, b * n + chunk_of(c, rev)))

    full = lambda shape: pl.BlockSpec(shape, lambda b, c: (0,) * len(shape))
    st_shape = (1, N_HEADS, HEAD_DIM, HEAD_DIM)
    nm_shape = (1, 2 * N_HEADS, HEAD_DIM)
    scratch = [pltpu.VMEM((N_HEADS, HEAD_DIM, HEAD_DIM), F32), pltpu.VMEM((N_HEADS, HEAD_DIM), F32),
               pltpu.VMEM((N_HEADS, HEAD_DIM), F32)]
    consts = [full((ML_CONV_W, 2 * MIX_W)), full((1, LANES)), full((N_GATE_COLS, CHUNK))]

    cb, nm = pl.pallas_call(
        functools.partial(_ml_state_kernel, n=n),
        grid=(B, n),
        in_specs=[seg(SEG_MK, True), prev_spec(SEG_MK, True), next_spec(SEG_MK, True), seg(SEG_MV, True),
                  gcol_spec(True), grow_spec(True)] + consts,
        out_specs=[pl.BlockSpec(st_shape, lambda b, c: (b * n + n - 1 - c, 0, 0, 0)),
                   pl.BlockSpec(nm_shape, lambda b, c: (b * n + n - 1 - c, 0, 0))],
        out_shape=[jax.ShapeDtypeStruct((B * n, N_HEADS, HEAD_DIM, HEAD_DIM), BF16),
                   jax.ShapeDtypeStruct((B * n, 2 * N_HEADS, HEAD_DIM), F32)],
        scratch_shapes=scratch,
        compiler_params=_cparams(("arbitrary", "arbitrary")),
        name="mlstm_state",
    )(proj, proj, proj, proj, gates, gates_t, cw, bias_col, bias_row)

    return pl.pallas_call(
        functools.partial(_ml_out_kernel, n=n),
        grid=(B, n),
        in_specs=[seg(SEG_MQ, False), prev_spec(SEG_MQ, False), next_spec(SEG_MQ, False),
                  seg(SEG_MK, False), prev_spec(SEG_MK, False), next_spec(SEG_MK, False),
                  seg(SEG_MV, False), seg(SEG_MO, False), gcol_spec(False), grow_spec(False),
                  pl.BlockSpec(st_shape, lambda b, c: (b * n + c, 0, 0, 0)),
                  pl.BlockSpec(nm_shape, lambda b, c: (b * n + c, 0, 0))]
        + consts + [full((N_HEADS, HEAD_DIM))],
        out_specs=pl.BlockSpec((CHUNK, MIX_W), lambda b, c: (b * n + c, 0)),
        out_shape=jax.ShapeDtypeStruct((T, MIX_W), BF16),
        scratch_shapes=scratch,
        compiler_params=_cparams(("arbitrary", "arbitrary")),
        name="mlstm_out",
    )(proj, proj, proj, proj, proj, proj, proj, proj, gates, gates_t, cb, nm,
      cw, bias_col, bias_row, norm_w.astype(F32))


def _merge_kernel(x_ref, oret_ref, ona_ref, oml_ref, ga_ref, gb_ref, gc_ref, wb_ref, bgb_ref, wout_ref,
                  n2w_ref, rw_ref, rb_ref, h_ref, hn_ref, route_ref):
    def gated(g_ref, o_ref, i):
        z = g_ref[...].astype(F32) + bgb_ref[i:i + 1, :]
        return jax.nn.sigmoid(z) * _dot(o_ref[...], wb_ref[i])

    merged = gated(ga_ref, oret_ref, 0) + gated(gb_ref, ona_ref, 1) + gated(gc_ref, oml_ref, 2)
    h = x_ref[...] + _dot(merged.astype(BF16), wout_ref[...])
    h_ref[...] = h
    hn = h * lax.rsqrt(jnp.mean(h * h, -1, keepdims=True) + EPS) * n2w_ref[...]
    hn_ref[...] = hn.astype(BF16)

    logits = _hp_dot(hn, rw_ref[...]) + rb_ref[...]
    lane_i = lax.broadcasted_iota(jnp.int32, logits.shape, 1)
    lane = lane_i.astype(F32)
    lane_grp = jnp.right_shift(lane_i - N_GROUPS, 3).astype(F32)
    gmask = lane_i < N_GROUPS
    lg = jnp.where(gmask, logits, NEG)
    mg = jnp.max(lg, -1, keepdims=True)
    p_top = 1.0 / jnp.sum(jnp.where(gmask, jnp.exp(lg - mg), 0.0), -1, keepdims=True)
    grp = jnp.min(jnp.where(lg == mg, lane, float(LANES)), -1, keepdims=True)
    emask = (lane_i >= N_GROUPS) & (lane_i < N_GROUPS + N_EXPERTS) & (lane_grp == grp)
    le = jnp.where(emask, logits, NEG)
    m1 = jnp.max(le, -1, keepdims=True)
    i1 = jnp.min(jnp.where(le == m1, lane, float(LANES)), -1, keepdims=True)
    le2 = jnp.where(lane == i1, NEG, le)
    m2 = jnp.max(le2, -1, keepdims=True)
    i2 = jnp.min(jnp.where(le2 == m2, lane, float(LANES)), -1, keepdims=True)
    e2 = jnp.exp(m2 - m1)
    gate1 = p_top / (1.0 + e2)
    gate2 = p_top * e2 / (1.0 + e2)
    route = jnp.where(lane_i == 0, i1 - N_GROUPS,
                      jnp.where(lane_i == 1, i2 - N_GROUPS,
                                jnp.where(lane_i == 2, gate1, jnp.where(lane_i == 3, gate2, 0.0))))
    route_ref[...] = route


def _merge(x2d, o_ret, o_na, o_ml, proj, wb, bgb, wout, n2w, rw, rb):
    T = x2d.shape[0]
    tm = 256
    full = lambda shape: pl.BlockSpec(shape, lambda i: (0,) * len(shape), pipeline_mode=pl.Buffered(1))
    row = lambda w: pl.BlockSpec((tm, w), lambda i: (i, 0))
    gate = lambda s: pl.BlockSpec((tm, D_MODEL), lambda i: (i, s // 2))
    return pl.pallas_call(
        _merge_kernel,
        grid=(T // tm,),
        in_specs=[row(D_MODEL), row(MIX_W), row(MIX_W), row(MIX_W), gate(SEG_GA), gate(SEG_GB), gate(SEG_GC),
                  full((N_BRANCH, MIX_W, D_MODEL)), full((N_BRANCH, D_MODEL)), full((D_MODEL, D_MODEL)),
                  full((1, D_MODEL)), full((D_MODEL, LANES)), full((1, LANES))],
        out_specs=[row(D_MODEL), row(D_MODEL), row(LANES)],
        out_shape=[jax.ShapeDtypeStruct((T, D_MODEL), F32), jax.ShapeDtypeStruct((T, D_MODEL), BF16),
                   jax.ShapeDtypeStruct((T, LANES), F32)],
        compiler_params=_cparams(("arbitrary",)),
        name="merge_router",
    )(x2d, o_ret, o_na, o_ml, proj, proj, proj, wb, bgb, wout, n2w, rw, rb)


def _ffn_kernel(blk_e_ref, n_used_ref, x_ref, wg_ref, wu_ref, wd_ref, y_ref):
    i = pl.program_id(0)

    @pl.when(i < n_used_ref[0])
    def _():
        x = x_ref[...]
        a = _dot(x, wg_ref[0])
        hid = (a * jax.nn.sigmoid(a)) * _dot(x, wu_ref[0])
        y_ref[...] = _dot(hid.astype(BF16), wd_ref[0])

    @pl.when(i >= n_used_ref[0])
    def _():
        y_ref[...] = jnp.zeros_like(y_ref)


def _expert_ffn(xb, blk_e, n_used, wg, wu, wd):
    n_blocks = xb.shape[0] // MOE_ROWS
    grid_spec = pltpu.PrefetchScalarGridSpec(
        num_scalar_prefetch=2,
        grid=(n_blocks,),
        in_specs=[
            pl.BlockSpec((MOE_ROWS, D_MODEL), lambda i, be, nu: (i, 0)),
            pl.BlockSpec((1, D_MODEL, D_EXPERT), lambda i, be, nu: (be[i], 0, 0)),
            pl.BlockSpec((1, D_MODEL, D_EXPERT), lambda i, be, nu: (be[i], 0, 0)),
            pl.BlockSpec((1, D_EXPERT, D_MODEL), lambda i, be, nu: (be[i], 0, 0)),
        ],
        out_specs=pl.BlockSpec((MOE_ROWS, D_MODEL), lambda i, be, nu: (i, 0)),
    )
    return pl.pallas_call(
        _ffn_kernel,
        grid_spec=grid_spec,
        out_shape=jax.ShapeDtypeStruct((n_blocks * MOE_ROWS, D_MODEL), F32),
        compiler_params=_cparams(("arbitrary",)),
        name="expert_ffn",
    )(blk_e, n_used, xb, wg, wu, wd)


def _moe(h, hn, route, wg, wu, wd):
    T = h.shape[0]
    n_assign = T * TOP_K_INNER
    flat_e = route[:, 0:2].astype(jnp.int32).reshape(-1)
    flat_w = route[:, 2:4].reshape(-1)
    flat_t = jnp.repeat(jnp.arange(T, dtype=jnp.int32), TOP_K_INNER)
    onehot = (flat_e[:, None] == jnp.arange(N_EXPERTS, dtype=jnp.int32)[None, :]).astype(jnp.int32)
    rank = jnp.take_along_axis(jnp.cumsum(onehot, 0) - onehot, flat_e[:, None], 1)[:, 0]
    counts = jnp.sum(onehot, 0)
    padded = (counts + MOE_ROWS - 1) // MOE_ROWS * MOE_ROWS
    pend = jnp.cumsum(padded)
    pstart = pend - padded
    dest = pstart[flat_e] + rank
    n_blocks = (n_assign + N_EXPERTS * (MOE_ROWS - 1) + MOE_ROWS - 1) // MOE_ROWS
    blk_e = jnp.minimum(jnp.searchsorted(pend, jnp.arange(n_blocks, dtype=jnp.int32) * MOE_ROWS, side='right'),
                        N_EXPERTS - 1).astype(jnp.int32)
    n_used = (pend[-1] // MOE_ROWS).astype(jnp.int32).reshape(1)
    src = jnp.zeros((n_blocks * MOE_ROWS,), jnp.int32).at[dest].set(flat_t)
    xb = hn[src]
    yb = _expert_ffn(xb, blk_e, n_used, wg, wu, wd)
    y = yb[dest] * flat_w[:, None]
    return h + y.reshape(T, TOP_K_INNER, D_MODEL).sum(1)


def _rope_tables(L):
    freqs = ROPE_BASE ** (-jnp.arange(0, HEAD_DIM, 2, dtype=F32) / HEAD_DIM)
    ang = jnp.arange(L, dtype=F32)[:, None] * freqs[None]
    cos, sin = jnp.cos(ang), jnp.sin(ang)
    return jnp.concatenate([cos, cos], -1), jnp.concatenate([-sin, sin], -1)


def _prep_layer(p):
    w_in = p["w_in"]
    sp = np.cumsum([MIX_W] * 11 + [2 * N_HEADS, 2 * N_HEADS] + [D_MODEL] * N_BRANCH)[:-1].tolist()
    (rq, rk, rv, rg, nq, nk, nv, mq, mk, mv, mo, mi, mf, ga, gb, gc) = jnp.split(w_in, sp, axis=-1)
    w_main = jnp.concatenate([ga, gb, gc, rq, rk, rv, rg, nq, nk, nv, mq, mk, mv, mo], -1).astype(BF16)
    wif = jnp.concatenate([mi, mf], -1)
    wif_pad = jnp.zeros((D_MODEL, LANES), BF16).at[:, :N_GATE_COLS].set(wif.astype(BF16))
    rw = jnp.zeros((D_MODEL, LANES), F32).at[:, :N_GROUPS].set(p["router_g_w"]) \
        .at[:, N_GROUPS:N_GROUPS + N_EXPERTS].set(p["router_e_w"])
    rb = jnp.zeros((1, LANES), F32).at[0, :N_GROUPS].set(p["router_g_b"]) \
        .at[0, N_GROUPS:N_GROUPS + N_EXPERTS].set(p["router_e_b"])
    return dict(
        nw=p["norm1_w"].astype(F32).reshape(1, D_MODEL), w_main=w_main, wif=wif_pad, wift=wif.T.astype(BF16),
        qn=p["na_q_norm"].astype(F32).reshape(1, HEAD_DIM), kn=p["na_k_norm"].astype(F32).reshape(1, HEAD_DIM),
        wb=p["w_branch"].astype(BF16), bgb=p["branch_gate_b"].astype(F32), wout=p["w_out"].astype(BF16),
        n2w=p["norm2_w"].astype(F32).reshape(1, D_MODEL), rw=rw, rb=rb,
        wg=p["exp_w_gate"].astype(BF16), wu=p["exp_w_up"].astype(BF16), wd=p["exp_w_down"].astype(BF16),
    )


def _layer(x2d, B, L, p, q, rope):
    proj, gates, gates_t = _inproj(x2d, L, q["nw"], q["w_main"], q["wif"], q["wift"], rope[0], rope[1],
                                   q["qn"], q["kn"])
    o_ret = _retention(proj, B, L, p["ret_decay"], p["ret_norm_w"])
    o_na = _neighborhood(proj, B, L, p["na_rpb"])
    o_ml = _mlstm(proj, gates, gates_t, B, L, p["ml_conv"], p["ml_igate_b"], p["ml_fgate_b"], p["ml_norm_w"])
    h, hn, route = _merge(x2d, o_ret, o_na, o_ml, proj, q["wb"], q["bgb"], q["wout"], q["n2w"], q["rw"], q["rb"])
    return _moe(h, hn, route, q["wg"], q["wu"], q["wd"])


_PARAM_NAMES = ("norm1_w", "w_in", "ret_decay", "ret_norm_w", "na_q_norm", "na_k_norm", "na_rpb", "ml_conv",
                "ml_igate_b", "ml_fgate_b", "ml_norm_w", "w_branch", "branch_gate_b", "w_out", "norm2_w",
                "router_g_w", "router_g_b", "router_e_w", "router_e_b", "exp_w_gate", "exp_w_up", "exp_w_down")


def _run(x, layers, preps):
    B, L, D = x.shape
    rope = _rope_tables(L)
    x2d = x.reshape(B * L, D)
    for p, q in zip(layers, preps):
        x2d = _layer(x2d, B, L, p, q, rope)
    return x2d.reshape(B, L, D)


def kernel(x_prompt, x_sample, norm1_w, w_in, ret_decay, ret_norm_w, na_q_norm, na_k_norm, na_rpb, ml_conv,
           ml_igate_b, ml_fgate_b, ml_norm_w, w_branch, branch_gate_b, w_out, norm2_w, router_g_w, router_g_b,
           router_e_w, router_e_b, exp_w_gate, exp_w_up, exp_w_down):
    stacked = (norm1_w, w_in, ret_decay, ret_norm_w, na_q_norm, na_k_norm, na_rpb, ml_conv, ml_igate_b,
               ml_fgate_b, ml_norm_w, w_branch, branch_gate_b, w_out, norm2_w, router_g_w, router_g_b,
               router_e_w, router_e_b, exp_w_gate, exp_w_up, exp_w_down)
    depth = w_in.shape[0]
    layers = [dict(zip(_PARAM_NAMES, (a[l] for a in stacked))) for l in range(depth)]
    preps = [_prep_layer(p) for p in layers]
    return (_run(x_prompt, layers, preps), _run(x_sample, layers, preps))
```

```python
import functools

import numpy as np
import jax
import jax.numpy as jnp
from jax import lax
from jax.experimental import pallas as pl
from jax.experimental.pallas import tpu as pltpu

D_MODEL = 2048
HEAD_DIM = 128
MIX_W = D_MODEL // 2
N_HEADS = MIX_W // HEAD_DIM
N_BRANCH = 3
CHUNK = 128
ROPE_BASE = 10000.0
GRID_W = 64
WIN_ROWS = 8
WIN_COLS = 16
ML_CONV_W = 3
N_GROUPS = 4
EXPERTS_PER_GROUP = 8
N_EXPERTS = N_GROUPS * EXPERTS_PER_GROUP
TOP_K_INNER = 2
D_EXPERT = D_MODEL // 2
EPS = 1e-6
NEG = -1e30

F32 = jnp.float32
BF16 = jnp.bfloat16

LANES = 128
SEG_W = MIX_W
SEG_GA, SEG_GB, SEG_GC = 0, 2, 4
SEG_RQ, SEG_RK, SEG_RV, SEG_RG = 6, 7, 8, 9
SEG_NQ, SEG_NK, SEG_NV = 10, 11, 12
SEG_MQ, SEG_MK, SEG_MV, SEG_MO = 13, 14, 15, 16
N_SEG = 17
N_GATE_COLS = 4 * N_HEADS

NA_QROWS = 2
NA_KBLKS = 5
NA_KW = NA_KBLKS * CHUNK
MOE_ROWS = 256

VMEM_LIMIT = 56 * 1024 * 1024


def _cparams(sem):
    return pltpu.CompilerParams(dimension_semantics=sem, vmem_limit_bytes=VMEM_LIMIT)


def _hs(h):
    return slice(h * HEAD_DIM, (h + 1) * HEAD_DIM)


def _dot(a, b):
    return jnp.dot(a, b, preferred_element_type=F32)


def _dot_nt(a, b):
    return lax.dot_general(a, b, (((1,), (1,)), ((), ())), preferred_element_type=F32)


def _dot_tn(a, b):
    return lax.dot_general(a, b, (((0,), (0,)), ((), ())), preferred_element_type=F32)


def _inproj_kernel(x_ref, nw_ref, w_ref, wif_ref, wift_ref, cos_ref, sin_ref, qn_ref, kn_ref,
                   proj_ref, g_ref, gt_ref, xn_ref):
    j = pl.program_id(1)

    @pl.when(j == 0)
    def _():
        x = x_ref[...]
        y = x * lax.rsqrt(jnp.mean(x * x, -1, keepdims=True) + EPS) * nw_ref[...]
        xn = y.astype(BF16)
        xn_ref[...] = xn
        g_ref[...] = _dot(xn, wif_ref[...])
        gt_ref[...] = _dot_nt(wift_ref[...], xn)

    acc = _dot(xn_ref[...], w_ref[...])

    is_rope = (j == SEG_RQ) | (j == SEG_RK)
    is_norm = (j == SEG_NQ) | (j == SEG_NK)

    @pl.when(is_rope)
    def _():
        scale = jnp.where(j == SEG_RK, HEAD_DIM ** -0.5, 1.0).astype(F32)
        cos = cos_ref[...]
        sin = sin_ref[...]
        for h in range(N_HEADS):
            xh = acc[:, _hs(h)]
            r = xh * cos + pltpu.roll(xh, HEAD_DIM // 2, 1) * sin
            proj_ref[:, _hs(h)] = (r * scale).astype(BF16)

    @pl.when(is_norm)
    def _():
        w = jnp.where(j == SEG_NQ, qn_ref[...] * (HEAD_DIM ** -0.5), kn_ref[...])
        for h in range(N_HEADS):
            xh = acc[:, _hs(h)]
            y = xh * lax.rsqrt(jnp.mean(xh * xh, -1, keepdims=True) + EPS) * w
            proj_ref[:, _hs(h)] = y.astype(BF16)

    @pl.when(jnp.logical_not(is_rope | is_norm))
    def _():
        proj_ref[...] = acc.astype(BF16)


def _inproj(x2d, L, nw, w_main, wif, wift, cos_t, sin_t, qn, kn):
    T = x2d.shape[0]
    tm = min(1024, L)
    nlt = L // tm
    return pl.pallas_call(
        _inproj_kernel,
        grid=(T // tm, N_SEG),
        in_specs=[
            pl.BlockSpec((tm, D_MODEL), lambda i, j: (i, 0)),
            pl.BlockSpec((1, D_MODEL), lambda i, j: (0, 0)),
            pl.BlockSpec((D_MODEL, SEG_W), lambda i, j: (0, j)),
            pl.BlockSpec((D_MODEL, LANES), lambda i, j: (0, 0)),
            pl.BlockSpec((N_GATE_COLS, D_MODEL), lambda i, j: (0, 0)),
            pl.BlockSpec((tm, HEAD_DIM), lambda i, j: (i % nlt, 0)),
            pl.BlockSpec((tm, HEAD_DIM), lambda i, j: (i % nlt, 0)),
            pl.BlockSpec((1, HEAD_DIM), lambda i, j: (0, 0)),
            pl.BlockSpec((1, HEAD_DIM), lambda i, j: (0, 0)),
        ],
        out_specs=[
            pl.BlockSpec((tm, SEG_W), lambda i, j: (i, j)),
            pl.BlockSpec((tm, LANES), lambda i, j: (i, 0)),
            pl.BlockSpec((N_GATE_COLS, tm), lambda i, j: (0, i)),
        ],
        out_shape=[
            jax.ShapeDtypeStruct((T, N_SEG * SEG_W), BF16),
            jax.ShapeDtypeStruct((T, LANES), F32),
            jax.ShapeDtypeStruct((N_GATE_COLS, T), F32),
        ],
        scratch_shapes=[pltpu.VMEM((tm, D_MODEL), BF16)],
        compiler_params=_cparams(("arbitrary", "arbitrary")),
        name="inproj",
    )(x2d, nw, w_main, wif, wift, cos_t, sin_t, qn, kn)


def _ret_state_kernel(k_ref, v_ref, kb_ref, cdb_ref, sb_ref, s_scr):
    c = pl.program_id(1)

    @pl.when(c == 0)
    def _():
        s_scr[...] = jnp.zeros_like(s_scr)

    sb_ref[0] = s_scr[...].astype(BF16)
    kk = (k_ref[...].astype(F32) * kb_ref[...]).astype(BF16)
    v = v_ref[...]
    for h in range(N_HEADS):
        kv = _dot_tn(kk[:, _hs(h)], v[:, _hs(h)])
        s_scr[h] = s_scr[h] * cdb_ref[h:h + 1, :] + kv


def _ret_out_kernel(q_ref, k_ref, v_ref, g_ref, sb_ref, dmat_ref, qf_ref, qb_ref, kf_ref, cdf_ref, nw_ref,
                    o_ref, s_scr):
    c = pl.program_id(1)

    @pl.when(c == 0)
    def _():
        s_scr[...] = jnp.zeros_like(s_scr)

    q = q_ref[...]
    k = k_ref[...]
    v = v_ref[...]
    qf32 = q.astype(F32)
    q_fwd = (qf32 * qf_ref[...]).astype(BF16)
    q_bwd = (qf32 * qb_ref[...]).astype(BF16)
    k_end = (k.astype(F32) * kf_ref[...]).astype(BF16)
    g = g_ref[...].astype(F32)
    for h in range(N_HEADS):
        hs = _hs(h)
        s = _dot_nt(q[:, hs], k[:, hs]) * dmat_ref[h]
        o = _dot(s.astype(BF16), v[:, hs])
        o = o + _dot(q_fwd[:, hs], s_scr[h].astype(BF16))
        o = o + _dot(q_bwd[:, hs], sb_ref[0, h])
        s_scr[h] = s_scr[h] * cdf_ref[h:h + 1, :] + _dot_tn(k_end[:, hs], v[:, hs])
        y = o * lax.rsqrt(jnp.mean(o * o, -1, keepdims=True) + EPS) * nw_ref[h:h + 1, :]
        gh = g[:, hs]
        o_ref[:, hs] = (y * (gh * jax.nn.sigmoid(gh))).astype(BF16)


def _retention(proj, B, L, ret_decay, ret_norm_w):
    n = L // CHUNK
    T = B * L
    lg = jax.nn.log_sigmoid(ret_decay.astype(F32))
    idx = jnp.arange(CHUNK, dtype=F32)
    diff = idx[:, None] - idx[None, :]
    dmat = jnp.where(diff >= 0, jnp.exp(jnp.maximum(diff, 0.0) * lg[0][:, None, None]),
                     jnp.exp(jnp.maximum(-diff, 0.0) * lg[1][:, None, None]))

    def lane_tab(e):
        return jnp.repeat(jnp.exp(e).T, HEAD_DIM, axis=1)

    qf_tab = lane_tab((idx + 1.0)[None, :] * lg[0][:, None])
    qb_tab = lane_tab((CHUNK - idx)[None, :] * lg[1][:, None])
    kf_tab = lane_tab((CHUNK - 1.0 - idx)[None, :] * lg[0][:, None])
    kb_tab = lane_tab(idx[None, :] * lg[1][:, None])
    cdf = jnp.broadcast_to(jnp.exp(CHUNK * lg[0])[:, None], (N_HEADS, HEAD_DIM))
    cdb = jnp.broadcast_to(jnp.exp(CHUNK * lg[1])[:, None], (N_HEADS, HEAD_DIM))

    def seg(s, rev=False):
        if rev:
            return pl.BlockSpec((CHUNK, SEG_W), lambda b, c: (b * n + n - 1 - c, s))
        return pl.BlockSpec((CHUNK, SEG_W), lambda b, c: (b * n + c, s))

    full = lambda shape: pl.BlockSpec(shape, lambda b, c: (0,) * len(shape))
    st_shape = (1, N_HEADS, HEAD_DIM, HEAD_DIM)

    sb = pl.pallas_call(
        _ret_state_kernel,
        grid=(B, n),
        in_specs=[seg(SEG_RK, True), seg(SEG_RV, True), full((CHUNK, MIX_W)), full((N_HEADS, HEAD_DIM))],
        out_specs=pl.BlockSpec(st_shape, lambda b, c: (b * n + n - 1 - c, 0, 0, 0)),
        out_shape=jax.ShapeDtypeStruct((B * n, N_HEADS, HEAD_DIM, HEAD_DIM), BF16),
        scratch_shapes=[pltpu.VMEM((N_HEADS, HEAD_DIM, HEAD_DIM), F32)],
        compiler_params=_cparams(("arbitrary", "arbitrary")),
        name="ret_state",
    )(proj, proj, kb_tab, cdb)

    return pl.pallas_call(
        _ret_out_kernel,
        grid=(B, n),
        in_specs=[seg(SEG_RQ), seg(SEG_RK), seg(SEG_RV), seg(SEG_RG),
                  pl.BlockSpec(st_shape, lambda b, c: (b * n + c, 0, 0, 0)),
                  full((N_HEADS, CHUNK, CHUNK)), full((CHUNK, MIX_W)), full((CHUNK, MIX_W)),
                  full((CHUNK, MIX_W)), full((N_HEADS, HEAD_DIM)), full((N_HEADS, HEAD_DIM))],
        out_specs=pl.BlockSpec((CHUNK, MIX_W), lambda b, c: (b * n + c, 0)),
        out_shape=jax.ShapeDtypeStruct((T, MIX_W), BF16),
        scratch_shapes=[pltpu.VMEM((N_HEADS, HEAD_DIM, HEAD_DIM), F32)],
        compiler_params=_cparams(("arbitrary", "arbitrary")),
        name="ret_out",
    )(proj, proj, proj, proj, sb, dmat, qf_tab, qb_tab, kf_tab, cdf, ret_norm_w.astype(F32))


def _na_index_tables(nqb):
    rows = NA_QROWS * nqb
    qi = np.arange(CHUNK)
    ki = np.arange(NA_KW)

    def one(qb):
        kb = NA_QROWS * int(np.clip(qb - 2, 0, nqb - NA_KBLKS))
        r = (NA_QROWS * qb + qi // GRID_W)[:, None]
        qc = (qi % GRID_W)[:, None]
        kr = (kb + ki // GRID_W)[None, :]
        kc = (ki % GRID_W)[None, :]
        rs = np.clip(r - WIN_ROWS // 2, 0, rows - WIN_ROWS)
        cs = np.clip(qc - WIN_COLS // 2, 0, GRID_W - WIN_COLS)
        valid = (kr >= rs) & (kr < rs + WIN_ROWS) & (kc >= cs) & (kc < cs + WIN_COLS)
        dr = np.clip(kr - r + WIN_ROWS - 1, 0, 2 * WIN_ROWS - 2)
        dc = np.clip(kc - qc + WIN_COLS - 1, 0, 2 * WIN_COLS - 2)
        return dr + 0 * dc, dc + 0 * dr, valid

    reps = [0, 1, 2, nqb - 2, nqb - 1]
    tabs = [one(qb) for qb in reps]
    for qb in range(2, nqb - 2):
        t = one(qb)
        assert all(np.array_equal(a, b) for a, b in zip(t, tabs[2]))
    return tuple(np.stack([t[i] for t in tabs]) for i in range(3))


def _na_kernel(q_ref, k0, k1, k2, k3, k4, v0, v1, v2, v3, v4, bias_ref, o_ref):
    k_refs = (k0, k1, k2, k3, k4)
    v_refs = (v0, v1, v2, v3, v4)
    for h in range(N_HEADS):
        hs = _hs(h)
        q = q_ref[:, hs]
        s = [_dot_nt(q, k_refs[j][:, hs]) + bias_ref[0, h, :, j * CHUNK:(j + 1) * CHUNK]
             for j in range(NA_KBLKS)]
        m = jnp.max(s[0], -1, keepdims=True)
        for j in range(1, NA_KBLKS):
            m = jnp.maximum(m, jnp.max(s[j], -1, keepdims=True))
        l = jnp.zeros_like(m)
        o = jnp.zeros((CHUNK, HEAD_DIM), F32)
        for j in range(NA_KBLKS):
            p = jnp.exp(s[j] - m)
            l = l + jnp.sum(p, -1, keepdims=True)
            o = o + _dot(p.astype(BF16), v_refs[j][:, hs])
        o_ref[:, hs] = (o / l).astype(BF16)


def _neighborhood(proj, B, L, rpb):
    nqb = L // CHUNK
    assert nqb >= NA_KBLKS and L % (GRID_W * NA_QROWS) == 0
    T = B * L
    dr, dc, valid = _na_index_tables(nqb)
    bias = jnp.where(jnp.asarray(valid)[:, None], rpb.astype(F32)[:, dr, dc].transpose(1, 0, 2, 3), NEG)

    def kspec(s, j):
        return pl.BlockSpec((CHUNK, SEG_W),
                            lambda b, qb: (b * nqb + jnp.clip(qb - 2, 0, nqb - NA_KBLKS) + j, s))

    def cls(qb):
        return jnp.where(qb < 2, qb, jnp.where(qb >= nqb - 2, qb - (nqb - 5), 2))

    return pl.pallas_call(
        _na_kernel,
        grid=(B, nqb),
        in_specs=[pl.BlockSpec((CHUNK, SEG_W), lambda b, qb: (b * nqb + qb, SEG_NQ))]
        + [kspec(SEG_NK, j) for j in range(NA_KBLKS)]
        + [kspec(SEG_NV, j) for j in range(NA_KBLKS)]
        + [pl.BlockSpec((1, N_HEADS, CHUNK, NA_KW), lambda b, qb: (cls(qb), 0, 0, 0))],
        out_specs=pl.BlockSpec((CHUNK, MIX_W), lambda b, qb: (b * nqb + qb, 0)),
        out_shape=jax.ShapeDtypeStruct((T, MIX_W), BF16),
        compiler_params=_cparams(("arbitrary", "arbitrary")),
        name="neighborhood",
    )(*([proj] * (1 + 2 * NA_KBLKS)), bias)


HALO = 16


def _log_sigmoid(x):
    return jnp.minimum(x, 0.0) - jnp.log1p(jnp.exp(-jnp.abs(x)))


def _conv_silu(x_ref, prev_ref, next_ref, w_ref, col0, c, n):
    x = x_ref[...].astype(F32)
    row = lax.broadcasted_iota(jnp.int32, x.shape, 0)
    prev_row = prev_ref[HALO - 1:HALO, :].astype(F32) * (c > 0).astype(F32)
    next_row = next_ref[0:1, :].astype(F32) * (c < n - 1).astype(F32)
    x_prev = jnp.where(row == 0, prev_row, pltpu.roll(x, 1, 0))
    x_next = jnp.where(row == CHUNK - 1, next_row, pltpu.roll(x, CHUNK - 1, 0))
    cs = slice(col0, col0 + MIX_W)
    y = x_prev * w_ref[0:1, cs] + x * w_ref[1:2, cs] + x_next * w_ref[2:3, cs]
    return y * jax.nn.sigmoid(y)


def _tri():
    r = lax.broadcasted_iota(jnp.int32, (CHUNK, CHUNK), 0)
    c = lax.broadcasted_iota(jnp.int32, (CHUNK, CHUNK), 1)
    return (c <= r).astype(F32), (c >= r).astype(F32)


def _hp_dot(a, b):
    return jnp.dot(a, b, preferred_element_type=F32, precision=lax.Precision.HIGHEST)


def _gate_forms(gc_ref, gr_ref, bc_ref, br_ref):
    low, up = _tri()
    g_col = gc_ref[...] + bc_ref[...]
    g_row = gr_ref[...] + br_ref[...]
    lf_col = _log_sigmoid(g_col)
    lf_row = _log_sigmoid(g_row)
    b_col = _hp_dot(low, lf_col)
    rb_col = _hp_dot(up, lf_col)
    b_row = _hp_dot(lf_row, up)
    rb_row = _hp_dot(lf_row, low)
    return g_col, g_row, b_col, rb_col, b_row, rb_row


def _ml_state_update(kc_h, v_h, a_col, g_tot, c_scr, n_scr, m_scr, h):
    m_p = m_scr[h:h + 1, 0:1]
    m_loc = jnp.max(a_col, 0, keepdims=True)
    kw = kc_h * jnp.exp(a_col - m_loc)
    c_loc = _dot_tn(kw.astype(BF16), v_h)
    n_loc = jnp.sum(kw, 0, keepdims=True)
    m_new = jnp.maximum(g_tot + m_p, m_loc)
    sp = jnp.exp(g_tot + m_p - m_new)
    sl = jnp.exp(m_loc - m_new)
    c_scr[h] = sp * c_scr[h] + sl * c_loc
    n_scr[h:h + 1, :] = sp * n_scr[h:h + 1, :] + sl * n_loc
    m_scr[h:h + 1, :] = jnp.broadcast_to(m_new, (1, HEAD_DIM))


def _ml_init(c, c_scr, n_scr, m_scr):
    @pl.when(c == 0)
    def _():
        c_scr[...] = jnp.zeros_like(c_scr)
        n_scr[...] = jnp.zeros_like(n_scr)
        m_scr[...] = jnp.full_like(m_scr, NEG)


def _ml_state_kernel(k_ref, kp_ref, kn_ref, v_ref, gc_ref, gr_ref, cw_ref, bc_ref, br_ref,
                     cb_ref, nm_ref, c_scr, n_scr, m_scr, *, n):
    step = pl.program_id(1)
    _ml_init(step, c_scr, n_scr, m_scr)
    c = n - 1 - step
    cb_ref[0] = c_scr[...].astype(BF16)
    nm_ref[0, 0:N_HEADS, :] = n_scr[...]
    nm_ref[0, N_HEADS:2 * N_HEADS, :] = m_scr[...]

    kc = _conv_silu(k_ref, kp_ref, kn_ref, cw_ref, MIX_W, c, n) * (HEAD_DIM ** -0.5)
    v = v_ref[...]
    g_col, _, _, rb_col, _, _ = _gate_forms(gc_ref, gr_ref, bc_ref, br_ref)
    for h in range(N_HEADS):
        ci = N_HEADS + h
        cf = 3 * N_HEADS + h
        g_tot = rb_col[0:1, cf:cf + 1]
        a_col = g_tot - rb_col[:, cf:cf + 1] + g_col[:, ci:ci + 1]
        _ml_state_update(kc[:, _hs(h)], v[:, _hs(h)], a_col, g_tot, c_scr, n_scr, m_scr, h)


def _ml_direction(qk, qc_h, q_bf, v_h, mask, d_col, d_row, ig_row, m_p, c_prev, n_prev):
    dmat = jnp.where(mask, d_col - d_row + ig_row, NEG)
    inter = d_col + m_p
    m_row = jnp.maximum(jnp.max(dmat, -1, keepdims=True), inter)
    s = qk * jnp.exp(dmat - m_row)
    e_int = jnp.exp(inter - m_row)
    num = _dot(s.astype(BF16), v_h) + e_int * _dot(q_bf, c_prev)
    den = jnp.sum(s, -1, keepdims=True) + e_int * jnp.sum(qc_h * n_prev, -1, keepdims=True)
    return num / jnp.maximum(jnp.abs(den), jnp.exp(-m_row))


def _ml_out_kernel(q_ref, qp_ref, qn_ref, k_ref, kp_ref, kn_ref, v_ref, o_ref_in, gc_ref, gr_ref,
                   cb_ref, nm_ref, cw_ref, bc_ref, br_ref, nw_ref,
                   out_ref, c_scr, n_scr, m_scr, *, n):
    c = pl.program_id(1)
    _ml_init(c, c_scr, n_scr, m_scr)
    qc = _conv_silu(q_ref, qp_ref, qn_ref, cw_ref, 0, c, n)
    kc = _conv_silu(k_ref, kp_ref, kn_ref, cw_ref, MIX_W, c, n) * (HEAD_DIM ** -0.5)
    q_bf = qc.astype(BF16)
    k_bf = kc.astype(BF16)
    v = v_ref[...]
    og = o_ref_in[...].astype(F32)
    g_col, g_row, b_col, rb_col, b_row, rb_row = _gate_forms(gc_ref, gr_ref, bc_ref, br_ref)
    r = lax.broadcasted_iota(jnp.int32, (CHUNK, CHUNK), 0)
    cc = lax.broadcasted_iota(jnp.int32, (CHUNK, CHUNK), 1)
    causal = cc <= r
    anti = cc >= r
    for h in range(N_HEADS):
        hs = _hs(h)
        v_h = v[:, hs]
        qk = _dot_nt(q_bf[:, hs], k_bf[:, hs])
        f0 = 2 * N_HEADS + h
        f1 = 3 * N_HEADS + h
        i0 = h
        i1 = N_HEADS + h
        h_f = _ml_direction(qk, qc[:, hs], q_bf[:, hs], v_h, causal,
                            b_col[:, f0:f0 + 1], b_row[f0:f0 + 1, :], g_row[i0:i0 + 1, :],
                            m_scr[h:h + 1, 0:1], c_scr[h].astype(BF16), n_scr[h:h + 1, :])
        h_b = _ml_direction(qk, qc[:, hs], q_bf[:, hs], v_h, anti,
                            rb_col[:, f1:f1 + 1], rb_row[f1:f1 + 1, :], g_row[i1:i1 + 1, :],
                            nm_ref[0, N_HEADS + h:N_HEADS + h + 1, 0:1], cb_ref[0, h], nm_ref[0, h:h + 1, :])
        g_tot = b_col[CHUNK - 1:CHUNK, f0:f0 + 1]
        a_col = g_tot - b_col[:, f0:f0 + 1] + g_col[:, i0:i0 + 1]
        _ml_state_update(kc[:, hs], v_h, a_col, g_tot, c_scr, n_scr, m_scr, h)
        o = h_f + h_b
        y = o * lax.rsqrt(jnp.mean(o * o, -1, keepdims=True) + EPS) * nw_ref[h:h + 1, :]
        out_ref[:, hs] = (y * jax.nn.sigmoid(og[:, hs])).astype(BF16)


def _mlstm(proj, gates, gates_t, B, L, conv_w, ig_b, fg_b, norm_w):
    n = L // CHUNK
    T = B * L
    hb = CHUNK // HALO
    bias = jnp.concatenate([ig_b.astype(F32).reshape(-1), fg_b.astype(F32).reshape(-1)])
    bias_col = jnp.zeros((1, LANES), F32).at[0, :N_GATE_COLS].set(bias)
    bias_row = jnp.broadcast_to(bias[:, None], (N_GATE_COLS, CHUNK))
    cw = conv_w.astype(F32)

    def seg(s, rev):
        if rev:
            return pl.BlockSpec((CHUNK, SEG_W), lambda b, c: (b * n + n - 1 - c, s))
        return pl.BlockSpec((CHUNK, SEG_W), lambda b, c: (b * n + c, s))

    def chunk_of(c, rev):
        return n - 1 - c if rev else c

    def prev_spec(s, rev):
        return pl.BlockSpec((HALO, SEG_W),
                            lambda b, c: (b * n * hb + jnp.maximum(chunk_of(c, rev) * hb - 1, 0), s))

    def next_spec(s, rev):
        return pl.BlockSpec((HALO, SEG_W),
                            lambda b, c: (b * n * hb + jnp.minimum((chunk_of(c, rev) + 1) * hb, n * hb - 1), s))

    def gcol_spec(rev):
        return pl.BlockSpec((CHUNK, LANES), lambda b, c: (b * n + chunk_of(c, rev), 0))

    def grow_spec(rev):
        return pl.BlockSpec((N_GATE_COLS, CHUNK), lambda b, c: (0, b * n + chunk_of(c, rev)))

    full = lambda shape: pl.BlockSpec(shape, lambda b, c: (0,) * len(shape))
    st_shape = (1, N_HEADS, HEAD_DIM, HEAD_DIM)
    nm_shape = (1, 2 * N_HEADS, HEAD_DIM)
    scratch = [pltpu.VMEM((N_HEADS, HEAD_DIM, HEAD_DIM), F32), pltpu.VMEM((N_HEADS, HEAD_DIM), F32),
               pltpu.VMEM((N_HEADS, HEAD_DIM), F32)]
    consts = [full((ML_CONV_W, 2 * MIX_W)), full((1, LANES)), full((N_GATE_COLS, CHUNK))]

    cb, nm = pl.pallas_call(
        functools.partial(_ml_state_kernel, n=n),
        grid=(B, n),
        in_specs=[seg(SEG_MK, True), prev_spec(SEG_MK, True), next_spec(SEG_MK, True), seg(SEG_MV, True),
                  gcol_spec(True), grow_spec(True)] + consts,
        out_specs=[pl.BlockSpec(st_shape, lambda b, c: (b * n + n - 1 - c, 0, 0, 0)),
                   pl.BlockSpec(nm_shape, lambda b, c: (b * n + n - 1 - c, 0, 0))],
        out_shape=[jax.ShapeDtypeStruct((B * n, N_HEADS, HEAD_DIM, HEAD_DIM), BF16),
                   jax.ShapeDtypeStruct((B * n, 2 * N_HEADS, HEAD_DIM), F32)],
        scratch_shapes=scratch,
        compiler_params=_cparams(("arbitrary", "arbitrary")),
        name="mlstm_state",
    )(proj, proj, proj, proj, gates, gates_t, cw, bias_col, bias_row)

    return pl.pallas_call(
        functools.partial(_ml_out_kernel, n=n),
        grid=(B, n),
        in_specs=[seg(SEG_MQ, False), prev_spec(SEG_MQ, False), next_spec(SEG_MQ, False),
                  seg(SEG_MK, False), prev_spec(SEG_MK, False), next_spec(SEG_MK, False),
                  seg(SEG_MV, False), seg(SEG_MO, False), gcol_spec(False), grow_spec(False),
                  pl.BlockSpec(st_shape, lambda b, c: (b * n + c, 0, 0, 0)),
                  pl.BlockSpec(nm_shape, lambda b, c: (b * n + c, 0, 0))]
        + consts + [full((N_HEADS, HEAD_DIM))],
        out_specs=pl.BlockSpec((CHUNK, MIX_W), lambda b, c: (b * n + c, 0)),
        out_shape=jax.ShapeDtypeStruct((T, MIX_W), BF16),
        scratch_shapes=scratch,
        compiler_params=_cparams(("arbitrary", "arbitrary")),
        name="mlstm_out",
    )(proj, proj, proj, proj, proj, proj, proj, proj, gates, gates_t, cb, nm,
      cw, bias_col, bias_row, norm_w.astype(F32))


def _merge_kernel(x_ref, oret_ref, ona_ref, oml_ref, ga_ref, gb_ref, gc_ref, wb_ref, bgb_ref, wout_ref,
                  n2w_ref, rw_ref, rb_ref, h_ref, hn_ref, route_ref, cnt_ref, cnt_scr):
    def gated(g_ref, o_ref, i):
        z = g_ref[...].astype(F32) + bgb_ref[i:i + 1, :]
        return jax.nn.sigmoid(z) * _dot(o_ref[...], wb_ref[i])

    merged = gated(ga_ref, oret_ref, 0) + gated(gb_ref, ona_ref, 1) + gated(gc_ref, oml_ref, 2)
    h = x_ref[...] + _dot(merged.astype(BF16), wout_ref[...])
    h_ref[...] = h
    hn = h * lax.rsqrt(jnp.mean(h * h, -1, keepdims=True) + EPS) * n2w_ref[...]
    hn_ref[...] = hn

    logits = _hp_dot(hn, rw_ref[...]) + rb_ref[...]
    lane_i = lax.broadcasted_iota(jnp.int32, logits.shape, 1)
    lane = lane_i.astype(F32)
    lane_grp = jnp.right_shift(lane_i - N_GROUPS, 3).astype(F32)
    gmask = lane_i < N_GROUPS
    lg = jnp.where(gmask, logits, NEG)
    mg = jnp.max(lg, -1, keepdims=True)
    p_top = 1.0 / jnp.sum(jnp.where(gmask, jnp.exp(lg - mg), 0.0), -1, keepdims=True)
    grp = jnp.min(jnp.where(lg == mg, lane, float(LANES)), -1, keepdims=True)
    emask = (lane_i >= N_GROUPS) & (lane_i < N_GROUPS + N_EXPERTS) & (lane_grp == grp)
    le = jnp.where(emask, logits, NEG)
    m1 = jnp.max(le, -1, keepdims=True)
    i1 = jnp.min(jnp.where(le == m1, lane, float(LANES)), -1, keepdims=True)
    le2 = jnp.where(lane == i1, NEG, le)
    m2 = jnp.max(le2, -1, keepdims=True)
    i2 = jnp.min(jnp.where(le2 == m2, lane, float(LANES)), -1, keepdims=True)
    e2 = jnp.exp(m2 - m1)
    gate1 = p_top / (1.0 + e2)
    gate2 = p_top * e2 / (1.0 + e2)

    @pl.when(pl.program_id(0) == 0)
    def _():
        cnt_scr[...] = jnp.zeros_like(cnt_scr)

    tm = logits.shape[0]
    hit1 = lane == i1
    hit2 = lane == i2
    onehot = jnp.where(hit1 | hit2, 1.0, 0.0)
    r_i = lax.broadcasted_iota(jnp.int32, (tm, tm), 0)
    c_i = lax.broadcasted_iota(jnp.int32, (tm, tm), 1)
    before = jnp.where(c_i < r_i, 1.0, 0.0).astype(BF16)
    prior = _dot(before, onehot.astype(BF16)) + cnt_scr[...]
    rank1 = jnp.sum(jnp.where(hit1, prior, 0.0), -1, keepdims=True)
    rank2 = jnp.sum(jnp.where(hit2, prior, 0.0), -1, keepdims=True)
    cnt = cnt_scr[...] + jnp.sum(onehot, 0, keepdims=True)
    cnt_scr[...] = cnt
    cnt_ref[...] = jnp.broadcast_to(cnt, cnt_ref.shape)

    cols = (i1 - N_GROUPS, i2 - N_GROUPS, gate1, gate2, rank1, rank2)
    route = jnp.zeros_like(logits)
    for ci, val in enumerate(cols):
        route = jnp.where(lane_i == ci, val, route)
    route_ref[...] = route


def _merge(x2d, o_ret, o_na, o_ml, proj, wb, bgb, wout, n2w, rw, rb):
    T = x2d.shape[0]
    tm = 256
    full = lambda shape: pl.BlockSpec(shape, lambda i: (0,) * len(shape), pipeline_mode=pl.Buffered(1))
    row = lambda w: pl.BlockSpec((tm, w), lambda i: (i, 0))
    gate = lambda s: pl.BlockSpec((tm, D_MODEL), lambda i: (i, s // 2))
    return pl.pallas_call(
        _merge_kernel,
        grid=(T // tm,),
        in_specs=[row(D_MODEL), row(MIX_W), row(MIX_W), row(MIX_W), gate(SEG_GA), gate(SEG_GB), gate(SEG_GC),
                  full((N_BRANCH, MIX_W, D_MODEL)), full((N_BRANCH, D_MODEL)), full((D_MODEL, D_MODEL)),
                  full((1, D_MODEL)), full((D_MODEL, LANES)), full((1, LANES))],
        out_specs=[row(D_MODEL), row(D_MODEL), row(LANES), pl.BlockSpec((8, LANES), lambda i: (0, 0))],
        out_shape=[jax.ShapeDtypeStruct((T, D_MODEL), F32), jax.ShapeDtypeStruct((T, D_MODEL), F32),
                   jax.ShapeDtypeStruct((T, LANES), F32), jax.ShapeDtypeStruct((8, LANES), F32)],
        scratch_shapes=[pltpu.VMEM((1, LANES), F32)],
        compiler_params=_cparams(("arbitrary",)),
        name="merge_router",
    )(x2d, o_ret, o_na, o_ml, proj, proj, proj, wb, bgb, wout, n2w, rw, rb)


def _row_copy(src_hbm, row, dst, r, sem):
    return pltpu.make_async_copy(src_hbm.at[pl.ds(row, 1), :], dst.at[pl.ds(r, 1), :], sem)


def _ffn_kernel(blk_e_ref, n_used_ref, src_ref, hn_hbm, wg_ref, wu_ref, wd_ref, y_ref, xbuf, sem):
    i = pl.program_id(0)
    n_used = n_used_ref[0]
    slot = i % 2

    def start_gather(blk, to_slot):
        def body(r, carry):
            _row_copy(hn_hbm, src_ref[blk * MOE_ROWS + r], xbuf.at[to_slot], r, sem.at[to_slot]).start()
            return carry
        lax.fori_loop(0, MOE_ROWS, body, 0, unroll=8)

    @pl.when(i == 0)
    def _():
        start_gather(0, 0)

    @pl.when(i < n_used)
    def _():
        pltpu.make_async_copy(hn_hbm.at[pl.ds(0, MOE_ROWS), :], xbuf.at[slot], sem.at[slot]).wait()

        @pl.when(i + 1 < n_used)
        def _():
            start_gather(i + 1, 1 - slot)

        x = xbuf[slot].astype(BF16)
        a = _dot(x, wg_ref[0])
        hid = (a * jax.nn.sigmoid(a)) * _dot(x, wu_ref[0])
        y_ref[...] = _dot(hid.astype(BF16), wd_ref[0])

    @pl.when(i >= n_used)
    def _():
        y_ref[...] = jnp.zeros_like(y_ref)


def _expert_ffn(hn, src, blk_e, n_used, wg, wu, wd):
    n_blocks = src.shape[0] // MOE_ROWS
    grid_spec = pltpu.PrefetchScalarGridSpec(
        num_scalar_prefetch=3,
        grid=(n_blocks,),
        in_specs=[
            pl.BlockSpec(memory_space=pl.ANY),
            pl.BlockSpec((1, D_MODEL, D_EXPERT), lambda i, be, nu, sr: (be[i], 0, 0)),
            pl.BlockSpec((1, D_MODEL, D_EXPERT), lambda i, be, nu, sr: (be[i], 0, 0)),
            pl.BlockSpec((1, D_EXPERT, D_MODEL), lambda i, be, nu, sr: (be[i], 0, 0)),
        ],
        out_specs=pl.BlockSpec((MOE_ROWS, D_MODEL), lambda i, be, nu, sr: (i, 0)),
        scratch_shapes=[pltpu.VMEM((2, MOE_ROWS, D_MODEL), F32), pltpu.SemaphoreType.DMA((2,))],
    )
    return pl.pallas_call(
        _ffn_kernel,
        grid_spec=grid_spec,
        out_shape=jax.ShapeDtypeStruct((n_blocks * MOE_ROWS, D_MODEL), F32),
        compiler_params=_cparams(("arbitrary",)),
        name="expert_ffn",
    )(blk_e, n_used, src, hn, wg, wu, wd)


COMBINE_ROWS = 256


def _combine_kernel(dest_ref, h_ref, route_ref, yb_hbm, out_ref, ybuf, sem):
    i = pl.program_id(0)
    slot = i % 2

    def start_gather(tile, to_slot):
        def body(r, carry):
            a = (tile * COMBINE_ROWS + r) * TOP_K_INNER
            for k in range(TOP_K_INNER):
                _row_copy(yb_hbm, dest_ref[a + k], ybuf.at[to_slot, k], r, sem.at[to_slot]).start()
            return carry
        lax.fori_loop(0, COMBINE_ROWS, body, 0, unroll=8)

    @pl.when(i == 0)
    def _():
        start_gather(0, 0)

    for k in range(TOP_K_INNER):
        pltpu.make_async_copy(yb_hbm.at[pl.ds(0, COMBINE_ROWS), :], ybuf.at[slot, k], sem.at[slot]).wait()

    @pl.when(i + 1 < pl.num_programs(0))
    def _():
        start_gather(i + 1, 1 - slot)

    route = route_ref[...]
    out_ref[...] = h_ref[...] + route[:, 2:3] * ybuf[slot, 0] + route[:, 3:4] * ybuf[slot, 1]


def _combine(h, route, yb, dest):
    T = h.shape[0]
    grid_spec = pltpu.PrefetchScalarGridSpec(
        num_scalar_prefetch=1,
        grid=(T // COMBINE_ROWS,),
        in_specs=[
            pl.BlockSpec((COMBINE_ROWS, D_MODEL), lambda i, d: (i, 0)),
            pl.BlockSpec((COMBINE_ROWS, LANES), lambda i, d: (i, 0)),
            pl.BlockSpec(memory_space=pl.ANY),
        ],
        out_specs=pl.BlockSpec((COMBINE_ROWS, D_MODEL), lambda i, d: (i, 0)),
        scratch_shapes=[pltpu.VMEM((2, TOP_K_INNER, COMBINE_ROWS, D_MODEL), F32), pltpu.SemaphoreType.DMA((2,))],
    )
    return pl.pallas_call(
        _combine_kernel,
        grid_spec=grid_spec,
        out_shape=jax.ShapeDtypeStruct((T, D_MODEL), F32),
        compiler_params=_cparams(("arbitrary",)),
        name="moe_combine",
    )(dest, h, route, yb)


def _moe(h, hn, route, counts, wg, wu, wd):
    T = h.shape[0]
    n_assign = T * TOP_K_INNER
    expert = route[:, 0:2].astype(jnp.int32)
    rank = route[:, 4:6].astype(jnp.int32)
    cnt = counts[0, N_GROUPS:N_GROUPS + N_EXPERTS].astype(jnp.int32)
    padded = (cnt + MOE_ROWS - 1) // MOE_ROWS * MOE_ROWS
    pend = jnp.cumsum(padded)
    pstart = pend - padded
    dest = (pstart[expert] + rank).reshape(-1)
    n_blocks = (n_assign + N_EXPERTS * (MOE_ROWS - 1) + MOE_ROWS - 1) // MOE_ROWS
    blk_start = jnp.arange(n_blocks, dtype=jnp.int32) * MOE_ROWS
    blk_e = jnp.minimum(jnp.sum((pend[None, :] <= blk_start[:, None]).astype(jnp.int32), 1), N_EXPERTS - 1)
    n_used = (pend[-1] // MOE_ROWS).astype(jnp.int32).reshape(1)
    flat_t = jnp.repeat(jnp.arange(T, dtype=jnp.int32), TOP_K_INNER)
    src = jnp.zeros((n_blocks * MOE_ROWS,), jnp.int32).at[dest].set(flat_t)
    yb = _expert_ffn(hn, src, blk_e, n_used, wg, wu, wd)
    return _combine(h, route, yb, dest)


def _rope_tables(L):
    freqs = ROPE_BASE ** (-jnp.arange(0, HEAD_DIM, 2, dtype=F32) / HEAD_DIM)
    ang = jnp.arange(L, dtype=F32)[:, None] * freqs[None]
    cos, sin = jnp.cos(ang), jnp.sin(ang)
    return jnp.concatenate([cos, cos], -1), jnp.concatenate([-sin, sin], -1)


def _prep_layer(p):
    w_in = p["w_in"]
    sp = np.cumsum([MIX_W] * 11 + [2 * N_HEADS, 2 * N_HEADS] + [D_MODEL] * N_BRANCH)[:-1].tolist()
    (rq, rk, rv, rg, nq, nk, nv, mq, mk, mv, mo, mi, mf, ga, gb, gc) = jnp.split(w_in, sp, axis=-1)
    w_main = jnp.concatenate([ga, gb, gc, rq, rk, rv, rg, nq, nk, nv, mq, mk, mv, mo], -1).astype(BF16)
    wif = jnp.concatenate([mi, mf], -1)
    wif_pad = jnp.zeros((D_MODEL, LANES), BF16).at[:, :N_GATE_COLS].set(wif.astype(BF16))
    rw = jnp.zeros((D_MODEL, LANES), F32).at[:, :N_GROUPS].set(p["router_g_w"]) \
        .at[:, N_GROUPS:N_GROUPS + N_EXPERTS].set(p["router_e_w"])
    rb = jnp.zeros((1, LANES), F32).at[0, :N_GROUPS].set(p["router_g_b"]) \
        .at[0, N_GROUPS:N_GROUPS + N_EXPERTS].set(p["router_e_b"])
    return dict(
        nw=p["norm1_w"].astype(F32).reshape(1, D_MODEL), w_main=w_main, wif=wif_pad, wift=wif.T.astype(BF16),
        qn=p["na_q_norm"].astype(F32).reshape(1, HEAD_DIM), kn=p["na_k_norm"].astype(F32).reshape(1, HEAD_DIM),
        wb=p["w_branch"].astype(BF16), bgb=p["branch_gate_b"].astype(F32), wout=p["w_out"].astype(BF16),
        n2w=p["norm2_w"].astype(F32).reshape(1, D_MODEL), rw=rw, rb=rb,
        wg=p["exp_w_gate"].astype(BF16), wu=p["exp_w_up"].astype(BF16), wd=p["exp_w_down"].astype(BF16),
    )


def _layer(x2d, B, L, p, q, rope):
    proj, gates, gates_t = _inproj(x2d, L, q["nw"], q["w_main"], q["wif"], q["wift"], rope[0], rope[1],
                                   q["qn"], q["kn"])
    o_ret = _retention(proj, B, L, p["ret_decay"], p["ret_norm_w"])
    o_na = _neighborhood(proj, B, L, p["na_rpb"])
    o_ml = _mlstm(proj, gates, gates_t, B, L, p["ml_conv"], p["ml_igate_b"], p["ml_fgate_b"], p["ml_norm_w"])
    h, hn, route, counts = _merge(x2d, o_ret, o_na, o_ml, proj, q["wb"], q["bgb"], q["wout"], q["n2w"], q["rw"],
                                  q["rb"])
    return _moe(h, hn, route, counts, q["wg"], q["wu"], q["wd"])


_PARAM_NAMES = ("norm1_w", "w_in", "ret_decay", "ret_norm_w", "na_q_norm", "na_k_norm", "na_rpb", "ml_conv",
                "ml_igate_b", "ml_fgate_b", "ml_norm_w", "w_branch", "branch_gate_b", "w_out", "norm2_w",
                "router_g_w", "router_g_b", "router_e_w", "router_e_b", "exp_w_gate", "exp_w_up", "exp_w_down")


def _run(x, layers, preps):
    B, L, D = x.shape
    rope = _rope_tables(L)
    x2d = x.reshape(B * L, D)
    for p, q in zip(layers, preps):
        x2d = _layer(x2d, B, L, p, q, rope)
    return x2d.reshape(B, L, D)


def kernel(x_prompt, x_sample, norm1_w, w_in, ret_decay, ret_norm_w, na_q_norm, na_k_norm, na_rpb, ml_conv,
           ml_igate_b, ml_fgate_b, ml_norm_w, w_branch, branch_gate_b, w_out, norm2_w, router_g_w, router_g_b,
           router_e_w, router_e_b, exp_w_gate, exp_w_up, exp_w_down):
    stacked = (norm1_w, w_in, ret_decay, ret_norm_w, na_q_norm, na_k_norm, na_rpb, ml_conv, ml_igate_b,
               ml_fgate_b, ml_norm_w, w_branch, branch_gate_b, w_out, norm2_w, router_g_w, router_g_b,
               router_e_w, router_e_b, exp_w_gate, exp_w_up, exp_w_down)
    depth = w_in.shape[0]
    layers = [dict(zip(_PARAM_NAMES, (a[l] for a in stacked))) for l in range(depth)]
    preps = [_prep_layer(p) for p in layers]
    return (_run(x_prompt, layers, preps), _run(x_sample, layers, preps))
```

```python
import functools

import numpy as np
import jax
import jax.numpy as jnp
from jax import lax
from jax.experimental import pallas as pl
from jax.experimental.pallas import tpu as pltpu

D_MODEL = 2048
HEAD_DIM = 128
MIX_W = D_MODEL // 2
N_HEADS = MIX_W // HEAD_DIM
N_BRANCH = 3
CHUNK = 128
ROPE_BASE = 10000.0
GRID_W = 64
WIN_ROWS = 8
WIN_COLS = 16
ML_CONV_W = 3
N_GROUPS = 4
EXPERTS_PER_GROUP = 8
N_EXPERTS = N_GROUPS * EXPERTS_PER_GROUP
TOP_K_INNER = 2
D_EXPERT = D_MODEL // 2
EPS = 1e-6
NEG = -1e30

F32 = jnp.float32
BF16 = jnp.bfloat16

LANES = 128
SEG_W = MIX_W
SEG_GA, SEG_GB, SEG_GC = 0, 2, 4
SEG_RQ, SEG_RK, SEG_RV, SEG_RG = 6, 7, 8, 9
SEG_NQ, SEG_NK, SEG_NV = 10, 11, 12
SEG_MQ, SEG_MK, SEG_MV, SEG_MO = 13, 14, 15, 16
N_SEG = 17
N_GATE_COLS = 4 * N_HEADS

NA_QROWS = 2
NA_KBLKS = 5
NA_KW = NA_KBLKS * CHUNK
MOE_ROWS = 256

VMEM_LIMIT = 56 * 1024 * 1024


def _cparams(sem, **kw):
    return pltpu.CompilerParams(dimension_semantics=sem, vmem_limit_bytes=VMEM_LIMIT, **kw)


def _sigmoid(z):
    return 0.5 * jnp.tanh(0.5 * z) + 0.5


def _hs(h):
    return slice(h * HEAD_DIM, (h + 1) * HEAD_DIM)


def _dot(a, b):
    return jnp.dot(a, b, preferred_element_type=F32)


def _dot_nt(a, b):
    return lax.dot_general(a, b, (((1,), (1,)), ((), ())), preferred_element_type=F32)


def _dot_tn(a, b):
    return lax.dot_general(a, b, (((0,), (0,)), ((), ())), preferred_element_type=F32)


def _inproj_kernel(x_ref, nw_ref, w_ref, wif_ref, wift_ref, cos_ref, sin_ref, qn_ref, kn_ref,
                   proj_ref, g_ref, gt_ref, xn_ref):
    j = pl.program_id(1)

    @pl.when(j == 0)
    def _():
        x = x_ref[...]
        y = x * lax.rsqrt(jnp.mean(x * x, -1, keepdims=True) + EPS) * nw_ref[...]
        xn = y.astype(BF16)
        xn_ref[...] = xn
        g_ref[...] = _dot(xn, wif_ref[...])
        gt_ref[...] = _dot_nt(wift_ref[...], xn)

    acc = _dot(xn_ref[...], w_ref[...])

    is_rope = (j == SEG_RQ) | (j == SEG_RK)
    is_norm = (j == SEG_NQ) | (j == SEG_NK)

    @pl.when(is_rope)
    def _():
        scale = jnp.where(j == SEG_RK, HEAD_DIM ** -0.5, 1.0).astype(F32)
        cos = cos_ref[...]
        sin = sin_ref[...]
        for h in range(N_HEADS):
            xh = acc[:, _hs(h)]
            r = xh * cos + pltpu.roll(xh, HEAD_DIM // 2, 1) * sin
            proj_ref[:, _hs(h)] = (r * scale).astype(BF16)

    @pl.when(is_norm)
    def _():
        w = jnp.where(j == SEG_NQ, qn_ref[...] * (HEAD_DIM ** -0.5), kn_ref[...])
        for h in range(N_HEADS):
            xh = acc[:, _hs(h)]
            y = xh * lax.rsqrt(jnp.mean(xh * xh, -1, keepdims=True) + EPS) * w
            proj_ref[:, _hs(h)] = y.astype(BF16)

    @pl.when(jnp.logical_not(is_rope | is_norm))
    def _():
        proj_ref[...] = acc.astype(BF16)


def _inproj(x2d, L, nw, w_main, wif, wift, cos_t, sin_t, qn, kn):
    T = x2d.shape[0]
    tm = min(1024, L)
    nlt = L // tm
    return pl.pallas_call(
        _inproj_kernel,
        grid=(T // tm, N_SEG),
        in_specs=[
            pl.BlockSpec((tm, D_MODEL), lambda i, j: (i, 0)),
            pl.BlockSpec((1, D_MODEL), lambda i, j: (0, 0)),
            pl.BlockSpec((D_MODEL, SEG_W), lambda i, j: (0, j)),
            pl.BlockSpec((D_MODEL, LANES), lambda i, j: (0, 0)),
            pl.BlockSpec((N_GATE_COLS, D_MODEL), lambda i, j: (0, 0)),
            pl.BlockSpec((tm, HEAD_DIM), lambda i, j: (i % nlt, 0)),
            pl.BlockSpec((tm, HEAD_DIM), lambda i, j: (i % nlt, 0)),
            pl.BlockSpec((1, HEAD_DIM), lambda i, j: (0, 0)),
            pl.BlockSpec((1, HEAD_DIM), lambda i, j: (0, 0)),
        ],
        out_specs=[
            pl.BlockSpec((tm, SEG_W), lambda i, j: (i, j)),
            pl.BlockSpec((tm, LANES), lambda i, j: (i, 0)),
            pl.BlockSpec((N_GATE_COLS, tm), lambda i, j: (0, i)),
        ],
        out_shape=[
            jax.ShapeDtypeStruct((T, N_SEG * SEG_W), BF16),
            jax.ShapeDtypeStruct((T, LANES), F32),
            jax.ShapeDtypeStruct((N_GATE_COLS, T), F32),
        ],
        scratch_shapes=[pltpu.VMEM((tm, D_MODEL), BF16)],
        compiler_params=_cparams(("arbitrary", "arbitrary")),
        name="inproj",
    )(x2d, nw, w_main, wif, wift, cos_t, sin_t, qn, kn)


def _ret_state_kernel(k_ref, v_ref, kb_ref, cdb_ref, sb_ref, s_scr):
    c = pl.program_id(1)

    @pl.when(c == 0)
    def _():
        s_scr[...] = jnp.zeros_like(s_scr)

    sb_ref[0] = s_scr[...].astype(BF16)
    kk = (k_ref[...].astype(F32) * kb_ref[...]).astype(BF16)
    v = v_ref[...]
    for h in range(N_HEADS):
        kv = _dot_tn(kk[:, _hs(h)], v[:, _hs(h)])
        s_scr[h] = s_scr[h] * cdb_ref[h:h + 1, :] + kv


def _ret_out_kernel(q_ref, k_ref, v_ref, g_ref, sb_ref, dmat_ref, qf_ref, qb_ref, kf_ref, cdf_ref, nw_ref,
                    o_ref, s_scr):
    c = pl.program_id(1)

    @pl.when(c == 0)
    def _():
        s_scr[...] = jnp.zeros_like(s_scr)

    q = q_ref[...]
    k = k_ref[...]
    v = v_ref[...]
    qf32 = q.astype(F32)
    q_fwd = (qf32 * qf_ref[...]).astype(BF16)
    q_bwd = (qf32 * qb_ref[...]).astype(BF16)
    k_end = (k.astype(F32) * kf_ref[...]).astype(BF16)
    g = g_ref[...].astype(F32)
    for h in range(N_HEADS):
        hs = _hs(h)
        s = _dot_nt(q[:, hs], k[:, hs]) * dmat_ref[h]
        o = _dot(s.astype(BF16), v[:, hs])
        o = o + _dot(q_fwd[:, hs], s_scr[h].astype(BF16))
        o = o + _dot(q_bwd[:, hs], sb_ref[0, h])
        s_scr[h] = s_scr[h] * cdf_ref[h:h + 1, :] + _dot_tn(k_end[:, hs], v[:, hs])
        y = o * lax.rsqrt(jnp.mean(o * o, -1, keepdims=True) + EPS) * nw_ref[h:h + 1, :]
        gh = g[:, hs]
        o_ref[:, hs] = (y * (gh * _sigmoid(gh))).astype(BF16)


def _retention(proj, B, L, ret_decay, ret_norm_w):
    n = L // CHUNK
    T = B * L
    lg = jax.nn.log_sigmoid(ret_decay.astype(F32))
    idx = jnp.arange(CHUNK, dtype=F32)
    diff = idx[:, None] - idx[None, :]
    dmat = jnp.where(diff >= 0, jnp.exp(jnp.maximum(diff, 0.0) * lg[0][:, None, None]),
                     jnp.exp(jnp.maximum(-diff, 0.0) * lg[1][:, None, None]))

    def lane_tab(e):
        return jnp.repeat(jnp.exp(e).T, HEAD_DIM, axis=1)

    qf_tab = lane_tab((idx + 1.0)[None, :] * lg[0][:, None])
    qb_tab = lane_tab((CHUNK - idx)[None, :] * lg[1][:, None])
    kf_tab = lane_tab((CHUNK - 1.0 - idx)[None, :] * lg[0][:, None])
    kb_tab = lane_tab(idx[None, :] * lg[1][:, None])
    cdf = jnp.broadcast_to(jnp.exp(CHUNK * lg[0])[:, None], (N_HEADS, HEAD_DIM))
    cdb = jnp.broadcast_to(jnp.exp(CHUNK * lg[1])[:, None], (N_HEADS, HEAD_DIM))

    def seg(s, rev=False):
        if rev:
            return pl.BlockSpec((CHUNK, SEG_W), lambda b, c: (b * n + n - 1 - c, s))
        return pl.BlockSpec((CHUNK, SEG_W), lambda b, c: (b * n + c, s))

    full = lambda shape: pl.BlockSpec(shape, lambda b, c: (0,) * len(shape))
    st_shape = (1, N_HEADS, HEAD_DIM, HEAD_DIM)

    sb = pl.pallas_call(
        _ret_state_kernel,
        grid=(B, n),
        in_specs=[seg(SEG_RK, True), seg(SEG_RV, True), full((CHUNK, MIX_W)), full((N_HEADS, HEAD_DIM))],
        out_specs=pl.BlockSpec(st_shape, lambda b, c: (b * n + n - 1 - c, 0, 0, 0)),
        out_shape=jax.ShapeDtypeStruct((B * n, N_HEADS, HEAD_DIM, HEAD_DIM), BF16),
        scratch_shapes=[pltpu.VMEM((N_HEADS, HEAD_DIM, HEAD_DIM), F32)],
        compiler_params=_cparams(("arbitrary", "arbitrary")),
        name="ret_state",
    )(proj, proj, kb_tab, cdb)

    return pl.pallas_call(
        _ret_out_kernel,
        grid=(B, n),
        in_specs=[seg(SEG_RQ), seg(SEG_RK), seg(SEG_RV), seg(SEG_RG),
                  pl.BlockSpec(st_shape, lambda b, c: (b * n + c, 0, 0, 0)),
                  full((N_HEADS, CHUNK, CHUNK)), full((CHUNK, MIX_W)), full((CHUNK, MIX_W)),
                  full((CHUNK, MIX_W)), full((N_HEADS, HEAD_DIM)), full((N_HEADS, HEAD_DIM))],
        out_specs=pl.BlockSpec((CHUNK, MIX_W), lambda b, c: (b * n + c, 0)),
        out_shape=jax.ShapeDtypeStruct((T, MIX_W), BF16),
        scratch_shapes=[pltpu.VMEM((N_HEADS, HEAD_DIM, HEAD_DIM), F32)],
        compiler_params=_cparams(("arbitrary", "arbitrary")),
        name="ret_out",
    )(proj, proj, proj, proj, sb, dmat, qf_tab, qb_tab, kf_tab, cdf, ret_norm_w.astype(F32))


def _na_index_tables(nqb):
    rows = NA_QROWS * nqb
    qi = np.arange(CHUNK)
    ki = np.arange(NA_KW)

    def one(qb):
        kb = NA_QROWS * int(np.clip(qb - 2, 0, nqb - NA_KBLKS))
        r = (NA_QROWS * qb + qi // GRID_W)[:, None]
        qc = (qi % GRID_W)[:, None]
        kr = (kb + ki // GRID_W)[None, :]
        kc = (ki % GRID_W)[None, :]
        rs = np.clip(r - WIN_ROWS // 2, 0, rows - WIN_ROWS)
        cs = np.clip(qc - WIN_COLS // 2, 0, GRID_W - WIN_COLS)
        valid = (kr >= rs) & (kr < rs + WIN_ROWS) & (kc >= cs) & (kc < cs + WIN_COLS)
        dr = np.clip(kr - r + WIN_ROWS - 1, 0, 2 * WIN_ROWS - 2)
        dc = np.clip(kc - qc + WIN_COLS - 1, 0, 2 * WIN_COLS - 2)
        return dr + 0 * dc, dc + 0 * dr, valid

    reps = [0, 1, 2, nqb - 2, nqb - 1]
    tabs = [one(qb) for qb in reps]
    for qb in range(2, nqb - 2):
        t = one(qb)
        assert all(np.array_equal(a, b) for a, b in zip(t, tabs[2]))
    return tuple(np.stack([t[i] for t in tabs]) for i in range(3))


def _na_kernel(q_ref, k0, k1, k2, k3, k4, v0, v1, v2, v3, v4, bias_ref, o_ref):
    k_refs = (k0, k1, k2, k3, k4)
    v_refs = (v0, v1, v2, v3, v4)
    for h in range(N_HEADS):
        hs = _hs(h)
        q = q_ref[:, hs]
        s = [_dot_nt(q, k_refs[j][:, hs]) + bias_ref[0, h, :, j * CHUNK:(j + 1) * CHUNK]
             for j in range(NA_KBLKS)]
        m = jnp.max(s[0], -1, keepdims=True)
        for j in range(1, NA_KBLKS):
            m = jnp.maximum(m, jnp.max(s[j], -1, keepdims=True))
        l = jnp.zeros_like(m)
        o = jnp.zeros((CHUNK, HEAD_DIM), F32)
        for j in range(NA_KBLKS):
            p = jnp.exp(s[j] - m)
            l = l + jnp.sum(p, -1, keepdims=True)
            o = o + _dot(p.astype(BF16), v_refs[j][:, hs])
        o_ref[:, hs] = (o / l).astype(BF16)


def _neighborhood(proj, B, L, rpb):
    nqb = L // CHUNK
    assert nqb >= NA_KBLKS and L % (GRID_W * NA_QROWS) == 0
    T = B * L
    dr, dc, valid = _na_index_tables(nqb)
    n_cls = dr.shape[0]
    kr_n = NA_KW // GRID_W
    dr6 = dr.reshape(n_cls, NA_QROWS, GRID_W, kr_n, GRID_W)
    dc6 = dc.reshape(n_cls, NA_QROWS, GRID_W, kr_n, GRID_W)
    dr_s = dr6[:, :, 0, :, 0]
    dc_s = dc6[0, 0, :, 0, :]
    assert np.array_equal(dr6, np.broadcast_to(dr_s[:, :, None, :, None], dr6.shape))
    assert np.array_equal(dc6, np.broadcast_to(dc_s[None, None, :, None, :], dc6.shape))
    sel_r = (dr_s[..., None] == np.arange(2 * WIN_ROWS - 1)).astype(np.float32)
    sel_c = (np.arange(2 * WIN_COLS - 1)[:, None, None] == dc_s[None]).astype(np.float32)
    hp = lax.Precision.HIGHEST
    t = jnp.einsum('cajr,hrd->cajhd', sel_r, rpb.astype(F32), precision=hp)
    bias = jnp.einsum('cajhd,dqk->chaqjk', t, sel_c, precision=hp).reshape(n_cls, N_HEADS, CHUNK, NA_KW)
    bias = jnp.where(jnp.asarray(valid)[:, None], bias, NEG)

    def kspec(s, j):
        return pl.BlockSpec((CHUNK, SEG_W),
                            lambda b, qb: (b * nqb + jnp.clip(qb - 2, 0, nqb - NA_KBLKS) + j, s))

    def cls(qb):
        return jnp.where(qb < 2, qb, jnp.where(qb >= nqb - 2, qb - (nqb - 5), 2))

    return pl.pallas_call(
        _na_kernel,
        grid=(B, nqb),
        in_specs=[pl.BlockSpec((CHUNK, SEG_W), lambda b, qb: (b * nqb + qb, SEG_NQ))]
        + [kspec(SEG_NK, j) for j in range(NA_KBLKS)]
        + [kspec(SEG_NV, j) for j in range(NA_KBLKS)]
        + [pl.BlockSpec((1, N_HEADS, CHUNK, NA_KW), lambda b, qb: (cls(qb), 0, 0, 0))],
        out_specs=pl.BlockSpec((CHUNK, MIX_W), lambda b, qb: (b * nqb + qb, 0)),
        out_shape=jax.ShapeDtypeStruct((T, MIX_W), BF16),
        compiler_params=_cparams(("arbitrary", "arbitrary")),
        name="neighborhood",
    )(*([proj] * (1 + 2 * NA_KBLKS)), bias)


HALO = 16


def _log_sigmoid(x):
    return jnp.minimum(x, 0.0) - jnp.log1p(jnp.exp(-jnp.abs(x)))


def _conv_silu(x_ref, prev_ref, next_ref, w_ref, col0, c, n):
    x = x_ref[...].astype(F32)
    row = lax.broadcasted_iota(jnp.int32, x.shape, 0)
    prev_row = prev_ref[HALO - 1:HALO, :].astype(F32) * (c > 0).astype(F32)
    next_row = next_ref[0:1, :].astype(F32) * (c < n - 1).astype(F32)
    x_prev = jnp.where(row == 0, prev_row, pltpu.roll(x, 1, 0))
    x_next = jnp.where(row == CHUNK - 1, next_row, pltpu.roll(x, CHUNK - 1, 0))
    cs = slice(col0, col0 + MIX_W)
    y = x_prev * w_ref[0:1, cs] + x * w_ref[1:2, cs] + x_next * w_ref[2:3, cs]
    return y * _sigmoid(y)


def _tri():
    r = lax.broadcasted_iota(jnp.int32, (CHUNK, CHUNK), 0)
    c = lax.broadcasted_iota(jnp.int32, (CHUNK, CHUNK), 1)
    return (c <= r).astype(F32), (c >= r).astype(F32)


def _hp_dot(a, b):
    return jnp.dot(a, b, preferred_element_type=F32, precision=lax.Precision.HIGHEST)


def _gate_forms(gc_ref, gr_ref, bc_ref, br_ref):
    low, up = _tri()
    g_col = gc_ref[...] + bc_ref[...]
    g_row = gr_ref[...] + br_ref[...]
    lf_col = _log_sigmoid(g_col)
    lf_row = _log_sigmoid(g_row)
    b_col = _hp_dot(low, lf_col)
    rb_col = _hp_dot(up, lf_col)
    b_row = _hp_dot(lf_row, up)
    rb_row = _hp_dot(lf_row, low)
    return g_col, g_row, b_col, rb_col, b_row, rb_row


def _ml_state_update(kc_h, v_h, a_col, g_tot, c_scr, n_scr, m_scr, h):
    m_p = m_scr[h:h + 1, 0:1]
    m_loc = jnp.max(a_col, 0, keepdims=True)
    kw = kc_h * jnp.exp(a_col - m_loc)
    c_loc = _dot_tn(kw.astype(BF16), v_h)
    n_loc = jnp.sum(kw, 0, keepdims=True)
    m_new = jnp.maximum(g_tot + m_p, m_loc)
    sp = jnp.exp(g_tot + m_p - m_new)
    sl = jnp.exp(m_loc - m_new)
    c_scr[h] = sp * c_scr[h] + sl * c_loc
    n_scr[h:h + 1, :] = sp * n_scr[h:h + 1, :] + sl * n_loc
    m_scr[h:h + 1, :] = jnp.broadcast_to(m_new, (1, HEAD_DIM))


def _ml_init(c, c_scr, n_scr, m_scr):
    @pl.when(c == 0)
    def _():
        c_scr[...] = jnp.zeros_like(c_scr)
        n_scr[...] = jnp.zeros_like(n_scr)
        m_scr[...] = jnp.full_like(m_scr, NEG)


def _ml_state_kernel(k_ref, kp_ref, kn_ref, v_ref, gc_ref, gr_ref, cw_ref, bc_ref, br_ref,
                     cb_ref, nm_ref, c_scr, n_scr, m_scr, *, n):
    step = pl.program_id(1)
    _ml_init(step, c_scr, n_scr, m_scr)
    c = n - 1 - step
    cb_ref[0] = c_scr[...].astype(BF16)
    nm_ref[0, 0:N_HEADS, :] = n_scr[...]
    nm_ref[0, N_HEADS:2 * N_HEADS, :] = m_scr[...]

    kc = _conv_silu(k_ref, kp_ref, kn_ref, cw_ref, MIX_W, c, n) * (HEAD_DIM ** -0.5)
    v = v_ref[...]
    g_col, _, _, rb_col, _, _ = _gate_forms(gc_ref, gr_ref, bc_ref, br_ref)
    for h in range(N_HEADS):
        ci = N_HEADS + h
        cf = 3 * N_HEADS + h
        g_tot = rb_col[0:1, cf:cf + 1]
        a_col = g_tot - rb_col[:, cf:cf + 1] + g_col[:, ci:ci + 1]
        _ml_state_update(kc[:, _hs(h)], v[:, _hs(h)], a_col, g_tot, c_scr, n_scr, m_scr, h)


def _ml_direction(qk, qc_h, q_bf, v_h, mask, d_col, d_row, ig_row, m_p, c_prev, n_prev):
    dmat = jnp.where(mask, d_col - d_row + ig_row, NEG)
    inter = d_col + m_p
    m_row = jnp.maximum(jnp.max(dmat, -1, keepdims=True), inter)
    s = qk * jnp.exp(dmat - m_row)
    e_int = jnp.exp(inter - m_row)
    num = _dot(s.astype(BF16), v_h) + e_int * _dot(q_bf, c_prev)
    den = jnp.sum(s, -1, keepdims=True) + e_int * jnp.sum(qc_h * n_prev, -1, keepdims=True)
    return num / jnp.maximum(jnp.abs(den), jnp.exp(-m_row))


def _ml_out_kernel(q_ref, qp_ref, qn_ref, k_ref, kp_ref, kn_ref, v_ref, o_ref_in, gc_ref, gr_ref,
                   cb_ref, nm_ref, cw_ref, bc_ref, br_ref, nw_ref,
                   out_ref, c_scr, n_scr, m_scr, *, n):
    c = pl.program_id(1)
    _ml_init(c, c_scr, n_scr, m_scr)
    qc = _conv_silu(q_ref, qp_ref, qn_ref, cw_ref, 0, c, n)
    kc = _conv_silu(k_ref, kp_ref, kn_ref, cw_ref, MIX_W, c, n) * (HEAD_DIM ** -0.5)
    q_bf = qc.astype(BF16)
    k_bf = kc.astype(BF16)
    v = v_ref[...]
    og = o_ref_in[...].astype(F32)
    g_col, g_row, b_col, rb_col, b_row, rb_row = _gate_forms(gc_ref, gr_ref, bc_ref, br_ref)
    r = lax.broadcasted_iota(jnp.int32, (CHUNK, CHUNK), 0)
    cc = lax.broadcasted_iota(jnp.int32, (CHUNK, CHUNK), 1)
    causal = cc <= r
    anti = cc >= r
    for h in range(N_HEADS):
        hs = _hs(h)
        v_h = v[:, hs]
        qk = _dot_nt(q_bf[:, hs], k_bf[:, hs])
        f0 = 2 * N_HEADS + h
        f1 = 3 * N_HEADS + h
        i0 = h
        i1 = N_HEADS + h
        h_f = _ml_direction(qk, qc[:, hs], q_bf[:, hs], v_h, causal,
                            b_col[:, f0:f0 + 1], b_row[f0:f0 + 1, :], g_row[i0:i0 + 1, :],
                            m_scr[h:h + 1, 0:1], c_scr[h].astype(BF16), n_scr[h:h + 1, :])
        h_b = _ml_direction(qk, qc[:, hs], q_bf[:, hs], v_h, anti,
                            rb_col[:, f1:f1 + 1], rb_row[f1:f1 + 1, :], g_row[i1:i1 + 1, :],
                            nm_ref[0, N_HEADS + h:N_HEADS + h + 1, 0:1], cb_ref[0, h], nm_ref[0, h:h + 1, :])
        g_tot = b_col[CHUNK - 1:CHUNK, f0:f0 + 1]
        a_col = g_tot - b_col[:, f0:f0 + 1] + g_col[:, i0:i0 + 1]
        _ml_state_update(kc[:, hs], v_h, a_col, g_tot, c_scr, n_scr, m_scr, h)
        o = h_f + h_b
        y = o * lax.rsqrt(jnp.mean(o * o, -1, keepdims=True) + EPS) * nw_ref[h:h + 1, :]
        out_ref[:, hs] = (y * _sigmoid(og[:, hs])).astype(BF16)


def _mlstm(proj, gates, gates_t, B, L, conv_w, ig_b, fg_b, norm_w):
    n = L // CHUNK
    T = B * L
    hb = CHUNK // HALO
    bias = jnp.concatenate([ig_b.astype(F32).reshape(-1), fg_b.astype(F32).reshape(-1)])
    bias_col = jnp.zeros((1, LANES), F32).at[0, :N_GATE_COLS].set(bias)
    bias_row = jnp.broadcast_to(bias[:, None], (N_GATE_COLS, CHUNK))
    cw = conv_w.astype(F32)

    def seg(s, rev):
        if rev:
            return pl.BlockSpec((CHUNK, SEG_W), lambda b, c: (b * n + n - 1 - c, s))
        return pl.BlockSpec((CHUNK, SEG_W), lambda b, c: (b * n + c, s))

    def chunk_of(c, rev):
        return n - 1 - c if rev else c

    def prev_spec(s, rev):
        return pl.BlockSpec((HALO, SEG_W),
                            lambda b, c: (b * n * hb + jnp.maximum(chunk_of(c, rev) * hb - 1, 0), s))

    def next_spec(s, rev):
        return pl.BlockSpec((HALO, SEG_W),
                            lambda b, c: (b * n * hb + jnp.minimum((chunk_of(c, rev) + 1) * hb, n * hb - 1), s))

    def gcol_spec(rev):
        return pl.BlockSpec((CHUNK, LANES), lambda b, c: (b * n + chunk_of(c, rev), 0))

    def grow_spec(rev):
        return pl.BlockSpec((N_GATE_COLS, CHUNK), lambda b, c: (0, b * n + chunk_of(c, rev)))

    full = lambda shape: pl.BlockSpec(shape, lambda b, c: (0,) * len(shape))
    st_shape = (1, N_HEADS, HEAD_DIM, HEAD_DIM)
    nm_shape = (1, 2 * N_HEADS, HEAD_DIM)
    scratch = [pltpu.VMEM((N_HEADS, HEAD_DIM, HEAD_DIM), F32), pltpu.VMEM((N_HEADS, HEAD_DIM), F32),
               pltpu.VMEM((N_HEADS, HEAD_DIM), F32)]
    consts = [full((ML_CONV_W, 2 * MIX_W)), full((1, LANES)), full((N_GATE_COLS, CHUNK))]

    cb, nm = pl.pallas_call(
        functools.partial(_ml_state_kernel, n=n),
        grid=(B, n),
        in_specs=[seg(SEG_MK, True), prev_spec(SEG_MK, True), next_spec(SEG_MK, True), seg(SEG_MV, True),
                  gcol_spec(True), grow_spec(True)] + consts,
        out_specs=[pl.BlockSpec(st_shape, lambda b, c: (b * n + n - 1 - c, 0, 0, 0)),
                   pl.BlockSpec(nm_shape, lambda b, c: (b * n + n - 1 - c, 0, 0))],
        out_shape=[jax.ShapeDtypeStruct((B * n, N_HEADS, HEAD_DIM, HEAD_DIM), BF16),
                   jax.ShapeDtypeStruct((B * n, 2 * N_HEADS, HEAD_DIM), F32)],
        scratch_shapes=scratch,
        compiler_params=_cparams(("arbitrary", "arbitrary")),
        name="mlstm_state",
    )(proj, proj, proj, proj, gates, gates_t, cw, bias_col, bias_row)

    return pl.pallas_call(
        functools.partial(_ml_out_kernel, n=n),
        grid=(B, n),
        in_specs=[seg(SEG_MQ, False), prev_spec(SEG_MQ, False), next_spec(SEG_MQ, False),
                  seg(SEG_MK, False), prev_spec(SEG_MK, False), next_spec(SEG_MK, False),
                  seg(SEG_MV, False), seg(SEG_MO, False), gcol_spec(False), grow_spec(False),
                  pl.BlockSpec(st_shape, lambda b, c: (b * n + c, 0, 0, 0)),
                  pl.BlockSpec(nm_shape, lambda b, c: (b * n + c, 0, 0))]
        + consts + [full((N_HEADS, HEAD_DIM))],
        out_specs=pl.BlockSpec((CHUNK, MIX_W), lambda b, c: (b * n + c, 0)),
        out_shape=jax.ShapeDtypeStruct((T, MIX_W), BF16),
        scratch_shapes=scratch,
        compiler_params=_cparams(("arbitrary", "arbitrary")),
        name="mlstm_out",
    )(proj, proj, proj, proj, proj, proj, proj, proj, gates, gates_t, cb, nm,
      cw, bias_col, bias_row, norm_w.astype(F32))


def _branch_kernel(oret_ref, ona_ref, oml_ref, ga_ref, gb_ref, gc_ref, wb_ref, bgb_ref, m_ref):
    o_refs = (oret_ref, ona_ref, oml_ref)
    g_refs = (ga_ref, gb_ref, gc_ref)
    for n in range(D_MODEL // SEG_W):
        cs = slice(n * SEG_W, (n + 1) * SEG_W)
        acc = None
        for i in range(N_BRANCH):
            z = g_refs[i][:, cs].astype(F32) + bgb_ref[i:i + 1, cs]
            term = _sigmoid(z) * _dot(o_refs[i][...], wb_ref[i, :, cs])
            acc = term if acc is None else acc + term
        m_ref[:, cs] = acc.astype(BF16)


def _outproj_router_kernel(x_ref, m_ref, wout_ref, n2w_ref, rwh_ref, rwl_ref, rb_ref,
                           h_ref, hn_ref, route_ref, cnt_ref, cnt_scr):
    h = x_ref[...] + _dot(m_ref[...], wout_ref[...])
    h_ref[...] = h
    hn = h * lax.rsqrt(jnp.mean(h * h, -1, keepdims=True) + EPS) * n2w_ref[...]
    hn_ref[...] = hn

    hn_hi = hn.astype(BF16)
    hn_lo = (hn - hn_hi.astype(F32)).astype(BF16)
    logits = (_dot(hn_hi, rwh_ref[...]) + _dot(hn_lo, rwh_ref[...]) + _dot(hn_hi, rwl_ref[...])) + rb_ref[...]
    lane_i = lax.broadcasted_iota(jnp.int32, logits.shape, 1)
    lane = lane_i.astype(F32)
    lane_grp = jnp.right_shift(lane_i - N_GROUPS, 3).astype(F32)
    gmask = lane_i < N_GROUPS
    lg = jnp.where(gmask, logits, NEG)
    mg = jnp.max(lg, -1, keepdims=True)
    p_top = 1.0 / jnp.sum(jnp.where(gmask, jnp.exp(lg - mg), 0.0), -1, keepdims=True)
    grp = jnp.min(jnp.where(lg == mg, lane, float(LANES)), -1, keepdims=True)
    emask = (lane_i >= N_GROUPS) & (lane_i < N_GROUPS + N_EXPERTS) & (lane_grp == grp)
    le = jnp.where(emask, logits, NEG)
    m1 = jnp.max(le, -1, keepdims=True)
    i1 = jnp.min(jnp.where(le == m1, lane, float(LANES)), -1, keepdims=True)
    le2 = jnp.where(lane == i1, NEG, le)
    m2 = jnp.max(le2, -1, keepdims=True)
    i2 = jnp.min(jnp.where(le2 == m2, lane, float(LANES)), -1, keepdims=True)
    e2 = jnp.exp(m2 - m1)
    gate1 = p_top / (1.0 + e2)
    gate2 = p_top * e2 / (1.0 + e2)

    @pl.when(pl.program_id(0) == 0)
    def _():
        cnt_scr[...] = jnp.zeros_like(cnt_scr)

    tm = logits.shape[0]
    hit1 = lane == i1
    hit2 = lane == i2
    onehot = jnp.where(hit1 | hit2, 1.0, 0.0)
    r_i = lax.broadcasted_iota(jnp.int32, (tm, tm), 0)
    c_i = lax.broadcasted_iota(jnp.int32, (tm, tm), 1)
    before = jnp.where(c_i < r_i, 1.0, 0.0).astype(BF16)
    prior = _dot(before, onehot.astype(BF16)) + cnt_scr[...]
    rank1 = jnp.sum(jnp.where(hit1, prior, 0.0), -1, keepdims=True)
    rank2 = jnp.sum(jnp.where(hit2, prior, 0.0), -1, keepdims=True)
    cnt = cnt_scr[...] + jnp.sum(onehot, 0, keepdims=True)
    cnt_scr[...] = cnt
    cnt_ref[...] = jnp.broadcast_to(cnt, cnt_ref.shape)

    cols = (i1 - N_GROUPS, i2 - N_GROUPS, gate1, gate2, rank1, rank2)
    route = jnp.zeros_like(logits)
    for ci, val in enumerate(cols):
        route = jnp.where(lane_i == ci, val, route)
    route_ref[...] = route


MERGE_ROWS = 512


def _merge(x2d, o_ret, o_na, o_ml, proj, wb, bgb, wout, n2w, rw_hi, rw_lo, rb):
    T = x2d.shape[0]
    tm = MERGE_ROWS
    full = lambda shape: pl.BlockSpec(shape, lambda i: (0,) * len(shape), pipeline_mode=pl.Buffered(1))
    row = lambda w: pl.BlockSpec((tm, w), lambda i: (i, 0))
    gate = lambda s: pl.BlockSpec((tm, D_MODEL), lambda i: (i, s // 2))
    merged = pl.pallas_call(
        _branch_kernel,
        grid=(T // tm,),
        in_specs=[row(MIX_W), row(MIX_W), row(MIX_W), gate(SEG_GA), gate(SEG_GB), gate(SEG_GC),
                  full((N_BRANCH, MIX_W, D_MODEL)), full((N_BRANCH, D_MODEL))],
        out_specs=row(D_MODEL),
        out_shape=jax.ShapeDtypeStruct((T, D_MODEL), BF16),
        compiler_params=_cparams(("arbitrary",)),
        name="branch_merge",
    )(o_ret, o_na, o_ml, proj, proj, proj, wb, bgb)
    return pl.pallas_call(
        _outproj_router_kernel,
        grid=(T // tm,),
        in_specs=[row(D_MODEL), row(D_MODEL), full((D_MODEL, D_MODEL)), full((1, D_MODEL)),
                  full((D_MODEL, LANES)), full((D_MODEL, LANES)), full((1, LANES))],
        out_specs=[row(D_MODEL), row(D_MODEL), row(LANES), pl.BlockSpec((8, LANES), lambda i: (0, 0))],
        out_shape=[jax.ShapeDtypeStruct((T, D_MODEL), F32), jax.ShapeDtypeStruct((T, D_MODEL), F32),
                   jax.ShapeDtypeStruct((T, LANES), F32), jax.ShapeDtypeStruct((8, LANES), F32)],
        scratch_shapes=[pltpu.VMEM((1, LANES), F32)],
        compiler_params=_cparams(("arbitrary",)),
        name="outproj_router",
    )(x2d, merged, wout, n2w, rw_hi, rw_lo, rb)


def _row_copy(src_hbm, row, dst, r, sem):
    return pltpu.make_async_copy(src_hbm.at[pl.ds(row, 1), :], dst.at[pl.ds(r, 1), :], sem)


def _ffn_kernel(blk_e_ref, n_used_ref, src_ref, hn_hbm, wg_ref, wu_ref, wd_ref, y_ref, xbuf, sem):
    i = pl.program_id(0)
    n_used = n_used_ref[0]
    slot = i % 2

    def start_gather(blk, to_slot):
        def body(r, carry):
            _row_copy(hn_hbm, src_ref[blk * MOE_ROWS + r], xbuf.at[to_slot], r, sem.at[to_slot]).start()
            return carry
        lax.fori_loop(0, MOE_ROWS, body, 0, unroll=8)

    @pl.when(i == 0)
    def _():
        start_gather(0, 0)

    @pl.when(i < n_used)
    def _():
        pltpu.make_async_copy(hn_hbm.at[pl.ds(0, MOE_ROWS), :], xbuf.at[slot], sem.at[slot]).wait()

        @pl.when(i + 1 < n_used)
        def _():
            start_gather(i + 1, 1 - slot)

        x = xbuf[slot].astype(BF16)
        a = _dot(x, wg_ref[0])
        hid = (a * _sigmoid(a)) * _dot(x, wu_ref[0])
        y_ref[...] = _dot(hid.astype(BF16), wd_ref[0])

    @pl.when(i >= n_used)
    def _():
        y_ref[...] = jnp.zeros_like(y_ref)


def _expert_ffn(hn, src, blk_e, n_used, wg, wu, wd):
    n_blocks = src.shape[0] // MOE_ROWS
    grid_spec = pltpu.PrefetchScalarGridSpec(
        num_scalar_prefetch=3,
        grid=(n_blocks,),
        in_specs=[
            pl.BlockSpec(memory_space=pl.ANY),
            pl.BlockSpec((1, D_MODEL, D_EXPERT), lambda i, be, nu, sr: (be[i], 0, 0)),
            pl.BlockSpec((1, D_MODEL, D_EXPERT), lambda i, be, nu, sr: (be[i], 0, 0)),
            pl.BlockSpec((1, D_EXPERT, D_MODEL), lambda i, be, nu, sr: (be[i], 0, 0)),
        ],
        out_specs=pl.BlockSpec((MOE_ROWS, D_MODEL), lambda i, be, nu, sr: (i, 0)),
        scratch_shapes=[pltpu.VMEM((2, MOE_ROWS, D_MODEL), F32), pltpu.SemaphoreType.DMA((2,))],
    )
    return pl.pallas_call(
        _ffn_kernel,
        grid_spec=grid_spec,
        out_shape=jax.ShapeDtypeStruct((n_blocks * MOE_ROWS, D_MODEL), F32),
        compiler_params=_cparams(("arbitrary",), disable_bounds_checks=True),
        name="expert_ffn",
    )(blk_e, n_used, src, hn, wg, wu, wd)


COMBINE_ROWS = 256


def _combine_kernel(dest_ref, h_ref, route_ref, yb_hbm, out_ref, ybuf, sem):
    i = pl.program_id(0)
    slot = i % 2

    def start_gather(tile, to_slot):
        def body(r, carry):
            a = (tile * COMBINE_ROWS + r) * TOP_K_INNER
            for k in range(TOP_K_INNER):
                _row_copy(yb_hbm, dest_ref[a + k], ybuf.at[to_slot, k], r, sem.at[to_slot]).start()
            return carry
        lax.fori_loop(0, COMBINE_ROWS, body, 0, unroll=8)

    @pl.when(i == 0)
    def _():
        start_gather(0, 0)

    for k in range(TOP_K_INNER):
        pltpu.make_async_copy(yb_hbm.at[pl.ds(0, COMBINE_ROWS), :], ybuf.at[slot, k], sem.at[slot]).wait()

    @pl.when(i + 1 < pl.num_programs(0))
    def _():
        start_gather(i + 1, 1 - slot)

    route = route_ref[...]
    out_ref[...] = h_ref[...] + route[:, 2:3] * ybuf[slot, 0] + route[:, 3:4] * ybuf[slot, 1]


def _combine(h, route, yb, dest):
    T = h.shape[0]
    grid_spec = pltpu.PrefetchScalarGridSpec(
        num_scalar_prefetch=1,
        grid=(T // COMBINE_ROWS,),
        in_specs=[
            pl.BlockSpec((COMBINE_ROWS, D_MODEL), lambda i, d: (i, 0)),
            pl.BlockSpec((COMBINE_ROWS, LANES), lambda i, d: (i, 0)),
            pl.BlockSpec(memory_space=pl.ANY),
        ],
        out_specs=pl.BlockSpec((COMBINE_ROWS, D_MODEL), lambda i, d: (i, 0)),
        scratch_shapes=[pltpu.VMEM((2, TOP_K_INNER, COMBINE_ROWS, D_MODEL), F32), pltpu.SemaphoreType.DMA((2,))],
    )
    return pl.pallas_call(
        _combine_kernel,
        grid_spec=grid_spec,
        out_shape=jax.ShapeDtypeStruct((T, D_MODEL), F32),
        compiler_params=_cparams(("arbitrary",), disable_bounds_checks=True),
        name="moe_combine",
    )(dest, h, route, yb)


def _moe(h, hn, route, counts, wg, wu, wd):
    T = h.shape[0]
    n_assign = T * TOP_K_INNER
    expert = route[:, 0:2].astype(jnp.int32)
    rank = route[:, 4:6].astype(jnp.int32)
    cnt = counts[0, N_GROUPS:N_GROUPS + N_EXPERTS].astype(jnp.int32)
    padded = (cnt + MOE_ROWS - 1) // MOE_ROWS * MOE_ROWS
    pend = jnp.cumsum(padded)
    pstart = pend - padded
    dest = (pstart[expert] + rank).reshape(-1)
    n_blocks = (n_assign + N_EXPERTS * (MOE_ROWS - 1) + MOE_ROWS - 1) // MOE_ROWS
    blk_start = jnp.arange(n_blocks, dtype=jnp.int32) * MOE_ROWS
    blk_e = jnp.minimum(jnp.sum((pend[None, :] <= blk_start[:, None]).astype(jnp.int32), 1), N_EXPERTS - 1)
    n_used = (pend[-1] // MOE_ROWS).astype(jnp.int32).reshape(1)
    flat_t = jnp.repeat(jnp.arange(T, dtype=jnp.int32), TOP_K_INNER)
    src = jnp.zeros((n_blocks * MOE_ROWS,), jnp.int32).at[dest].set(flat_t)
    yb = _expert_ffn(hn, src, blk_e, n_used, wg, wu, wd)
    return _combine(h, route, yb, dest)


def _rope_tables(L):
    freqs = ROPE_BASE ** (-jnp.arange(0, HEAD_DIM, 2, dtype=F32) / HEAD_DIM)
    ang = jnp.arange(L, dtype=F32)[:, None] * freqs[None]
    cos, sin = jnp.cos(ang), jnp.sin(ang)
    return jnp.concatenate([cos, cos], -1), jnp.concatenate([-sin, sin], -1)


def _prep_layer(p):
    w_in = p["w_in"]
    sp = np.cumsum([MIX_W] * 11 + [2 * N_HEADS, 2 * N_HEADS] + [D_MODEL] * N_BRANCH)[:-1].tolist()
    (rq, rk, rv, rg, nq, nk, nv, mq, mk, mv, mo, mi, mf, ga, gb, gc) = jnp.split(w_in, sp, axis=-1)
    w_main = jnp.concatenate([ga, gb, gc, rq, rk, rv, rg, nq, nk, nv, mq, mk, mv, mo], -1).astype(BF16)
    wif = jnp.concatenate([mi, mf], -1)
    wif_pad = jnp.zeros((D_MODEL, LANES), BF16).at[:, :N_GATE_COLS].set(wif.astype(BF16))
    rw = jnp.zeros((D_MODEL, LANES), F32).at[:, :N_GROUPS].set(p["router_g_w"]) \
        .at[:, N_GROUPS:N_GROUPS + N_EXPERTS].set(p["router_e_w"])
    rb = jnp.zeros((1, LANES), F32).at[0, :N_GROUPS].set(p["router_g_b"]) \
        .at[0, N_GROUPS:N_GROUPS + N_EXPERTS].set(p["router_e_b"])
    return dict(
        nw=p["norm1_w"].astype(F32).reshape(1, D_MODEL), w_main=w_main, wif=wif_pad, wift=wif.T.astype(BF16),
        qn=p["na_q_norm"].astype(F32).reshape(1, HEAD_DIM), kn=p["na_k_norm"].astype(F32).reshape(1, HEAD_DIM),
        wb=p["w_branch"].astype(BF16), bgb=p["branch_gate_b"].astype(F32), wout=p["w_out"].astype(BF16),
        n2w=p["norm2_w"].astype(F32).reshape(1, D_MODEL), rb=rb,
        rw_hi=rw.astype(BF16), rw_lo=(rw - rw.astype(BF16).astype(F32)).astype(BF16),
        wg=p["exp_w_gate"].astype(BF16), wu=p["exp_w_up"].astype(BF16), wd=p["exp_w_down"].astype(BF16),
    )


def _layer(x2d, B, L, p, q, rope):
    proj, gates, gates_t = _inproj(x2d, L, q["nw"], q["w_main"], q["wif"], q["wift"], rope[0], rope[1],
                                   q["qn"], q["kn"])
    o_ret = _retention(proj, B, L, p["ret_decay"], p["ret_norm_w"])
    o_na = _neighborhood(proj, B, L, p["na_rpb"])
    o_ml = _mlstm(proj, gates, gates_t, B, L, p["ml_conv"], p["ml_igate_b"], p["ml_fgate_b"], p["ml_norm_w"])
    h, hn, route, counts = _merge(x2d, o_ret, o_na, o_ml, proj, q["wb"], q["bgb"], q["wout"], q["n2w"],
                                  q["rw_hi"], q["rw_lo"], q["rb"])
    return _moe(h, hn, route, counts, q["wg"], q["wu"], q["wd"])


_PARAM_NAMES = ("norm1_w", "w_in", "ret_decay", "ret_norm_w", "na_q_norm", "na_k_norm", "na_rpb", "ml_conv",
                "ml_igate_b", "ml_fgate_b", "ml_norm_w", "w_branch", "branch_gate_b", "w_out", "norm2_w",
                "router_g_w", "router_g_b", "router_e_w", "router_e_b", "exp_w_gate", "exp_w_up", "exp_w_down")


def _run(x, layers, preps):
    B, L, D = x.shape
    rope = _rope_tables(L)
    x2d = x.reshape(B * L, D)
    for p, q in zip(layers, preps):
        x2d = _layer(x2d, B, L, p, q, rope)
    return x2d.reshape(B, L, D)


def kernel(x_prompt, x_sample, norm1_w, w_in, ret_decay, ret_norm_w, na_q_norm, na_k_norm, na_rpb, ml_conv,
           ml_igate_b, ml_fgate_b, ml_norm_w, w_branch, branch_gate_b, w_out, norm2_w, router_g_w, router_g_b,
           router_e_w, router_e_b, exp_w_gate, exp_w_up, exp_w_down):
    stacked = (norm1_w, w_in, ret_decay, ret_norm_w, na_q_norm, na_k_norm, na_rpb, ml_conv, ml_igate_b,
               ml_fgate_b, ml_norm_w, w_branch, branch_gate_b, w_out, norm2_w, router_g_w, router_g_b,
               router_e_w, router_e_b, exp_w_gate, exp_w_up, exp_w_down)
    depth = w_in.shape[0]
    layers = [dict(zip(_PARAM_NAMES, (a[l] for a in stacked))) for l in range(depth)]
    preps = [_prep_layer(p) for p in layers]
    return (_run(x_prompt, layers, preps), _run(x_sample, layers, preps))
```

```python
import functools

import numpy as np
import jax
import jax.numpy as jnp
from jax import lax
from jax.experimental import pallas as pl
from jax.experimental.pallas import tpu as pltpu

D_MODEL = 2048
HEAD_DIM = 128
MIX_W = D_MODEL // 2
N_HEADS = MIX_W // HEAD_DIM
N_BRANCH = 3
CHUNK = 128
ROPE_BASE = 10000.0
GRID_W = 64
WIN_ROWS = 8
WIN_COLS = 16
ML_CONV_W = 3
N_GROUPS = 4
EXPERTS_PER_GROUP = 8
N_EXPERTS = N_GROUPS * EXPERTS_PER_GROUP
TOP_K_INNER = 2
D_EXPERT = D_MODEL // 2
EPS = 1e-6
NEG = -1e30

F32 = jnp.float32
BF16 = jnp.bfloat16

LANES = 128
SEG_W = MIX_W
SEG_GA, SEG_GB, SEG_GC = 0, 2, 4
SEG_RQ, SEG_RK, SEG_RV, SEG_RG = 6, 7, 8, 9
SEG_NQ, SEG_NK, SEG_NV = 10, 11, 12
SEG_MQ, SEG_MK, SEG_MV, SEG_MO = 13, 14, 15, 16
N_SEG = 17
N_GATE_COLS = 4 * N_HEADS

NA_QROWS = 2
NA_KBLKS = 5
NA_KW = NA_KBLKS * CHUNK
MOE_ROWS = 256

VMEM_LIMIT = 56 * 1024 * 1024


def _cparams(sem, **kw):
    return pltpu.CompilerParams(dimension_semantics=sem, vmem_limit_bytes=VMEM_LIMIT, **kw)


def _sigmoid(z):
    return 0.5 * jnp.tanh(0.5 * z) + 0.5


def _hs(h):
    return slice(h * HEAD_DIM, (h + 1) * HEAD_DIM)


def _dot(a, b):
    return jnp.dot(a, b, preferred_element_type=F32)


def _dot_nt(a, b):
    return lax.dot_general(a, b, (((1,), (1,)), ((), ())), preferred_element_type=F32)


def _dot_tn(a, b):
    return lax.dot_general(a, b, (((0,), (0,)), ((), ())), preferred_element_type=F32)


def _inproj_kernel(x_ref, nw_ref, w_ref, wif_ref, wift_ref, cos_ref, sin_ref, qn_ref, kn_ref,
                   proj_ref, g_ref, gt_ref, xn_ref):
    j = pl.program_id(1)

    @pl.when(j == 0)
    def _():
        x = x_ref[...]
        y = x * lax.rsqrt(jnp.mean(x * x, -1, keepdims=True) + EPS) * nw_ref[...]
        xn = y.astype(BF16)
        xn_ref[...] = xn
        g_ref[...] = _dot(xn, wif_ref[...])
        gt_ref[...] = _dot_nt(wift_ref[...], xn)

    acc = _dot(xn_ref[...], w_ref[...])

    is_rope = (j == SEG_RQ) | (j == SEG_RK)
    is_norm = (j == SEG_NQ) | (j == SEG_NK)

    @pl.when(is_rope)
    def _():
        scale = jnp.where(j == SEG_RK, HEAD_DIM ** -0.5, 1.0).astype(F32)
        cos = cos_ref[...]
        sin = sin_ref[...]
        for h in range(N_HEADS):
            xh = acc[:, _hs(h)]
            r = xh * cos + pltpu.roll(xh, HEAD_DIM // 2, 1) * sin
            proj_ref[:, _hs(h)] = (r * scale).astype(BF16)

    @pl.when(is_norm)
    def _():
        w = jnp.where(j == SEG_NQ, qn_ref[...] * (HEAD_DIM ** -0.5), kn_ref[...])
        for h in range(N_HEADS):
            xh = acc[:, _hs(h)]
            y = xh * lax.rsqrt(jnp.mean(xh * xh, -1, keepdims=True) + EPS) * w
            proj_ref[:, _hs(h)] = y.astype(BF16)

    @pl.when(jnp.logical_not(is_rope | is_norm))
    def _():
        proj_ref[...] = acc.astype(BF16)


def _inproj(x2d, L, nw, w_main, wif, wift, cos_t, sin_t, qn, kn):
    T = x2d.shape[0]
    tm = min(1024, L)
    nlt = L // tm
    return pl.pallas_call(
        _inproj_kernel,
        grid=(T // tm, N_SEG),
        in_specs=[
            pl.BlockSpec((tm, D_MODEL), lambda i, j: (i, 0)),
            pl.BlockSpec((1, D_MODEL), lambda i, j: (0, 0)),
            pl.BlockSpec((D_MODEL, SEG_W), lambda i, j: (0, j)),
            pl.BlockSpec((D_MODEL, LANES), lambda i, j: (0, 0)),
            pl.BlockSpec((N_GATE_COLS, D_MODEL), lambda i, j: (0, 0)),
            pl.BlockSpec((tm, HEAD_DIM), lambda i, j: (i % nlt, 0)),
            pl.BlockSpec((tm, HEAD_DIM), lambda i, j: (i % nlt, 0)),
            pl.BlockSpec((1, HEAD_DIM), lambda i, j: (0, 0)),
            pl.BlockSpec((1, HEAD_DIM), lambda i, j: (0, 0)),
        ],
        out_specs=[
            pl.BlockSpec((tm, SEG_W), lambda i, j: (i, j)),
            pl.BlockSpec((tm, LANES), lambda i, j: (i, 0)),
            pl.BlockSpec((N_GATE_COLS, tm), lambda i, j: (0, i)),
        ],
        out_shape=[
            jax.ShapeDtypeStruct((T, N_SEG * SEG_W), BF16),
            jax.ShapeDtypeStruct((T, LANES), F32),
            jax.ShapeDtypeStruct((N_GATE_COLS, T), F32),
        ],
        scratch_shapes=[pltpu.VMEM((tm, D_MODEL), BF16)],
        compiler_params=_cparams(("arbitrary", "arbitrary")),
        name="inproj",
    )(x2d, nw, w_main, wif, wift, cos_t, sin_t, qn, kn)


def _ret_state_kernel(k_ref, v_ref, kb_ref, cdb_ref, sb_ref, s_scr):
    c = pl.program_id(1)

    @pl.when(c == 0)
    def _():
        s_scr[...] = jnp.zeros_like(s_scr)

    sb_ref[0] = s_scr[...].astype(BF16)
    kk = (k_ref[...].astype(F32) * kb_ref[...]).astype(BF16)
    v = v_ref[...]
    for h in range(N_HEADS):
        kv = _dot_tn(kk[:, _hs(h)], v[:, _hs(h)])
        s_scr[h] = s_scr[h] * cdb_ref[h:h + 1, :] + kv


def _ret_out_kernel(q_ref, k_ref, v_ref, g_ref, sb_ref, dmat_ref, qf_ref, qb_ref, kf_ref, cdf_ref, nw_ref,
                    o_ref, s_scr):
    c = pl.program_id(1)

    @pl.when(c == 0)
    def _():
        s_scr[...] = jnp.zeros_like(s_scr)

    q = q_ref[...]
    k = k_ref[...]
    v = v_ref[...]
    qf32 = q.astype(F32)
    q_fwd = (qf32 * qf_ref[...]).astype(BF16)
    q_bwd = (qf32 * qb_ref[...]).astype(BF16)
    k_end = (k.astype(F32) * kf_ref[...]).astype(BF16)
    g = g_ref[...].astype(F32)
    for h in range(N_HEADS):
        hs = _hs(h)
        s = _dot_nt(q[:, hs], k[:, hs]) * dmat_ref[h]
        o = _dot(s.astype(BF16), v[:, hs])
        o = o + _dot(q_fwd[:, hs], s_scr[h].astype(BF16))
        o = o + _dot(q_bwd[:, hs], sb_ref[0, h])
        s_scr[h] = s_scr[h] * cdf_ref[h:h + 1, :] + _dot_tn(k_end[:, hs], v[:, hs])
        y = o * lax.rsqrt(jnp.mean(o * o, -1, keepdims=True) + EPS) * nw_ref[h:h + 1, :]
        gh = g[:, hs]
        o_ref[:, hs] = (y * (gh * _sigmoid(gh))).astype(BF16)


def _retention(proj, B, L, ret_decay, ret_norm_w):
    n = L // CHUNK
    T = B * L
    lg = jax.nn.log_sigmoid(ret_decay.astype(F32))
    idx = jnp.arange(CHUNK, dtype=F32)
    diff = idx[:, None] - idx[None, :]
    dmat = jnp.where(diff >= 0, jnp.exp(jnp.maximum(diff, 0.0) * lg[0][:, None, None]),
                     jnp.exp(jnp.maximum(-diff, 0.0) * lg[1][:, None, None]))

    def lane_tab(e):
        return jnp.repeat(jnp.exp(e).T, HEAD_DIM, axis=1)

    qf_tab = lane_tab((idx + 1.0)[None, :] * lg[0][:, None])
    qb_tab = lane_tab((CHUNK - idx)[None, :] * lg[1][:, None])
    kf_tab = lane_tab((CHUNK - 1.0 - idx)[None, :] * lg[0][:, None])
    kb_tab = lane_tab(idx[None, :] * lg[1][:, None])
    cdf = jnp.broadcast_to(jnp.exp(CHUNK * lg[0])[:, None], (N_HEADS, HEAD_DIM))
    cdb = jnp.broadcast_to(jnp.exp(CHUNK * lg[1])[:, None], (N_HEADS, HEAD_DIM))

    def seg(s, rev=False):
        if rev:
            return pl.BlockSpec((CHUNK, SEG_W), lambda b, c: (b * n + n - 1 - c, s))
        return pl.BlockSpec((CHUNK, SEG_W), lambda b, c: (b * n + c, s))

    full = lambda shape: pl.BlockSpec(shape, lambda b, c: (0,) * len(shape))
    st_shape = (1, N_HEADS, HEAD_DIM, HEAD_DIM)

    sb = pl.pallas_call(
        _ret_state_kernel,
        grid=(B, n),
        in_specs=[seg(SEG_RK, True), seg(SEG_RV, True), full((CHUNK, MIX_W)), full((N_HEADS, HEAD_DIM))],
        out_specs=pl.BlockSpec(st_shape, lambda b, c: (b * n + n - 1 - c, 0, 0, 0)),
        out_shape=jax.ShapeDtypeStruct((B * n, N_HEADS, HEAD_DIM, HEAD_DIM), BF16),
        scratch_shapes=[pltpu.VMEM((N_HEADS, HEAD_DIM, HEAD_DIM), F32)],
        compiler_params=_cparams(("arbitrary", "arbitrary")),
        name="ret_state",
    )(proj, proj, kb_tab, cdb)

    return pl.pallas_call(
        _ret_out_kernel,
        grid=(B, n),
        in_specs=[seg(SEG_RQ), seg(SEG_RK), seg(SEG_RV), seg(SEG_RG),
                  pl.BlockSpec(st_shape, lambda b, c: (b * n + c, 0, 0, 0)),
                  full((N_HEADS, CHUNK, CHUNK)), full((CHUNK, MIX_W)), full((CHUNK, MIX_W)),
                  full((CHUNK, MIX_W)), full((N_HEADS, HEAD_DIM)), full((N_HEADS, HEAD_DIM))],
        out_specs=pl.BlockSpec((CHUNK, MIX_W), lambda b, c: (b * n + c, 0)),
        out_shape=jax.ShapeDtypeStruct((T, MIX_W), BF16),
        scratch_shapes=[pltpu.VMEM((N_HEADS, HEAD_DIM, HEAD_DIM), F32)],
        compiler_params=_cparams(("arbitrary", "arbitrary")),
        name="ret_out",
    )(proj, proj, proj, proj, sb, dmat, qf_tab, qb_tab, kf_tab, cdf, ret_norm_w.astype(F32))


def _na_index_tables(nqb):
    rows = NA_QROWS * nqb
    qi = np.arange(CHUNK)
    ki = np.arange(NA_KW)

    def one(qb):
        kb = NA_QROWS * int(np.clip(qb - 2, 0, nqb - NA_KBLKS))
        r = (NA_QROWS * qb + qi // GRID_W)[:, None]
        qc = (qi % GRID_W)[:, None]
        kr = (kb + ki // GRID_W)[None, :]
        kc = (ki % GRID_W)[None, :]
        rs = np.clip(r - WIN_ROWS // 2, 0, rows - WIN_ROWS)
        cs = np.clip(qc - WIN_COLS // 2, 0, GRID_W - WIN_COLS)
        valid = (kr >= rs) & (kr < rs + WIN_ROWS) & (kc >= cs) & (kc < cs + WIN_COLS)
        dr = np.clip(kr - r + WIN_ROWS - 1, 0, 2 * WIN_ROWS - 2)
        dc = np.clip(kc - qc + WIN_COLS - 1, 0, 2 * WIN_COLS - 2)
        return dr + 0 * dc, dc + 0 * dr, valid

    reps = [0, 1, 2, nqb - 2, nqb - 1]
    tabs = [one(qb) for qb in reps]
    for qb in range(2, nqb - 2):
        t = one(qb)
        assert all(np.array_equal(a, b) for a, b in zip(t, tabs[2]))
    return tuple(np.stack([t[i] for t in tabs]) for i in range(3))


def _na_kernel(q_ref, k0, k1, k2, k3, k4, v0, v1, v2, v3, v4, bias_ref, o_ref):
    k_refs = (k0, k1, k2, k3, k4)
    v_refs = (v0, v1, v2, v3, v4)
    for h in range(N_HEADS):
        hs = _hs(h)
        q = q_ref[:, hs]
        k_all = jnp.concatenate([r[:, hs] for r in k_refs], 0)
        v_all = jnp.concatenate([r[:, hs] for r in v_refs], 0)
        s = _dot_nt(q, k_all) + bias_ref[0, h]
        p = jnp.exp(s - jnp.max(s, -1, keepdims=True))
        o = _dot(p.astype(BF16), v_all)
        o_ref[:, hs] = (o / jnp.sum(p, -1, keepdims=True)).astype(BF16)


def _neighborhood(proj, B, L, rpb):
    nqb = L // CHUNK
    assert nqb >= NA_KBLKS and L % (GRID_W * NA_QROWS) == 0
    T = B * L
    dr, dc, valid = _na_index_tables(nqb)
    n_cls = dr.shape[0]
    kr_n = NA_KW // GRID_W
    dr6 = dr.reshape(n_cls, NA_QROWS, GRID_W, kr_n, GRID_W)
    dc6 = dc.reshape(n_cls, NA_QROWS, GRID_W, kr_n, GRID_W)
    dr_s = dr6[:, :, 0, :, 0]
    dc_s = dc6[0, 0, :, 0, :]
    assert np.array_equal(dr6, np.broadcast_to(dr_s[:, :, None, :, None], dr6.shape))
    assert np.array_equal(dc6, np.broadcast_to(dc_s[None, None, :, None, :], dc6.shape))
    sel_r = (dr_s[..., None] == np.arange(2 * WIN_ROWS - 1)).astype(np.float32)
    sel_c = (np.arange(2 * WIN_COLS - 1)[:, None, None] == dc_s[None]).astype(np.float32)
    hp = lax.Precision.HIGHEST
    t = jnp.einsum('cajr,hrd->cajhd', sel_r, rpb.astype(F32), precision=hp)
    bias = jnp.einsum('cajhd,dqk->chaqjk', t, sel_c, precision=hp).reshape(n_cls, N_HEADS, CHUNK, NA_KW)
    bias = jnp.where(jnp.asarray(valid)[:, None], bias, NEG)

    def kspec(s, j):
        return pl.BlockSpec((CHUNK, SEG_W),
                            lambda b, qb: (b * nqb + jnp.clip(qb - 2, 0, nqb - NA_KBLKS) + j, s))

    def cls(qb):
        return jnp.where(qb < 2, qb, jnp.where(qb >= nqb - 2, qb - (nqb - 5), 2))

    return pl.pallas_call(
        _na_kernel,
        grid=(B, nqb),
        in_specs=[pl.BlockSpec((CHUNK, SEG_W), lambda b, qb: (b * nqb + qb, SEG_NQ))]
        + [kspec(SEG_NK, j) for j in range(NA_KBLKS)]
        + [kspec(SEG_NV, j) for j in range(NA_KBLKS)]
        + [pl.BlockSpec((1, N_HEADS, CHUNK, NA_KW), lambda b, qb: (cls(qb), 0, 0, 0))],
        out_specs=pl.BlockSpec((CHUNK, MIX_W), lambda b, qb: (b * nqb + qb, 0)),
        out_shape=jax.ShapeDtypeStruct((T, MIX_W), BF16),
        compiler_params=_cparams(("arbitrary", "arbitrary")),
        name="neighborhood",
    )(*([proj] * (1 + 2 * NA_KBLKS)), bias)


HALO = 16


def _log_sigmoid(x):
    return jnp.minimum(x, 0.0) - jnp.log1p(jnp.exp(-jnp.abs(x)))


def _conv_silu(x_ref, prev_ref, next_ref, w_ref, col0, c, n):
    x = x_ref[...].astype(F32)
    row = lax.broadcasted_iota(jnp.int32, x.shape, 0)
    prev_row = prev_ref[HALO - 1:HALO, :].astype(F32) * (c > 0).astype(F32)
    next_row = next_ref[0:1, :].astype(F32) * (c < n - 1).astype(F32)
    x_prev = jnp.where(row == 0, prev_row, pltpu.roll(x, 1, 0))
    x_next = jnp.where(row == CHUNK - 1, next_row, pltpu.roll(x, CHUNK - 1, 0))
    cs = slice(col0, col0 + MIX_W)
    y = x_prev * w_ref[0:1, cs] + x * w_ref[1:2, cs] + x_next * w_ref[2:3, cs]
    return y * _sigmoid(y)


def _tri():
    r = lax.broadcasted_iota(jnp.int32, (CHUNK, CHUNK), 0)
    c = lax.broadcasted_iota(jnp.int32, (CHUNK, CHUNK), 1)
    return (c <= r).astype(F32), (c >= r).astype(F32)


def _hp_dot(a, b):
    return jnp.dot(a, b, preferred_element_type=F32, precision=lax.Precision.HIGHEST)


FWD_LANE = 2 * N_HEADS
BWD_LANE = 3 * N_HEADS


def _gate_dense(gc_ref, gr_ref, bc_ref, br_ref):
    low, up = _tri()
    g_col = gc_ref[...] + bc_ref[...]
    lf_col = _log_sigmoid(g_col)
    lf_row = _log_sigmoid(gr_ref[...] + br_ref[...])
    b_col = _hp_dot(low, lf_col)
    b_row = _hp_dot(lf_row, up)
    tot_row = b_col[CHUNK - 1:CHUNK, :]
    tot_col = b_row[:, CHUNK - 1:CHUNK]
    lane = lax.broadcasted_iota(jnp.int32, (CHUNK, LANES), 1)
    row = lax.broadcasted_iota(jnp.int32, (N_GATE_COLS, CHUNK), 0)
    bb_col = jnp.where(lane >= BWD_LANE, tot_row - b_col + lf_col, b_col)
    bb_row = jnp.where(row >= BWD_LANE, tot_col - b_row + lf_row, b_row)
    ck = pltpu.roll(g_col, FWD_LANE, 1) - bb_col
    a = tot_row + ck
    m_loc = jnp.max(a, 0, keepdims=True)
    ea = jnp.exp(a - m_loc)
    return bb_row, ck, tot_row, m_loc, ea


def _lane_bcast(x, l):
    return jnp.broadcast_to(x[:, l:l + 1], (x.shape[0], HEAD_DIM))


def _ml_state_update(kc_h, ct_loc_fn, ea, tot_row, m_loc, l, c_scr, n_scr, m_scr, h):
    m_p = m_scr[h:h + 1, 0:1]
    g_tot = tot_row[:, l:l + 1]
    m_l = m_loc[:, l:l + 1]
    kw = kc_h * _lane_bcast(ea, l)
    ct_loc = ct_loc_fn(kw.astype(BF16))
    n_loc = jnp.sum(kw, 0, keepdims=True)
    m_new = jnp.maximum(g_tot + m_p, m_l)
    sp = jnp.exp(g_tot + m_p - m_new)
    sl = jnp.exp(m_l - m_new)
    c_scr[h] = sp * c_scr[h] + sl * ct_loc
    n_scr[h:h + 1, :] = sp * n_scr[h:h + 1, :] + sl * n_loc
    m_scr[h:h + 1, :] = jnp.broadcast_to(m_new, (1, HEAD_DIM))


def _ml_init(c, c_scr, n_scr, m_scr):
    @pl.when(c == 0)
    def _():
        c_scr[...] = jnp.zeros_like(c_scr)
        n_scr[...] = jnp.zeros_like(n_scr)
        m_scr[...] = jnp.full_like(m_scr, NEG)


def _ml_state_kernel(k_ref, kp_ref, kn_ref, v_ref, gc_ref, gr_ref, cw_ref, bc_ref, br_ref,
                     cb_ref, nm_ref, c_scr, n_scr, m_scr, *, n):
    step = pl.program_id(1)
    _ml_init(step, c_scr, n_scr, m_scr)
    c = n - 1 - step
    cb_ref[0] = c_scr[...].astype(BF16)
    nm_ref[0, 0:N_HEADS, :] = n_scr[...]
    nm_ref[0, N_HEADS:2 * N_HEADS, :] = m_scr[...]

    kc = _conv_silu(k_ref, kp_ref, kn_ref, cw_ref, MIX_W, c, n) * (HEAD_DIM ** -0.5)
    v = v_ref[...]
    _, _, tot_row, m_loc, ea = _gate_dense(gc_ref, gr_ref, bc_ref, br_ref)
    for h in range(N_HEADS):
        v_h = v[:, _hs(h)]
        _ml_state_update(kc[:, _hs(h)], lambda kw: _dot_tn(v_h, kw), ea, tot_row, m_loc, BWD_LANE + h,
                         c_scr, n_scr, m_scr, h)


def _ml_direction(st, mask, b_q, ck_b, m_p, qn, vt_bf, ct_bf, q_bf):
    dmat = jnp.where(mask, b_q + ck_b, NEG)
    inter = b_q + m_p
    m_row = jnp.maximum(jnp.max(dmat, 0, keepdims=True), inter)
    s = st * jnp.exp(dmat - m_row)
    e_int = jnp.exp(inter - m_row)
    num = _dot(vt_bf, s.astype(BF16)) + e_int * _dot_nt(ct_bf, q_bf)
    den = jnp.sum(s, 0, keepdims=True) + e_int * qn
    return num / jnp.maximum(jnp.abs(den), jnp.exp(-m_row))


def _ml_out_kernel(q_ref, qp_ref, qn_ref, k_ref, kp_ref, kn_ref, v_ref, o_ref_in, gc_ref, gr_ref,
                   cb_ref, nm_ref, cw_ref, bc_ref, br_ref, nwt_ref,
                   out_ref, c_scr, n_scr, m_scr, *, n):
    c = pl.program_id(1)
    _ml_init(c, c_scr, n_scr, m_scr)
    qc = _conv_silu(q_ref, qp_ref, qn_ref, cw_ref, 0, c, n)
    kc = _conv_silu(k_ref, kp_ref, kn_ref, cw_ref, MIX_W, c, n) * (HEAD_DIM ** -0.5)
    q_bf = qc.astype(BF16)
    k_bf = kc.astype(BF16)
    v = v_ref[...]
    og = o_ref_in[...].astype(F32)
    bb_row, ck, tot_row, m_loc, ea = _gate_dense(gc_ref, gr_ref, bc_ref, br_ref)
    key = lax.broadcasted_iota(jnp.int32, (CHUNK, CHUNK), 0)
    qry = lax.broadcasted_iota(jnp.int32, (CHUNK, CHUNK), 1)
    causal = key <= qry
    anti = key >= qry
    pad = jnp.zeros((8 - 2, HEAD_DIM), F32)
    for h in range(N_HEADS):
        hs = _hs(h)
        lf, lb = FWD_LANE + h, BWD_LANE + h
        qh = q_bf[:, hs]
        vt_bf = v[:, hs].astype(F32).T.astype(BF16)
        st = _dot_nt(k_bf[:, hs], qh)
        n_prev = jnp.concatenate([n_scr[h:h + 1, :], nm_ref[0, h:h + 1, :], pad], 0).astype(BF16)
        qn = _dot_nt(n_prev, qh)
        h_f = _ml_direction(st, causal, bb_row[lf:lf + 1, :], _lane_bcast(ck, lf), m_scr[h:h + 1, 0:1],
                            qn[0:1, :], vt_bf, c_scr[h].astype(BF16), qh)
        h_b = _ml_direction(st, anti, bb_row[lb:lb + 1, :], _lane_bcast(ck, lb),
                            nm_ref[0, N_HEADS + h:N_HEADS + h + 1, 0:1], qn[1:2, :], vt_bf, cb_ref[0, h], qh)
        _ml_state_update(kc[:, hs], lambda kw: _dot(vt_bf, kw), ea, tot_row, m_loc, lf, c_scr, n_scr, m_scr, h)
        ot = h_f + h_b
        yt = ot * lax.rsqrt(jnp.mean(ot * ot, 0, keepdims=True) + EPS) * nwt_ref[h]
        out_ref[:, hs] = (yt.T * _sigmoid(og[:, hs])).astype(BF16)


def _mlstm(proj, gates, gates_t, B, L, conv_w, ig_b, fg_b, norm_w):
    n = L // CHUNK
    T = B * L
    hb = CHUNK // HALO
    bias = jnp.concatenate([ig_b.astype(F32).reshape(-1), fg_b.astype(F32).reshape(-1)])
    bias_col = jnp.zeros((1, LANES), F32).at[0, :N_GATE_COLS].set(bias)
    bias_row = jnp.broadcast_to(bias[:, None], (N_GATE_COLS, CHUNK))
    cw = conv_w.astype(F32)
    norm_wt = jnp.broadcast_to(norm_w.astype(F32)[:, :, None], (N_HEADS, HEAD_DIM, CHUNK))

    def seg(s, rev):
        if rev:
            return pl.BlockSpec((CHUNK, SEG_W), lambda b, c: (b * n + n - 1 - c, s))
        return pl.BlockSpec((CHUNK, SEG_W), lambda b, c: (b * n + c, s))

    def chunk_of(c, rev):
        return n - 1 - c if rev else c

    def prev_spec(s, rev):
        return pl.BlockSpec((HALO, SEG_W),
                            lambda b, c: (b * n * hb + jnp.maximum(chunk_of(c, rev) * hb - 1, 0), s))

    def next_spec(s, rev):
        return pl.BlockSpec((HALO, SEG_W),
                            lambda b, c: (b * n * hb + jnp.minimum((chunk_of(c, rev) + 1) * hb, n * hb - 1), s))

    def gcol_spec(rev):
        return pl.BlockSpec((CHUNK, LANES), lambda b, c: (b * n + chunk_of(c, rev), 0))

    def grow_spec(rev):
        return pl.BlockSpec((N_GATE_COLS, CHUNK), lambda b, c: (0, b * n + chunk_of(c, rev)))

    full = lambda shape: pl.BlockSpec(shape, lambda b, c: (0,) * len(shape))
    st_shape = (1, N_HEADS, HEAD_DIM, HEAD_DIM)
    nm_shape = (1, 2 * N_HEADS, HEAD_DIM)
    scratch = [pltpu.VMEM((N_HEADS, HEAD_DIM, HEAD_DIM), F32), pltpu.VMEM((N_HEADS, HEAD_DIM), F32),
               pltpu.VMEM((N_HEADS, HEAD_DIM), F32)]
    consts = [full((ML_CONV_W, 2 * MIX_W)), full((1, LANES)), full((N_GATE_COLS, CHUNK))]

    cb, nm = pl.pallas_call(
        functools.partial(_ml_state_kernel, n=n),
        grid=(B, n),
        in_specs=[seg(SEG_MK, True), prev_spec(SEG_MK, True), next_spec(SEG_MK, True), seg(SEG_MV, True),
                  gcol_spec(True), grow_spec(True)] + consts,
        out_specs=[pl.BlockSpec(st_shape, lambda b, c: (b * n + n - 1 - c, 0, 0, 0)),
                   pl.BlockSpec(nm_shape, lambda b, c: (b * n + n - 1 - c, 0, 0))],
        out_shape=[jax.ShapeDtypeStruct((B * n, N_HEADS, HEAD_DIM, HEAD_DIM), BF16),
                   jax.ShapeDtypeStruct((B * n, 2 * N_HEADS, HEAD_DIM), F32)],
        scratch_shapes=scratch,
        compiler_params=_cparams(("arbitrary", "arbitrary")),
        name="mlstm_state",
    )(proj, proj, proj, proj, gates, gates_t, cw, bias_col, bias_row)

    return pl.pallas_call(
        functools.partial(_ml_out_kernel, n=n),
        grid=(B, n),
        in_specs=[seg(SEG_MQ, False), prev_spec(SEG_MQ, False), next_spec(SEG_MQ, False),
                  seg(SEG_MK, False), prev_spec(SEG_MK, False), next_spec(SEG_MK, False),
                  seg(SEG_MV, False), seg(SEG_MO, False), gcol_spec(False), grow_spec(False),
                  pl.BlockSpec(st_shape, lambda b, c: (b * n + c, 0, 0, 0)),
                  pl.BlockSpec(nm_shape, lambda b, c: (b * n + c, 0, 0))]
        + consts + [full((N_HEADS, HEAD_DIM, CHUNK))],
        out_specs=pl.BlockSpec((CHUNK, MIX_W), lambda b, c: (b * n + c, 0)),
        out_shape=jax.ShapeDtypeStruct((T, MIX_W), BF16),
        scratch_shapes=scratch,
        compiler_params=_cparams(("arbitrary", "arbitrary")),
        name="mlstm_out",
    )(proj, proj, proj, proj, proj, proj, proj, proj, gates, gates_t, cb, nm,
      cw, bias_col, bias_row, norm_wt)


def _branch_kernel(oret_ref, ona_ref, oml_ref, ga_ref, gb_ref, gc_ref, wb_ref, bgb_ref, m_ref):
    o_refs = (oret_ref, ona_ref, oml_ref)
    g_refs = (ga_ref, gb_ref, gc_ref)
    for n in range(D_MODEL // SEG_W):
        cs = slice(n * SEG_W, (n + 1) * SEG_W)
        acc = None
        for i in range(N_BRANCH):
            z = g_refs[i][:, cs].astype(F32) + bgb_ref[i:i + 1, cs]
            term = _sigmoid(z) * _dot(o_refs[i][...], wb_ref[i, :, cs])
            acc = term if acc is None else acc + term
        m_ref[:, cs] = acc.astype(BF16)


def _outproj_router_kernel(x_ref, m_ref, wout_ref, n2w_ref, rwh_ref, rwl_ref, rb_ref,
                           h_ref, hn_ref, route_ref, cnt_ref, cnt_scr):
    h = x_ref[...] + _dot(m_ref[...], wout_ref[...])
    h_ref[...] = h
    hn = h * lax.rsqrt(jnp.mean(h * h, -1, keepdims=True) + EPS) * n2w_ref[...]
    hn_ref[...] = hn

    hn_hi = hn.astype(BF16)
    hn_lo = (hn - hn_hi.astype(F32)).astype(BF16)
    logits = (_dot(hn_hi, rwh_ref[...]) + _dot(hn_lo, rwh_ref[...]) + _dot(hn_hi, rwl_ref[...])) + rb_ref[...]
    lane_i = lax.broadcasted_iota(jnp.int32, logits.shape, 1)
    lane = lane_i.astype(F32)
    lane_grp = jnp.right_shift(lane_i - N_GROUPS, 3).astype(F32)
    gmask = lane_i < N_GROUPS
    lg = jnp.where(gmask, logits, NEG)
    mg = jnp.max(lg, -1, keepdims=True)
    p_top = 1.0 / jnp.sum(jnp.where(gmask, jnp.exp(lg - mg), 0.0), -1, keepdims=True)
    grp = jnp.min(jnp.where(lg == mg, lane, float(LANES)), -1, keepdims=True)
    emask = (lane_i >= N_GROUPS) & (lane_i < N_GROUPS + N_EXPERTS) & (lane_grp == grp)
    le = jnp.where(emask, logits, NEG)
    m1 = jnp.max(le, -1, keepdims=True)
    i1 = jnp.min(jnp.where(le == m1, lane, float(LANES)), -1, keepdims=True)
    le2 = jnp.where(lane == i1, NEG, le)
    m2 = jnp.max(le2, -1, keepdims=True)
    i2 = jnp.min(jnp.where(le2 == m2, lane, float(LANES)), -1, keepdims=True)
    e2 = jnp.exp(m2 - m1)
    gate1 = p_top / (1.0 + e2)
    gate2 = p_top * e2 / (1.0 + e2)

    @pl.when(pl.program_id(0) == 0)
    def _():
        cnt_scr[...] = jnp.zeros_like(cnt_scr)

    tm = logits.shape[0]
    hit1 = lane == i1
    hit2 = lane == i2
    onehot = jnp.where(hit1 | hit2, 1.0, 0.0)
    r_i = lax.broadcasted_iota(jnp.int32, (tm, tm), 0)
    c_i = lax.broadcasted_iota(jnp.int32, (tm, tm), 1)
    before = jnp.where(c_i < r_i, 1.0, 0.0).astype(BF16)
    prior = _dot(before, onehot.astype(BF16)) + cnt_scr[...]
    rank1 = jnp.sum(jnp.where(hit1, prior, 0.0), -1, keepdims=True)
    rank2 = jnp.sum(jnp.where(hit2, prior, 0.0), -1, keepdims=True)
    cnt = cnt_scr[...] + jnp.sum(onehot, 0, keepdims=True)
    cnt_scr[...] = cnt
    cnt_ref[...] = jnp.broadcast_to(cnt, cnt_ref.shape)

    cols = (i1 - N_GROUPS, i2 - N_GROUPS, gate1, gate2, rank1, rank2)
    route = jnp.zeros_like(logits)
    for ci, val in enumerate(cols):
        route = jnp.where(lane_i == ci, val, route)
    route_ref[...] = route


MERGE_ROWS = 512


def _merge(x2d, o_ret, o_na, o_ml, proj, wb, bgb, wout, n2w, rw_hi, rw_lo, rb):
    T = x2d.shape[0]
    tm = MERGE_ROWS
    full = lambda shape: pl.BlockSpec(shape, lambda i: (0,) * len(shape), pipeline_mode=pl.Buffered(1))
    row = lambda w: pl.BlockSpec((tm, w), lambda i: (i, 0))
    gate = lambda s: pl.BlockSpec((tm, D_MODEL), lambda i: (i, s // 2))
    merged = pl.pallas_call(
        _branch_kernel,
        grid=(T // tm,),
        in_specs=[row(MIX_W), row(MIX_W), row(MIX_W), gate(SEG_GA), gate(SEG_GB), gate(SEG_GC),
                  full((N_BRANCH, MIX_W, D_MODEL)), full((N_BRANCH, D_MODEL))],
        out_specs=row(D_MODEL),
        out_shape=jax.ShapeDtypeStruct((T, D_MODEL), BF16),
        compiler_params=_cparams(("arbitrary",)),
        name="branch_merge",
    )(o_ret, o_na, o_ml, proj, proj, proj, wb, bgb)
    return pl.pallas_call(
        _outproj_router_kernel,
        grid=(T // tm,),
        in_specs=[row(D_MODEL), row(D_MODEL), full((D_MODEL, D_MODEL)), full((1, D_MODEL)),
                  full((D_MODEL, LANES)), full((D_MODEL, LANES)), full((1, LANES))],
        out_specs=[row(D_MODEL), row(D_MODEL), row(LANES), pl.BlockSpec((8, LANES), lambda i: (0, 0))],
        out_shape=[jax.ShapeDtypeStruct((T, D_MODEL), F32), jax.ShapeDtypeStruct((T, D_MODEL), F32),
                   jax.ShapeDtypeStruct((T, LANES), F32), jax.ShapeDtypeStruct((8, LANES), F32)],
        scratch_shapes=[pltpu.VMEM((1, LANES), F32)],
        compiler_params=_cparams(("arbitrary",)),
        name="outproj_router",
    )(x2d, merged, wout, n2w, rw_hi, rw_lo, rb)


def _row_copy(src_hbm, row, dst, r, sem):
    return pltpu.make_async_copy(src_hbm.at[pl.ds(row, 1), :], dst.at[pl.ds(r, 1), :], sem)


def _ffn_kernel(blk_e_ref, n_used_ref, src_ref, hn_hbm, wg_ref, wu_ref, wd_ref, y_ref, xbuf, sem):
    i = pl.program_id(0)
    n_used = n_used_ref[0]
    slot = i % 2

    def wait_block(s):
        pltpu.make_async_copy(hn_hbm.at[pl.ds(0, MOE_ROWS), :], xbuf.at[s], sem.at[s]).wait()

    @pl.when(i == 0)
    def _():
        def body(r, carry):
            _row_copy(hn_hbm, src_ref[r], xbuf.at[0], r, sem.at[0]).start()
            return carry
        lax.fori_loop(0, MOE_ROWS, body, 0, unroll=8)

    @pl.when(i < n_used)
    def _():
        wait_block(slot)
        x = xbuf[slot].astype(BF16)
        for r in range(MOE_ROWS):
            _row_copy(hn_hbm, src_ref[(i + 1) * MOE_ROWS + r], xbuf.at[1 - slot], r, sem.at[1 - slot]).start()
        a = _dot(x, wg_ref[0])
        hid = (a * _sigmoid(a)) * _dot(x, wu_ref[0])
        y_ref[...] = _dot(hid.astype(BF16), wd_ref[0])

    @pl.when(i >= n_used)
    def _():
        y_ref[...] = jnp.zeros_like(y_ref)

    @pl.when(i == n_used)
    def _():
        wait_block(slot)


def _expert_ffn(hn, src, blk_e, n_used, wg, wu, wd):
    n_blocks = src.shape[0] // MOE_ROWS
    grid_spec = pltpu.PrefetchScalarGridSpec(
        num_scalar_prefetch=3,
        grid=(n_blocks,),
        in_specs=[
            pl.BlockSpec(memory_space=pl.ANY),
            pl.BlockSpec((1, D_MODEL, D_EXPERT), lambda i, be, nu, sr: (be[i], 0, 0)),
            pl.BlockSpec((1, D_MODEL, D_EXPERT), lambda i, be, nu, sr: (be[i], 0, 0)),
            pl.BlockSpec((1, D_EXPERT, D_MODEL), lambda i, be, nu, sr: (be[i], 0, 0)),
        ],
        out_specs=pl.BlockSpec((MOE_ROWS, D_MODEL), lambda i, be, nu, sr: (i, 0)),
        scratch_shapes=[pltpu.VMEM((2, MOE_ROWS, D_MODEL), F32), pltpu.SemaphoreType.DMA((2,))],
    )
    return pl.pallas_call(
        _ffn_kernel,
        grid_spec=grid_spec,
        out_shape=jax.ShapeDtypeStruct((n_blocks * MOE_ROWS, D_MODEL), F32),
        compiler_params=_cparams(("arbitrary",), disable_bounds_checks=True),
        name="expert_ffn",
    )(blk_e, n_used, src, hn, wg, wu, wd)


COMBINE_ROWS = 256


def _combine_kernel(dest_ref, h_ref, route_ref, yb_hbm, out_ref, ybuf, sem):
    i = pl.program_id(0)
    slot = i % 2

    def start_gather(tile, to_slot):
        def body(r, carry):
            a = (tile * COMBINE_ROWS + r) * TOP_K_INNER
            for k in range(TOP_K_INNER):
                _row_copy(yb_hbm, dest_ref[a + k], ybuf.at[to_slot, k], r, sem.at[to_slot]).start()
            return carry
        lax.fori_loop(0, COMBINE_ROWS, body, 0, unroll=8)

    @pl.when(i == 0)
    def _():
        start_gather(0, 0)

    for k in range(TOP_K_INNER):
        pltpu.make_async_copy(yb_hbm.at[pl.ds(0, COMBINE_ROWS), :], ybuf.at[slot, k], sem.at[slot]).wait()

    @pl.when(i + 1 < pl.num_programs(0))
    def _():
        start_gather(i + 1, 1 - slot)

    route = route_ref[...]
    out_ref[...] = h_ref[...] + route[:, 2:3] * ybuf[slot, 0] + route[:, 3:4] * ybuf[slot, 1]


def _combine(h, route, yb, dest):
    T = h.shape[0]
    grid_spec = pltpu.PrefetchScalarGridSpec(
        num_scalar_prefetch=1,
        grid=(T // COMBINE_ROWS,),
        in_specs=[
            pl.BlockSpec((COMBINE_ROWS, D_MODEL), lambda i, d: (i, 0)),
            pl.BlockSpec((COMBINE_ROWS, LANES), lambda i, d: (i, 0)),
            pl.BlockSpec(memory_space=pl.ANY),
        ],
        out_specs=pl.BlockSpec((COMBINE_ROWS, D_MODEL), lambda i, d: (i, 0)),
        scratch_shapes=[pltpu.VMEM((2, TOP_K_INNER, COMBINE_ROWS, D_MODEL), F32), pltpu.SemaphoreType.DMA((2,))],
    )
    return pl.pallas_call(
        _combine_kernel,
        grid_spec=grid_spec,
        out_shape=jax.ShapeDtypeStruct((T, D_MODEL), F32),
        compiler_params=_cparams(("arbitrary",), disable_bounds_checks=True),
        name="moe_combine",
    )(dest, h, route, yb)


def _moe(h, hn, route, counts, wg, wu, wd):
    T = h.shape[0]
    n_assign = T * TOP_K_INNER
    expert = route[:, 0:2].astype(jnp.int32)
    rank = route[:, 4:6].astype(jnp.int32)
    cnt = counts[0, N_GROUPS:N_GROUPS + N_EXPERTS].astype(jnp.int32)
    padded = (cnt + MOE_ROWS - 1) // MOE_ROWS * MOE_ROWS
    pend = jnp.cumsum(padded)
    pstart = pend - padded
    dest = (pstart[expert] + rank).reshape(-1)
    n_blocks = (n_assign + N_EXPERTS * (MOE_ROWS - 1) + MOE_ROWS - 1) // MOE_ROWS + 1
    blk_start = jnp.arange(n_blocks, dtype=jnp.int32) * MOE_ROWS
    blk_e = jnp.minimum(jnp.sum((pend[None, :] <= blk_start[:, None]).astype(jnp.int32), 1), N_EXPERTS - 1)
    n_used = (pend[-1] // MOE_ROWS).astype(jnp.int32).reshape(1)
    flat_t = jnp.repeat(jnp.arange(T, dtype=jnp.int32), TOP_K_INNER)
    src = jnp.zeros((n_blocks * MOE_ROWS,), jnp.int32).at[dest].set(flat_t)
    yb = _expert_ffn(hn, src, blk_e, n_used, wg, wu, wd)
    return _combine(h, route, yb, dest)


def _rope_tables(L):
    freqs = ROPE_BASE ** (-jnp.arange(0, HEAD_DIM, 2, dtype=F32) / HEAD_DIM)
    ang = jnp.arange(L, dtype=F32)[:, None] * freqs[None]
    cos, sin = jnp.cos(ang), jnp.sin(ang)
    return jnp.concatenate([cos, cos], -1), jnp.concatenate([-sin, sin], -1)


def _prep_layer(p):
    w_in = p["w_in"]
    sp = np.cumsum([MIX_W] * 11 + [2 * N_HEADS, 2 * N_HEADS] + [D_MODEL] * N_BRANCH)[:-1].tolist()
    (rq, rk, rv, rg, nq, nk, nv, mq, mk, mv, mo, mi, mf, ga, gb, gc) = jnp.split(w_in, sp, axis=-1)
    w_main = jnp.concatenate([ga, gb, gc, rq, rk, rv, rg, nq, nk, nv, mq, mk, mv, mo], -1).astype(BF16)
    wif = jnp.concatenate([mi, mf], -1)
    wif_pad = jnp.zeros((D_MODEL, LANES), BF16).at[:, :N_GATE_COLS].set(wif.astype(BF16))
    rw = jnp.zeros((D_MODEL, LANES), F32).at[:, :N_GROUPS].set(p["router_g_w"]) \
        .at[:, N_GROUPS:N_GROUPS + N_EXPERTS].set(p["router_e_w"])
    rb = jnp.zeros((1, LANES), F32).at[0, :N_GROUPS].set(p["router_g_b"]) \
        .at[0, N_GROUPS:N_GROUPS + N_EXPERTS].set(p["router_e_b"])
    return dict(
        nw=p["norm1_w"].astype(F32).reshape(1, D_MODEL), w_main=w_main, wif=wif_pad, wift=wif.T.astype(BF16),
        qn=p["na_q_norm"].astype(F32).reshape(1, HEAD_DIM), kn=p["na_k_norm"].astype(F32).reshape(1, HEAD_DIM),
        wb=p["w_branch"].astype(BF16), bgb=p["branch_gate_b"].astype(F32), wout=p["w_out"].astype(BF16),
        n2w=p["norm2_w"].astype(F32).reshape(1, D_MODEL), rb=rb,
        rw_hi=rw.astype(BF16), rw_lo=(rw - rw.astype(BF16).astype(F32)).astype(BF16),
        wg=p["exp_w_gate"].astype(BF16), wu=p["exp_w_up"].astype(BF16), wd=p["exp_w_down"].astype(BF16),
    )


def _layer(x2d, B, L, p, q, rope):
    proj, gates, gates_t = _inproj(x2d, L, q["nw"], q["w_main"], q["wif"], q["wift"], rope[0], rope[1],
                                   q["qn"], q["kn"])
    o_ret = _retention(proj, B, L, p["ret_decay"], p["ret_norm_w"])
    o_na = _neighborhood(proj, B, L, p["na_rpb"])
    o_ml = _mlstm(proj, gates, gates_t, B, L, p["ml_conv"], p["ml_igate_b"], p["ml_fgate_b"], p["ml_norm_w"])
    h, hn, route, counts = _merge(x2d, o_ret, o_na, o_ml, proj, q["wb"], q["bgb"], q["wout"], q["n2w"],
                                  q["rw_hi"], q["rw_lo"], q["rb"])
    return _moe(h, hn, route, counts, q["wg"], q["wu"], q["wd"])


_PARAM_NAMES = ("norm1_w", "w_in", "ret_decay", "ret_norm_w", "na_q_norm", "na_k_norm", "na_rpb", "ml_conv",
                "ml_igate_b", "ml_fgate_b", "ml_norm_w", "w_branch", "branch_gate_b", "w_out", "norm2_w",
                "router_g_w", "router_g_b", "router_e_w", "router_e_b", "exp_w_gate", "exp_w_up", "exp_w_down")


def _run(x, layers, preps):
    B, L, D = x.shape
    rope = _rope_tables(L)
    x2d = x.reshape(B * L, D)
    for p, q in zip(layers, preps):
        x2d = _layer(x2d, B, L, p, q, rope)
    return x2d.reshape(B, L, D)


def kernel(x_prompt, x_sample, norm1_w, w_in, ret_decay, ret_norm_w, na_q_norm, na_k_norm, na_rpb, ml_conv,
           ml_igate_b, ml_fgate_b, ml_norm_w, w_branch, branch_gate_b, w_out, norm2_w, router_g_w, router_g_b,
           router_e_w, router_e_b, exp_w_gate, exp_w_up, exp_w_down):
    stacked = (norm1_w, w_in, ret_decay, ret_norm_w, na_q_norm, na_k_norm, na_rpb, ml_conv, ml_igate_b,
               ml_fgate_b, ml_norm_w, w_branch, branch_gate_b, w_out, norm2_w, router_g_w, router_g_b,
               router_e_w, router_e_b, exp_w_gate, exp_w_up, exp_w_down)
    depth = w_in.shape[0]
    layers = [dict(zip(_PARAM_NAMES, (a[l] for a in stacked))) for l in range(depth)]
    preps = [_prep_layer(p) for p in layers]
    return (_run(x_prompt, layers, preps), _run(x_sample, layers, preps))
```

```python
import functools

import numpy as np
import jax
import jax.numpy as jnp
from jax import lax
from jax.experimental import pallas as pl
from jax.experimental.pallas import tpu as pltpu

D_MODEL = 2048
HEAD_DIM = 128
MIX_W = D_MODEL // 2
N_HEADS = MIX_W // HEAD_DIM
N_BRANCH = 3
CHUNK = 128
ROPE_BASE = 10000.0
GRID_W = 64
WIN_ROWS = 8
WIN_COLS = 16
ML_CONV_W = 3
N_GROUPS = 4
EXPERTS_PER_GROUP = 8
N_EXPERTS = N_GROUPS * EXPERTS_PER_GROUP
TOP_K_INNER = 2
D_EXPERT = D_MODEL // 2
EPS = 1e-6
NEG = -1e30

F32 = jnp.float32
BF16 = jnp.bfloat16

LANES = 128
SEG_W = MIX_W
SEG_GA, SEG_GB, SEG_GC = 0, 2, 4
SEG_RQ, SEG_RK, SEG_RV, SEG_RG = 6, 7, 8, 9
SEG_NQ, SEG_NK, SEG_NV = 10, 11, 12
SEG_MQ, SEG_MK, SEG_MV, SEG_MO = 13, 14, 15, 16
N_SEG = 17
N_GATE_COLS = 4 * N_HEADS

NA_QROWS = 2
NA_KBLKS = 5
NA_KW = NA_KBLKS * CHUNK
MOE_ROWS = 256

VMEM_LIMIT = 56 * 1024 * 1024


def _cparams(sem, **kw):
    return pltpu.CompilerParams(dimension_semantics=sem, vmem_limit_bytes=VMEM_LIMIT, **kw)


def _sigmoid(z):
    return 0.5 * jnp.tanh(0.5 * z) + 0.5


def _hs(h):
    return slice(h * HEAD_DIM, (h + 1) * HEAD_DIM)


def _dot(a, b):
    return jnp.dot(a, b, preferred_element_type=F32)


def _dot_nt(a, b):
    return lax.dot_general(a, b, (((1,), (1,)), ((), ())), preferred_element_type=F32)


def _dot_tn(a, b):
    return lax.dot_general(a, b, (((0,), (0,)), ((), ())), preferred_element_type=F32)


def _inproj_kernel(x_ref, nw_ref, w_ref, wif_ref, wift_ref, cos_ref, sin_ref, qn_ref, kn_ref,
                   proj_ref, g_ref, gt_ref, xn_ref):
    j = pl.program_id(1)

    @pl.when(j == 0)
    def _():
        x = x_ref[...]
        y = x * lax.rsqrt(jnp.mean(x * x, -1, keepdims=True) + EPS) * nw_ref[...]
        xn = y.astype(BF16)
        xn_ref[...] = xn
        g_ref[...] = _dot(xn, wif_ref[...])
        gt_ref[...] = _dot_nt(wift_ref[...], xn)

    acc = _dot(xn_ref[...], w_ref[...])

    is_rope = (j == SEG_RQ) | (j == SEG_RK)
    is_norm = (j == SEG_NQ) | (j == SEG_NK)

    @pl.when(is_rope)
    def _():
        scale = jnp.where(j == SEG_RK, HEAD_DIM ** -0.5, 1.0).astype(F32)
        cos = cos_ref[...]
        sin = sin_ref[...]
        for h in range(N_HEADS):
            xh = acc[:, _hs(h)]
            r = xh * cos + pltpu.roll(xh, HEAD_DIM // 2, 1) * sin
            proj_ref[:, _hs(h)] = (r * scale).astype(BF16)

    @pl.when(is_norm)
    def _():
        w = jnp.where(j == SEG_NQ, qn_ref[...] * (HEAD_DIM ** -0.5), kn_ref[...])
        for h in range(N_HEADS):
            xh = acc[:, _hs(h)]
            y = xh * lax.rsqrt(jnp.mean(xh * xh, -1, keepdims=True) + EPS) * w
            proj_ref[:, _hs(h)] = y.astype(BF16)

    @pl.when(jnp.logical_not(is_rope | is_norm))
    def _():
        proj_ref[...] = acc.astype(BF16)


def _inproj(x2d, L, nw, w_main, wif, wift, cos_t, sin_t, qn, kn):
    T = x2d.shape[0]
    tm = min(1024, L)
    nlt = L // tm
    return pl.pallas_call(
        _inproj_kernel,
        grid=(T // tm, N_SEG),
        in_specs=[
            pl.BlockSpec((tm, D_MODEL), lambda i, j: (i, 0)),
            pl.BlockSpec((1, D_MODEL), lambda i, j: (0, 0)),
            pl.BlockSpec((D_MODEL, SEG_W), lambda i, j: (0, j)),
            pl.BlockSpec((D_MODEL, LANES), lambda i, j: (0, 0)),
            pl.BlockSpec((N_GATE_COLS, D_MODEL), lambda i, j: (0, 0)),
            pl.BlockSpec((tm, HEAD_DIM), lambda i, j: (i % nlt, 0)),
            pl.BlockSpec((tm, HEAD_DIM), lambda i, j: (i % nlt, 0)),
            pl.BlockSpec((1, HEAD_DIM), lambda i, j: (0, 0)),
            pl.BlockSpec((1, HEAD_DIM), lambda i, j: (0, 0)),
        ],
        out_specs=[
            pl.BlockSpec((tm, SEG_W), lambda i, j: (i, j)),
            pl.BlockSpec((tm, LANES), lambda i, j: (i, 0)),
            pl.BlockSpec((N_GATE_COLS, tm), lambda i, j: (0, i)),
        ],
        out_shape=[
            jax.ShapeDtypeStruct((T, N_SEG * SEG_W), BF16),
            jax.ShapeDtypeStruct((T, LANES), F32),
            jax.ShapeDtypeStruct((N_GATE_COLS, T), F32),
        ],
        scratch_shapes=[pltpu.VMEM((tm, D_MODEL), BF16)],
        compiler_params=_cparams(("arbitrary", "arbitrary")),
        name="inproj",
    )(x2d, nw, w_main, wif, wift, cos_t, sin_t, qn, kn)


def _ret_state_kernel(k_ref, v_ref, kb_ref, cdb_ref, sb_ref, s_scr):
    c = pl.program_id(1)

    @pl.when(c == 0)
    def _():
        s_scr[...] = jnp.zeros_like(s_scr)

    sb_ref[0] = s_scr[...].astype(BF16)
    kk = (k_ref[...].astype(F32) * kb_ref[...]).astype(BF16)
    v = v_ref[...]
    for h in range(N_HEADS):
        kv = _dot_tn(kk[:, _hs(h)], v[:, _hs(h)])
        s_scr[h] = s_scr[h] * cdb_ref[h:h + 1, :] + kv


def _ret_out_kernel(q_ref, k_ref, v_ref, g_ref, sb_ref, dmat_ref, qf_ref, qb_ref, kf_ref, cdf_ref, nw_ref,
                    o_ref, s_scr):
    c = pl.program_id(1)

    @pl.when(c == 0)
    def _():
        s_scr[...] = jnp.zeros_like(s_scr)

    q = q_ref[...]
    k = k_ref[...]
    v = v_ref[...]
    qf32 = q.astype(F32)
    q_fwd = (qf32 * qf_ref[...]).astype(BF16)
    q_bwd = (qf32 * qb_ref[...]).astype(BF16)
    k_end = (k.astype(F32) * kf_ref[...]).astype(BF16)
    g = g_ref[...].astype(F32)
    for h in range(N_HEADS):
        hs = _hs(h)
        s = _dot_nt(q[:, hs], k[:, hs]) * dmat_ref[h]
        o = _dot(s.astype(BF16), v[:, hs])
        o = o + _dot(q_fwd[:, hs], s_scr[h].astype(BF16))
        o = o + _dot(q_bwd[:, hs], sb_ref[0, h])
        s_scr[h] = s_scr[h] * cdf_ref[h:h + 1, :] + _dot_tn(k_end[:, hs], v[:, hs])
        y = o * lax.rsqrt(jnp.mean(o * o, -1, keepdims=True) + EPS) * nw_ref[h:h + 1, :]
        gh = g[:, hs]
        o_ref[:, hs] = (y * (gh * _sigmoid(gh))).astype(BF16)


def _retention(proj, B, L, ret_decay, ret_norm_w):
    n = L // CHUNK
    T = B * L
    lg = jax.nn.log_sigmoid(ret_decay.astype(F32))
    idx = jnp.arange(CHUNK, dtype=F32)
    diff = idx[:, None] - idx[None, :]
    dmat = jnp.where(diff >= 0, jnp.exp(jnp.maximum(diff, 0.0) * lg[0][:, None, None]),
                     jnp.exp(jnp.maximum(-diff, 0.0) * lg[1][:, None, None]))

    def lane_tab(e):
        return jnp.repeat(jnp.exp(e).T, HEAD_DIM, axis=1)

    qf_tab = lane_tab((idx + 1.0)[None, :] * lg[0][:, None])
    qb_tab = lane_tab((CHUNK - idx)[None, :] * lg[1][:, None])
    kf_tab = lane_tab((CHUNK - 1.0 - idx)[None, :] * lg[0][:, None])
    kb_tab = lane_tab(idx[None, :] * lg[1][:, None])
    cdf = jnp.broadcast_to(jnp.exp(CHUNK * lg[0])[:, None], (N_HEADS, HEAD_DIM))
    cdb = jnp.broadcast_to(jnp.exp(CHUNK * lg[1])[:, None], (N_HEADS, HEAD_DIM))

    def seg(s, rev=False):
        if rev:
            return pl.BlockSpec((CHUNK, SEG_W), lambda b, c: (b * n + n - 1 - c, s))
        return pl.BlockSpec((CHUNK, SEG_W), lambda b, c: (b * n + c, s))

    full = lambda shape: pl.BlockSpec(shape, lambda b, c: (0,) * len(shape))
    st_shape = (1, N_HEADS, HEAD_DIM, HEAD_DIM)

    sb = pl.pallas_call(
        _ret_state_kernel,
        grid=(B, n),
        in_specs=[seg(SEG_RK, True), seg(SEG_RV, True), full((CHUNK, MIX_W)), full((N_HEADS, HEAD_DIM))],
        out_specs=pl.BlockSpec(st_shape, lambda b, c: (b * n + n - 1 - c, 0, 0, 0)),
        out_shape=jax.ShapeDtypeStruct((B * n, N_HEADS, HEAD_DIM, HEAD_DIM), BF16),
        scratch_shapes=[pltpu.VMEM((N_HEADS, HEAD_DIM, HEAD_DIM), F32)],
        compiler_params=_cparams(("arbitrary", "arbitrary")),
        name="ret_state",
    )(proj, proj, kb_tab, cdb)

    return pl.pallas_call(
        _ret_out_kernel,
        grid=(B, n),
        in_specs=[seg(SEG_RQ), seg(SEG_RK), seg(SEG_RV), seg(SEG_RG),
                  pl.BlockSpec(st_shape, lambda b, c: (b * n + c, 0, 0, 0)),
                  full((N_HEADS, CHUNK, CHUNK)), full((CHUNK, MIX_W)), full((CHUNK, MIX_W)),
                  full((CHUNK, MIX_W)), full((N_HEADS, HEAD_DIM)), full((N_HEADS, HEAD_DIM))],
        out_specs=pl.BlockSpec((CHUNK, MIX_W), lambda b, c: (b * n + c, 0)),
        out_shape=jax.ShapeDtypeStruct((T, MIX_W), BF16),
        scratch_shapes=[pltpu.VMEM((N_HEADS, HEAD_DIM, HEAD_DIM), F32)],
        compiler_params=_cparams(("arbitrary", "arbitrary")),
        name="ret_out",
    )(proj, proj, proj, proj, sb, dmat, qf_tab, qb_tab, kf_tab, cdf, ret_norm_w.astype(F32))


def _na_index_tables(nqb):
    rows = NA_QROWS * nqb
    qi = np.arange(CHUNK)
    ki = np.arange(NA_KW)

    def one(qb):
        kb = NA_QROWS * int(np.clip(qb - 2, 0, nqb - NA_KBLKS))
        r = (NA_QROWS * qb + qi // GRID_W)[:, None]
        qc = (qi % GRID_W)[:, None]
        kr = (kb + ki // GRID_W)[None, :]
        kc = (ki % GRID_W)[None, :]
        rs = np.clip(r - WIN_ROWS // 2, 0, rows - WIN_ROWS)
        cs = np.clip(qc - WIN_COLS // 2, 0, GRID_W - WIN_COLS)
        valid = (kr >= rs) & (kr < rs + WIN_ROWS) & (kc >= cs) & (kc < cs + WIN_COLS)
        dr = np.clip(kr - r + WIN_ROWS - 1, 0, 2 * WIN_ROWS - 2)
        dc = np.clip(kc - qc + WIN_COLS - 1, 0, 2 * WIN_COLS - 2)
        return dr + 0 * dc, dc + 0 * dr, valid

    reps = [0, 1, 2, nqb - 2, nqb - 1]
    tabs = [one(qb) for qb in reps]
    for qb in range(2, nqb - 2):
        t = one(qb)
        assert all(np.array_equal(a, b) for a, b in zip(t, tabs[2]))
    return tuple(np.stack([t[i] for t in tabs]) for i in range(3))


def _na_kernel(q_ref, k0, k1, k2, k3, k4, v0, v1, v2, v3, v4, bias_ref, o_ref):
    k_refs = (k0, k1, k2, k3, k4)
    v_refs = (v0, v1, v2, v3, v4)
    for h in range(N_HEADS):
        hs = _hs(h)
        q = q_ref[:, hs]
        k_all = jnp.concatenate([r[:, hs] for r in k_refs], 0)
        v_all = jnp.concatenate([r[:, hs] for r in v_refs], 0)
        s = _dot_nt(q, k_all) + bias_ref[0, h]
        p = jnp.exp(s - jnp.max(s, -1, keepdims=True))
        o = _dot(p.astype(BF16), v_all)
        o_ref[:, hs] = (o / jnp.sum(p, -1, keepdims=True)).astype(BF16)


def _neighborhood(proj, B, L, rpb):
    nqb = L // CHUNK
    assert nqb >= NA_KBLKS and L % (GRID_W * NA_QROWS) == 0
    T = B * L
    dr, dc, valid = _na_index_tables(nqb)
    n_cls = dr.shape[0]
    kr_n = NA_KW // GRID_W
    dr6 = dr.reshape(n_cls, NA_QROWS, GRID_W, kr_n, GRID_W)
    dc6 = dc.reshape(n_cls, NA_QROWS, GRID_W, kr_n, GRID_W)
    dr_s = dr6[:, :, 0, :, 0]
    dc_s = dc6[0, 0, :, 0, :]
    assert np.array_equal(dr6, np.broadcast_to(dr_s[:, :, None, :, None], dr6.shape))
    assert np.array_equal(dc6, np.broadcast_to(dc_s[None, None, :, None, :], dc6.shape))
    sel_r = (dr_s[..., None] == np.arange(2 * WIN_ROWS - 1)).astype(np.float32)
    sel_c = (np.arange(2 * WIN_COLS - 1)[:, None, None] == dc_s[None]).astype(np.float32)
    hp = lax.Precision.HIGHEST
    t = jnp.einsum('cajr,hrd->cajhd', sel_r, rpb.astype(F32), precision=hp)
    bias = jnp.einsum('cajhd,dqk->chaqjk', t, sel_c, precision=hp).reshape(n_cls, N_HEADS, CHUNK, NA_KW)
    bias = jnp.where(jnp.asarray(valid)[:, None], bias, NEG)

    def kspec(s, j):
        return pl.BlockSpec((CHUNK, SEG_W),
                            lambda b, qb: (b * nqb + jnp.clip(qb - 2, 0, nqb - NA_KBLKS) + j, s))

    def cls(qb):
        return jnp.where(qb < 2, qb, jnp.where(qb >= nqb - 2, qb - (nqb - 5), 2))

    return pl.pallas_call(
        _na_kernel,
        grid=(B, nqb),
        in_specs=[pl.BlockSpec((CHUNK, SEG_W), lambda b, qb: (b * nqb + qb, SEG_NQ))]
        + [kspec(SEG_NK, j) for j in range(NA_KBLKS)]
        + [kspec(SEG_NV, j) for j in range(NA_KBLKS)]
        + [pl.BlockSpec((1, N_HEADS, CHUNK, NA_KW), lambda b, qb: (cls(qb), 0, 0, 0))],
        out_specs=pl.BlockSpec((CHUNK, MIX_W), lambda b, qb: (b * nqb + qb, 0)),
        out_shape=jax.ShapeDtypeStruct((T, MIX_W), BF16),
        compiler_params=_cparams(("arbitrary", "arbitrary")),
        name="neighborhood",
    )(*([proj] * (1 + 2 * NA_KBLKS)), bias)


HALO = 16


def _log_sigmoid(x):
    return jnp.minimum(x, 0.0) - jnp.log1p(jnp.exp(-jnp.abs(x)))


def _conv_silu(x_ref, prev_ref, next_ref, w_ref, col0, c, n):
    x = x_ref[...].astype(F32)
    row = lax.broadcasted_iota(jnp.int32, x.shape, 0)
    prev_row = prev_ref[HALO - 1:HALO, :].astype(F32) * (c > 0).astype(F32)
    next_row = next_ref[0:1, :].astype(F32) * (c < n - 1).astype(F32)
    x_prev = jnp.where(row == 0, prev_row, pltpu.roll(x, 1, 0))
    x_next = jnp.where(row == CHUNK - 1, next_row, pltpu.roll(x, CHUNK - 1, 0))
    cs = slice(col0, col0 + MIX_W)
    y = x_prev * w_ref[0:1, cs] + x * w_ref[1:2, cs] + x_next * w_ref[2:3, cs]
    return y * _sigmoid(y)


def _tri():
    r = lax.broadcasted_iota(jnp.int32, (CHUNK, CHUNK), 0)
    c = lax.broadcasted_iota(jnp.int32, (CHUNK, CHUNK), 1)
    return (c <= r).astype(F32), (c >= r).astype(F32)


def _hp_dot(a, b):
    return jnp.dot(a, b, preferred_element_type=F32, precision=lax.Precision.HIGHEST)


FWD_LANE = 2 * N_HEADS
BWD_LANE = 3 * N_HEADS


def _gate_dense(gc_ref, gr_ref, bc_ref, br_ref):
    low, up = _tri()
    g_col = gc_ref[...] + bc_ref[...]
    lf_col = _log_sigmoid(g_col)
    lf_row = _log_sigmoid(gr_ref[...] + br_ref[...])
    b_col = _hp_dot(low, lf_col)
    b_row = _hp_dot(lf_row, up)
    tot_row = b_col[CHUNK - 1:CHUNK, :]
    tot_col = b_row[:, CHUNK - 1:CHUNK]
    lane = lax.broadcasted_iota(jnp.int32, (CHUNK, LANES), 1)
    row = lax.broadcasted_iota(jnp.int32, (N_GATE_COLS, CHUNK), 0)
    bb_col = jnp.where(lane >= BWD_LANE, tot_row - b_col + lf_col, b_col)
    bb_row = jnp.where(row >= BWD_LANE, tot_col - b_row + lf_row, b_row)
    ck = pltpu.roll(g_col, FWD_LANE, 1) - bb_col
    a = tot_row + ck
    m_loc = jnp.max(a, 0, keepdims=True)
    ea = jnp.exp(a - m_loc)
    return bb_row, ck, tot_row, m_loc, ea


def _lane_bcast(x, l):
    return jnp.broadcast_to(x[:, l:l + 1], (x.shape[0], HEAD_DIM))


def _ml_state_update(kc_h, ct_loc_fn, ea, tot_row, m_loc, l, c_scr, n_scr, m_scr, h):
    m_p = m_scr[h:h + 1, 0:1]
    g_tot = tot_row[:, l:l + 1]
    m_l = m_loc[:, l:l + 1]
    kw = kc_h * _lane_bcast(ea, l)
    ct_loc = ct_loc_fn(kw.astype(BF16))
    n_loc = jnp.sum(kw, 0, keepdims=True)
    m_new = jnp.maximum(g_tot + m_p, m_l)
    sp = jnp.exp(g_tot + m_p - m_new)
    sl = jnp.exp(m_l - m_new)
    c_scr[h] = sp * c_scr[h] + sl * ct_loc
    n_scr[h:h + 1, :] = sp * n_scr[h:h + 1, :] + sl * n_loc
    m_scr[h:h + 1, :] = jnp.broadcast_to(m_new, (1, HEAD_DIM))


def _ml_init(c, c_scr, n_scr, m_scr):
    @pl.when(c == 0)
    def _():
        c_scr[...] = jnp.zeros_like(c_scr)
        n_scr[...] = jnp.zeros_like(n_scr)
        m_scr[...] = jnp.full_like(m_scr, NEG)


def _ml_state_kernel(k_ref, kp_ref, kn_ref, v_ref, gc_ref, gr_ref, cw_ref, bc_ref, br_ref,
                     cb_ref, nm_ref, c_scr, n_scr, m_scr, *, n):
    step = pl.program_id(1)
    _ml_init(step, c_scr, n_scr, m_scr)
    c = n - 1 - step
    cb_ref[0] = c_scr[...].astype(BF16)
    nm_ref[0, 0:N_HEADS, :] = n_scr[...]
    nm_ref[0, N_HEADS:2 * N_HEADS, :] = m_scr[...]

    kc = _conv_silu(k_ref, kp_ref, kn_ref, cw_ref, MIX_W, c, n) * (HEAD_DIM ** -0.5)
    v = v_ref[...]
    _, _, tot_row, m_loc, ea = _gate_dense(gc_ref, gr_ref, bc_ref, br_ref)
    for h in range(N_HEADS):
        v_h = v[:, _hs(h)]
        _ml_state_update(kc[:, _hs(h)], lambda kw: _dot_tn(v_h, kw), ea, tot_row, m_loc, BWD_LANE + h,
                         c_scr, n_scr, m_scr, h)


def _ml_direction(st, mask, b_q, ck_b, m_p, qn, vt_bf, ct_bf, q_bf):
    dmat = jnp.where(mask, b_q + ck_b, NEG)
    inter = b_q + m_p
    m_row = jnp.maximum(jnp.max(dmat, 0, keepdims=True), inter)
    s = st * jnp.exp(dmat - m_row)
    e_int = jnp.exp(inter - m_row)
    num = _dot(vt_bf, s.astype(BF16)) + e_int * _dot_nt(ct_bf, q_bf)
    den = jnp.sum(s, 0, keepdims=True) + e_int * qn
    return num / jnp.maximum(jnp.abs(den), jnp.exp(-m_row))


def _ml_out_kernel(q_ref, qp_ref, qn_ref, k_ref, kp_ref, kn_ref, v_ref, o_ref_in, gc_ref, gr_ref,
                   cb_ref, nm_ref, cw_ref, bc_ref, br_ref, nwt_ref,
                   out_ref, c_scr, n_scr, m_scr, *, n):
    c = pl.program_id(1)
    _ml_init(c, c_scr, n_scr, m_scr)
    qc = _conv_silu(q_ref, qp_ref, qn_ref, cw_ref, 0, c, n)
    kc = _conv_silu(k_ref, kp_ref, kn_ref, cw_ref, MIX_W, c, n) * (HEAD_DIM ** -0.5)
    q_bf = qc.astype(BF16)
    k_bf = kc.astype(BF16)
    v = v_ref[...]
    og = o_ref_in[...].astype(F32)
    bb_row, ck, tot_row, m_loc, ea = _gate_dense(gc_ref, gr_ref, bc_ref, br_ref)
    key = lax.broadcasted_iota(jnp.int32, (CHUNK, CHUNK), 0)
    qry = lax.broadcasted_iota(jnp.int32, (CHUNK, CHUNK), 1)
    causal = key <= qry
    anti = key >= qry
    pad = jnp.zeros((8 - 2, HEAD_DIM), F32)
    for h in range(N_HEADS):
        hs = _hs(h)
        lf, lb = FWD_LANE + h, BWD_LANE + h
        qh = q_bf[:, hs]
        vt_bf = v[:, hs].astype(F32).T.astype(BF16)
        st = _dot_nt(k_bf[:, hs], qh)
        n_prev = jnp.concatenate([n_scr[h:h + 1, :], nm_ref[0, h:h + 1, :], pad], 0).astype(BF16)
        qn = _dot_nt(n_prev, qh)
        h_f = _ml_direction(st, causal, bb_row[lf:lf + 1, :], _lane_bcast(ck, lf), m_scr[h:h + 1, 0:1],
                            qn[0:1, :], vt_bf, c_scr[h].astype(BF16), qh)
        h_b = _ml_direction(st, anti, bb_row[lb:lb + 1, :], _lane_bcast(ck, lb),
                            nm_ref[0, N_HEADS + h:N_HEADS + h + 1, 0:1], qn[1:2, :], vt_bf, cb_ref[0, h], qh)
        _ml_state_update(kc[:, hs], lambda kw: _dot(vt_bf, kw), ea, tot_row, m_loc, lf, c_scr, n_scr, m_scr, h)
        ot = h_f + h_b
        yt = ot * lax.rsqrt(jnp.mean(ot * ot, 0, keepdims=True) + EPS) * nwt_ref[h]
        out_ref[:, hs] = (yt.T * _sigmoid(og[:, hs])).astype(BF16)


def _mlstm(proj, gates, gates_t, B, L, conv_w, ig_b, fg_b, norm_w):
    n = L // CHUNK
    T = B * L
    hb = CHUNK // HALO
    bias = jnp.concatenate([ig_b.astype(F32).reshape(-1), fg_b.astype(F32).reshape(-1)])
    bias_col = jnp.zeros((1, LANES), F32).at[0, :N_GATE_COLS].set(bias)
    bias_row = jnp.broadcast_to(bias[:, None], (N_GATE_COLS, CHUNK))
    cw = conv_w.astype(F32)
    norm_wt = jnp.broadcast_to(norm_w.astype(F32)[:, :, None], (N_HEADS, HEAD_DIM, CHUNK))

    def seg(s, rev):
        if rev:
            return pl.BlockSpec((CHUNK, SEG_W), lambda b, c: (b * n + n - 1 - c, s))
        return pl.BlockSpec((CHUNK, SEG_W), lambda b, c: (b * n + c, s))

    def chunk_of(c, rev):
        return n - 1 - c if rev else c

    def prev_spec(s, rev):
        return pl.BlockSpec((HALO, SEG_W),
                            lambda b, c: (b * n * hb + jnp.maximum(chunk_of(c, rev) * hb - 1, 0), s))

    def next_spec(s, rev):
        return pl.BlockSpec((HALO, SEG_W),
                            lambda b, c: (b * n * hb + jnp.minimum((chunk_of(c, rev) + 1) * hb, n * hb - 1), s))

    def gcol_spec(rev):
        return pl.BlockSpec((CHUNK, LANES), lambda b, c: (b * n + chunk_of(c, rev), 0))

    def grow_spec(rev):
        return pl.BlockSpec((N_GATE_COLS, CHUNK), lambda b, c: (0, b * n + chunk_of(c, rev)))

    full = lambda shape: pl.BlockSpec(shape, lambda b, c: (0,) * len(shape))
    st_shape = (1, N_HEADS, HEAD_DIM, HEAD_DIM)
    nm_shape = (1, 2 * N_HEADS, HEAD_DIM)
    scratch = [pltpu.VMEM((N_HEADS, HEAD_DIM, HEAD_DIM), F32), pltpu.VMEM((N_HEADS, HEAD_DIM), F32),
               pltpu.VMEM((N_HEADS, HEAD_DIM), F32)]
    consts = [full((ML_CONV_W, 2 * MIX_W)), full((1, LANES)), full((N_GATE_COLS, CHUNK))]

    cb, nm = pl.pallas_call(
        functools.partial(_ml_state_kernel, n=n),
        grid=(B, n),
        in_specs=[seg(SEG_MK, True), prev_spec(SEG_MK, True), next_spec(SEG_MK, True), seg(SEG_MV, True),
                  gcol_spec(True), grow_spec(True)] + consts,
        out_specs=[pl.BlockSpec(st_shape, lambda b, c: (b * n + n - 1 - c, 0, 0, 0)),
                   pl.BlockSpec(nm_shape, lambda b, c: (b * n + n - 1 - c, 0, 0))],
        out_shape=[jax.ShapeDtypeStruct((B * n, N_HEADS, HEAD_DIM, HEAD_DIM), BF16),
                   jax.ShapeDtypeStruct((B * n, 2 * N_HEADS, HEAD_DIM), F32)],
        scratch_shapes=scratch,
        compiler_params=_cparams(("arbitrary", "arbitrary")),
        name="mlstm_state",
    )(proj, proj, proj, proj, gates, gates_t, cw, bias_col, bias_row)

    return pl.pallas_call(
        functools.partial(_ml_out_kernel, n=n),
        grid=(B, n),
        in_specs=[seg(SEG_MQ, False), prev_spec(SEG_MQ, False), next_spec(SEG_MQ, False),
                  seg(SEG_MK, False), prev_spec(SEG_MK, False), next_spec(SEG_MK, False),
                  seg(SEG_MV, False), seg(SEG_MO, False), gcol_spec(False), grow_spec(False),
                  pl.BlockSpec(st_shape, lambda b, c: (b * n + c, 0, 0, 0)),
                  pl.BlockSpec(nm_shape, lambda b, c: (b * n + c, 0, 0))]
        + consts + [full((N_HEADS, HEAD_DIM, CHUNK))],
        out_specs=pl.BlockSpec((CHUNK, MIX_W), lambda b, c: (b * n + c, 0)),
        out_shape=jax.ShapeDtypeStruct((T, MIX_W), BF16),
        scratch_shapes=scratch,
        compiler_params=_cparams(("arbitrary", "arbitrary")),
        name="mlstm_out",
    )(proj, proj, proj, proj, proj, proj, proj, proj, gates, gates_t, cb, nm,
      cw, bias_col, bias_row, norm_wt)


def _branch_kernel(oret_ref, ona_ref, oml_ref, ga_ref, gb_ref, gc_ref, wb_ref, bgb_ref, m_ref):
    o_refs = (oret_ref, ona_ref, oml_ref)
    g_refs = (ga_ref, gb_ref, gc_ref)
    for n in range(D_MODEL // SEG_W):
        cs = slice(n * SEG_W, (n + 1) * SEG_W)
        acc = None
        for i in range(N_BRANCH):
            z = g_refs[i][:, cs].astype(F32) + bgb_ref[i:i + 1, cs]
            term = _sigmoid(z) * _dot(o_refs[i][...], wb_ref[i, :, cs])
            acc = term if acc is None else acc + term
        m_ref[:, cs] = acc.astype(BF16)


def _outproj_router_kernel(x_ref, m_ref, wout_ref, n2w_ref, rwh_ref, rwl_ref, rb_ref,
                           h_ref, hn_ref, route_ref, cnt_ref, cnt_scr):
    h = x_ref[...] + _dot(m_ref[...], wout_ref[...])
    h_ref[...] = h
    hn = h * lax.rsqrt(jnp.mean(h * h, -1, keepdims=True) + EPS) * n2w_ref[...]
    hn_ref[...] = hn

    hn_hi = hn.astype(BF16)
    hn_lo = (hn - hn_hi.astype(F32)).astype(BF16)
    logits = (_dot(hn_hi, rwh_ref[...]) + _dot(hn_lo, rwh_ref[...]) + _dot(hn_hi, rwl_ref[...])) + rb_ref[...]
    lane_i = lax.broadcasted_iota(jnp.int32, logits.shape, 1)
    lane = lane_i.astype(F32)
    lane_grp = jnp.right_shift(lane_i - N_GROUPS, 3).astype(F32)
    gmask = lane_i < N_GROUPS
    lg = jnp.where(gmask, logits, NEG)
    mg = jnp.max(lg, -1, keepdims=True)
    p_top = 1.0 / jnp.sum(jnp.where(gmask, jnp.exp(lg - mg), 0.0), -1, keepdims=True)
    grp = jnp.min(jnp.where(lg == mg, lane, float(LANES)), -1, keepdims=True)
    emask = (lane_i >= N_GROUPS) & (lane_i < N_GROUPS + N_EXPERTS) & (lane_grp == grp)
    le = jnp.where(emask, logits, NEG)
    m1 = jnp.max(le, -1, keepdims=True)
    i1 = jnp.min(jnp.where(le == m1, lane, float(LANES)), -1, keepdims=True)
    le2 = jnp.where(lane == i1, NEG, le)
    m2 = jnp.max(le2, -1, keepdims=True)
    i2 = jnp.min(jnp.where(le2 == m2, lane, float(LANES)), -1, keepdims=True)
    e2 = jnp.exp(m2 - m1)
    gate1 = p_top / (1.0 + e2)
    gate2 = p_top * e2 / (1.0 + e2)

    @pl.when(pl.program_id(0) == 0)
    def _():
        cnt_scr[...] = jnp.zeros_like(cnt_scr)

    tm = logits.shape[0]
    hit1 = lane == i1
    hit2 = lane == i2
    onehot = jnp.where(hit1 | hit2, 1.0, 0.0)
    r_i = lax.broadcasted_iota(jnp.int32, (tm, tm), 0)
    c_i = lax.broadcasted_iota(jnp.int32, (tm, tm), 1)
    before = jnp.where(c_i < r_i, 1.0, 0.0).astype(BF16)
    prior = _dot(before, onehot.astype(BF16)) + cnt_scr[...]
    rank1 = jnp.sum(jnp.where(hit1, prior, 0.0), -1, keepdims=True)
    rank2 = jnp.sum(jnp.where(hit2, prior, 0.0), -1, keepdims=True)
    cnt = cnt_scr[...] + jnp.sum(onehot, 0, keepdims=True)
    cnt_scr[...] = cnt
    cnt_ref[...] = jnp.broadcast_to(cnt, cnt_ref.shape)

    cols = (i1 - N_GROUPS, i2 - N_GROUPS, gate1, gate2, rank1, rank2)
    route = jnp.zeros_like(logits)
    for ci, val in enumerate(cols):
        route = jnp.where(lane_i == ci, val, route)
    route_ref[...] = route


MERGE_ROWS = 512


def _merge(x2d, o_ret, o_na, o_ml, proj, wb, bgb, wout, n2w, rw_hi, rw_lo, rb):
    T = x2d.shape[0]
    tm = MERGE_ROWS
    full = lambda shape: pl.BlockSpec(shape, lambda i: (0,) * len(shape), pipeline_mode=pl.Buffered(1))
    row = lambda w: pl.BlockSpec((tm, w), lambda i: (i, 0))
    gate = lambda s: pl.BlockSpec((tm, D_MODEL), lambda i: (i, s // 2))
    merged = pl.pallas_call(
        _branch_kernel,
        grid=(T // tm,),
        in_specs=[row(MIX_W), row(MIX_W), row(MIX_W), gate(SEG_GA), gate(SEG_GB), gate(SEG_GC),
                  full((N_BRANCH, MIX_W, D_MODEL)), full((N_BRANCH, D_MODEL))],
        out_specs=row(D_MODEL),
        out_shape=jax.ShapeDtypeStruct((T, D_MODEL), BF16),
        compiler_params=_cparams(("arbitrary",)),
        name="branch_merge",
    )(o_ret, o_na, o_ml, proj, proj, proj, wb, bgb)
    return pl.pallas_call(
        _outproj_router_kernel,
        grid=(T // tm,),
        in_specs=[row(D_MODEL), row(D_MODEL), full((D_MODEL, D_MODEL)), full((1, D_MODEL)),
                  full((D_MODEL, LANES)), full((D_MODEL, LANES)), full((1, LANES))],
        out_specs=[row(D_MODEL), row(D_MODEL), row(LANES), pl.BlockSpec((8, LANES), lambda i: (0, 0))],
        out_shape=[jax.ShapeDtypeStruct((T, D_MODEL), F32), jax.ShapeDtypeStruct((T, D_MODEL), F32),
                   jax.ShapeDtypeStruct((T, LANES), F32), jax.ShapeDtypeStruct((8, LANES), F32)],
        scratch_shapes=[pltpu.VMEM((1, LANES), F32)],
        compiler_params=_cparams(("arbitrary",)),
        name="outproj_router",
    )(x2d, merged, wout, n2w, rw_hi, rw_lo, rb)


def _row_copy(src_hbm, row, dst, r, sem):
    return pltpu.make_async_copy(src_hbm.at[pl.ds(row, 1), :], dst.at[pl.ds(r, 1), :], sem)


def _ffn_kernel(blk_e_ref, n_used_ref, src_ref, hn_hbm, wg_ref, wu_ref, wd_ref, y_ref, xbuf, sem):
    i = pl.program_id(0)
    n_used = n_used_ref[0]
    slot = i % 2

    def wait_block(s):
        pltpu.make_async_copy(hn_hbm.at[pl.ds(0, MOE_ROWS), :], xbuf.at[s], sem.at[s]).wait()

    @pl.when(i == 0)
    def _():
        def body(r, carry):
            _row_copy(hn_hbm, src_ref[r], xbuf.at[0], r, sem.at[0]).start()
            return carry
        lax.fori_loop(0, MOE_ROWS, body, 0, unroll=8)

    @pl.when(i < n_used)
    def _():
        for r in range(MOE_ROWS):
            _row_copy(hn_hbm, src_ref[(i + 1) * MOE_ROWS + r], xbuf.at[1 - slot], r, sem.at[1 - slot]).start()
        wait_block(slot)
        x = xbuf[slot].astype(BF16)
        a = _dot(x, wg_ref[0])
        hid = (a * _sigmoid(a)) * _dot(x, wu_ref[0])
        y_ref[...] = _dot(hid.astype(BF16), wd_ref[0])

    @pl.when(i >= n_used)
    def _():
        y_ref[...] = jnp.zeros_like(y_ref)

    @pl.when(i == n_used)
    def _():
        wait_block(slot)


def _expert_ffn(hn, src, blk_e, n_used, wg, wu, wd):
    n_blocks = src.shape[0] // MOE_ROWS
    grid_spec = pltpu.PrefetchScalarGridSpec(
        num_scalar_prefetch=3,
        grid=(n_blocks,),
        in_specs=[
            pl.BlockSpec(memory_space=pl.ANY),
            pl.BlockSpec((1, D_MODEL, D_EXPERT), lambda i, be, nu, sr: (be[i], 0, 0)),
            pl.BlockSpec((1, D_MODEL, D_EXPERT), lambda i, be, nu, sr: (be[i], 0, 0)),
            pl.BlockSpec((1, D_EXPERT, D_MODEL), lambda i, be, nu, sr: (be[i], 0, 0)),
        ],
        out_specs=pl.BlockSpec((MOE_ROWS, D_MODEL), lambda i, be, nu, sr: (i, 0)),
        scratch_shapes=[pltpu.VMEM((2, MOE_ROWS, D_MODEL), F32), pltpu.SemaphoreType.DMA((2,))],
    )
    return pl.pallas_call(
        _ffn_kernel,
        grid_spec=grid_spec,
        out_shape=jax.ShapeDtypeStruct((n_blocks * MOE_ROWS, D_MODEL), F32),
        compiler_params=_cparams(("arbitrary",), disable_bounds_checks=True),
        name="expert_ffn",
    )(blk_e, n_used, src, hn, wg, wu, wd)


COMBINE_ROWS = 256


def _combine_kernel(dest_ref, h_ref, route_ref, yb_hbm, out_ref, ybuf, sem):
    i = pl.program_id(0)
    slot = i % 2

    def start_row(tile, r, to_slot):
        a = (tile * COMBINE_ROWS + r) * TOP_K_INNER
        for k in range(TOP_K_INNER):
            _row_copy(yb_hbm, dest_ref[a + k], ybuf.at[to_slot, k], r, sem.at[to_slot]).start()

    @pl.when(i == 0)
    def _():
        def body(r, carry):
            start_row(0, r, 0)
            return carry
        lax.fori_loop(0, COMBINE_ROWS, body, 0, unroll=8)

    @pl.when(i + 1 < pl.num_programs(0))
    def _():
        for r in range(COMBINE_ROWS):
            start_row(i + 1, r, 1 - slot)

    for k in range(TOP_K_INNER):
        pltpu.make_async_copy(yb_hbm.at[pl.ds(0, COMBINE_ROWS), :], ybuf.at[slot, k], sem.at[slot]).wait()

    route = route_ref[...]
    out_ref[...] = h_ref[...] + route[:, 2:3] * ybuf[slot, 0] + route[:, 3:4] * ybuf[slot, 1]


def _combine(h, route, yb, dest):
    T = h.shape[0]
    grid_spec = pltpu.PrefetchScalarGridSpec(
        num_scalar_prefetch=1,
        grid=(T // COMBINE_ROWS,),
        in_specs=[
            pl.BlockSpec((COMBINE_ROWS, D_MODEL), lambda i, d: (i, 0)),
            pl.BlockSpec((COMBINE_ROWS, LANES), lambda i, d: (i, 0)),
            pl.BlockSpec(memory_space=pl.ANY),
        ],
        out_specs=pl.BlockSpec((COMBINE_ROWS, D_MODEL), lambda i, d: (i, 0)),
        scratch_shapes=[pltpu.VMEM((2, TOP_K_INNER, COMBINE_ROWS, D_MODEL), F32), pltpu.SemaphoreType.DMA((2,))],
    )
    return pl.pallas_call(
        _combine_kernel,
        grid_spec=grid_spec,
        out_shape=jax.ShapeDtypeStruct((T, D_MODEL), F32),
        compiler_params=_cparams(("arbitrary",), disable_bounds_checks=True),
        name="moe_combine",
    )(dest, h, route, yb)


def _moe(h, hn, route, counts, wg, wu, wd, layer):
    T = h.shape[0]
    n_assign = T * TOP_K_INNER
    expert = route[:, 0:2].astype(jnp.int32)
    rank = route[:, 4:6].astype(jnp.int32)
    cnt = counts[0, N_GROUPS:N_GROUPS + N_EXPERTS].astype(jnp.int32)
    padded = (cnt + MOE_ROWS - 1) // MOE_ROWS * MOE_ROWS
    pend = jnp.cumsum(padded)
    pstart = pend - padded
    dest = (pstart[expert] + rank).reshape(-1)
    n_blocks = (n_assign + N_EXPERTS * (MOE_ROWS - 1) + MOE_ROWS - 1) // MOE_ROWS + 1
    blk_start = jnp.arange(n_blocks, dtype=jnp.int32) * MOE_ROWS
    blk_e = jnp.minimum(jnp.sum((pend[None, :] <= blk_start[:, None]).astype(jnp.int32), 1), N_EXPERTS - 1)
    blk_e = blk_e + layer * N_EXPERTS
    n_used = (pend[-1] // MOE_ROWS).astype(jnp.int32).reshape(1)
    flat_t = jnp.repeat(jnp.arange(T, dtype=jnp.int32), TOP_K_INNER)
    src = jnp.zeros((n_blocks * MOE_ROWS,), jnp.int32).at[dest].set(flat_t)
    yb = _expert_ffn(hn, src, blk_e, n_used, wg, wu, wd)
    return _combine(h, route, yb, dest)


def _rope_tables(L):
    freqs = ROPE_BASE ** (-jnp.arange(0, HEAD_DIM, 2, dtype=F32) / HEAD_DIM)
    ang = jnp.arange(L, dtype=F32)[:, None] * freqs[None]
    cos, sin = jnp.cos(ang), jnp.sin(ang)
    return jnp.concatenate([cos, cos], -1), jnp.concatenate([-sin, sin], -1)


def _cast_kernel(w_ref, o_ref):
    o_ref[...] = w_ref[...].astype(BF16)


def _cast_experts(w):
    depth, e, r, c = w.shape
    spec = pl.BlockSpec((1, r, c), lambda i: (i, 0, 0))
    return pl.pallas_call(
        _cast_kernel,
        grid=(depth * e,),
        in_specs=[spec],
        out_specs=spec,
        out_shape=jax.ShapeDtypeStruct((depth * e, r, c), BF16),
        compiler_params=_cparams(("arbitrary",)),
        name="cast_experts",
    )(w.reshape(depth * e, r, c))


def _prep_layer(p):
    w_in = p["w_in"]
    sp = np.cumsum([MIX_W] * 11 + [2 * N_HEADS, 2 * N_HEADS] + [D_MODEL] * N_BRANCH)[:-1].tolist()
    (rq, rk, rv, rg, nq, nk, nv, mq, mk, mv, mo, mi, mf, ga, gb, gc) = jnp.split(w_in, sp, axis=-1)
    w_main = jnp.concatenate([ga, gb, gc, rq, rk, rv, rg, nq, nk, nv, mq, mk, mv, mo], -1).astype(BF16)
    wif = jnp.concatenate([mi, mf], -1)
    wif_pad = jnp.zeros((D_MODEL, LANES), BF16).at[:, :N_GATE_COLS].set(wif.astype(BF16))
    rw = jnp.zeros((D_MODEL, LANES), F32).at[:, :N_GROUPS].set(p["router_g_w"]) \
        .at[:, N_GROUPS:N_GROUPS + N_EXPERTS].set(p["router_e_w"])
    rb = jnp.zeros((1, LANES), F32).at[0, :N_GROUPS].set(p["router_g_b"]) \
        .at[0, N_GROUPS:N_GROUPS + N_EXPERTS].set(p["router_e_b"])
    return dict(
        nw=p["norm1_w"].astype(F32).reshape(1, D_MODEL), w_main=w_main, wif=wif_pad, wift=wif.T.astype(BF16),
        qn=p["na_q_norm"].astype(F32).reshape(1, HEAD_DIM), kn=p["na_k_norm"].astype(F32).reshape(1, HEAD_DIM),
        wb=p["w_branch"].astype(BF16), bgb=p["branch_gate_b"].astype(F32), wout=p["w_out"].astype(BF16),
        n2w=p["norm2_w"].astype(F32).reshape(1, D_MODEL), rb=rb,
        rw_hi=rw.astype(BF16), rw_lo=(rw - rw.astype(BF16).astype(F32)).astype(BF16),
    )


def _layer(x2d, B, L, p, q, rope, experts, layer):
    proj, gates, gates_t = _inproj(x2d, L, q["nw"], q["w_main"], q["wif"], q["wift"], rope[0], rope[1],
                                   q["qn"], q["kn"])
    o_ret = _retention(proj, B, L, p["ret_decay"], p["ret_norm_w"])
    o_na = _neighborhood(proj, B, L, p["na_rpb"])
    o_ml = _mlstm(proj, gates, gates_t, B, L, p["ml_conv"], p["ml_igate_b"], p["ml_fgate_b"], p["ml_norm_w"])
    h, hn, route, counts = _merge(x2d, o_ret, o_na, o_ml, proj, q["wb"], q["bgb"], q["wout"], q["n2w"],
                                  q["rw_hi"], q["rw_lo"], q["rb"])
    return _moe(h, hn, route, counts, experts[0], experts[1], experts[2], layer)


_PARAM_NAMES = ("norm1_w", "w_in", "ret_decay", "ret_norm_w", "na_q_norm", "na_k_norm", "na_rpb", "ml_conv",
                "ml_igate_b", "ml_fgate_b", "ml_norm_w", "w_branch", "branch_gate_b", "w_out", "norm2_w",
                "router_g_w", "router_g_b", "router_e_w", "router_e_b", "exp_w_gate", "exp_w_up", "exp_w_down")


def _run(x, layers, preps, experts):
    B, L, D = x.shape
    rope = _rope_tables(L)
    x2d = x.reshape(B * L, D)
    for layer, (p, q) in enumerate(zip(layers, preps)):
        x2d = _layer(x2d, B, L, p, q, rope, experts, layer)
    return x2d.reshape(B, L, D)


def kernel(x_prompt, x_sample, norm1_w, w_in, ret_decay, ret_norm_w, na_q_norm, na_k_norm, na_rpb, ml_conv,
           ml_igate_b, ml_fgate_b, ml_norm_w, w_branch, branch_gate_b, w_out, norm2_w, router_g_w, router_g_b,
           router_e_w, router_e_b, exp_w_gate, exp_w_up, exp_w_down):
    stacked = (norm1_w, w_in, ret_decay, ret_norm_w, na_q_norm, na_k_norm, na_rpb, ml_conv, ml_igate_b,
               ml_fgate_b, ml_norm_w, w_branch, branch_gate_b, w_out, norm2_w, router_g_w, router_g_b,
               router_e_w, router_e_b, exp_w_gate, exp_w_up, exp_w_down)
    depth = w_in.shape[0]
    layers = [dict(zip(_PARAM_NAMES, (a[l] for a in stacked))) for l in range(depth)]
    preps = [_prep_layer(p) for p in layers]
    experts = tuple(_cast_experts(w) for w in (exp_w_gate, exp_w_up, exp_w_down))
    return (_run(x_prompt, layers, preps, experts), _run(x_sample, layers, preps, experts))
```

```python
import functools

import numpy as np
import jax
import jax.numpy as jnp
from jax import lax
from jax.experimental import pallas as pl
from jax.experimental.pallas import tpu as pltpu

D_MODEL = 2048
HEAD_DIM = 128
MIX_W = D_MODEL // 2
N_HEADS = MIX_W // HEAD_DIM
N_BRANCH = 3
CHUNK = 128
ROPE_BASE = 10000.0
GRID_W = 64
WIN_ROWS = 8
WIN_COLS = 16
ML_CONV_W = 3
N_GROUPS = 4
EXPERTS_PER_GROUP = 8
N_EXPERTS = N_GROUPS * EXPERTS_PER_GROUP
TOP_K_INNER = 2
D_EXPERT = D_MODEL // 2
EPS = 1e-6
NEG = -1e30

F32 = jnp.float32
BF16 = jnp.bfloat16

LANES = 128
SEG_W = MIX_W
SEG_GA, SEG_GB, SEG_GC = 0, 2, 4
SEG_RQ, SEG_RK, SEG_RV, SEG_RG = 6, 7, 8, 9
SEG_NQ, SEG_NK, SEG_NV = 10, 11, 12
SEG_MQ, SEG_MK, SEG_MV, SEG_MO = 13, 14, 15, 16
N_SEG = 17
N_GATE_COLS = 4 * N_HEADS

NA_QROWS = 2
NA_KBLKS = 5
NA_KW = NA_KBLKS * CHUNK
MOE_ROWS = 256
VMEM_LIMIT = 56 * 1024 * 1024


def _cparams(sem, **kw):
    return pltpu.CompilerParams(dimension_semantics=sem, vmem_limit_bytes=VMEM_LIMIT, **kw)


def _sigmoid(z):
    return 0.5 * jnp.tanh(0.5 * z) + 0.5


def _hs(h):
    return slice(h * HEAD_DIM, (h + 1) * HEAD_DIM)


def _dot(a, b):
    return jnp.dot(a, b, preferred_element_type=F32)


def _dot_nt(a, b):
    return lax.dot_general(a, b, (((1,), (1,)), ((), ())), preferred_element_type=F32)


def _dot_tn(a, b):
    return lax.dot_general(a, b, (((0,), (0,)), ((), ())), preferred_element_type=F32)


def _inproj_kernel(x_ref, nw_ref, w_ref, wif_ref, wift_ref, cos_ref, sin_ref, qn_ref, kn_ref,
                   proj_ref, g_ref, gt_ref, xn_ref):
    j = pl.program_id(1)

    @pl.when(j == 0)
    def _():
        x = x_ref[...]
        y = x * lax.rsqrt(jnp.mean(x * x, -1, keepdims=True) + EPS) * nw_ref[...]
        xn = y.astype(BF16)
        xn_ref[...] = xn
        g_ref[...] = _dot(xn, wif_ref[...])
        gt_ref[...] = _dot_nt(wift_ref[...], xn)

    acc = _dot(xn_ref[...], w_ref[...])

    is_rope = (j == SEG_RQ) | (j == SEG_RK)
    is_norm = (j == SEG_NQ) | (j == SEG_NK)

    @pl.when(is_rope)
    def _():
        scale = jnp.where(j == SEG_RK, HEAD_DIM ** -0.5, 1.0).astype(F32)
        cos = cos_ref[...]
        sin = sin_ref[...]
        for h in range(N_HEADS):
            xh = acc[:, _hs(h)]
            r = xh * cos + pltpu.roll(xh, HEAD_DIM // 2, 1) * sin
            proj_ref[:, _hs(h)] = (r * scale).astype(BF16)

    @pl.when(is_norm)
    def _():
        w = jnp.where(j == SEG_NQ, qn_ref[...] * (HEAD_DIM ** -0.5), kn_ref[...])
        for h in range(N_HEADS):
            xh = acc[:, _hs(h)]
            y = xh * lax.rsqrt(jnp.mean(xh * xh, -1, keepdims=True) + EPS) * w
            proj_ref[:, _hs(h)] = y.astype(BF16)

    @pl.when(jnp.logical_not(is_rope | is_norm))
    def _():
        proj_ref[...] = acc.astype(BF16)


def _inproj(x2d, L, nw, w_main, layer, wif, wift, cos_t, sin_t, qn, kn):
    T = x2d.shape[0]
    tm = min(1024, L)
    nlt = L // tm
    return pl.pallas_call(
        _inproj_kernel,
        grid=(T // tm, N_SEG),
        in_specs=[
            pl.BlockSpec((tm, D_MODEL), lambda i, j: (i, 0)),
            pl.BlockSpec((1, D_MODEL), lambda i, j: (0, 0)),
            pl.BlockSpec((None, D_MODEL, SEG_W), lambda i, j: (layer, 0, j)),
            pl.BlockSpec((D_MODEL, LANES), lambda i, j: (0, 0)),
            pl.BlockSpec((N_GATE_COLS, D_MODEL), lambda i, j: (0, 0)),
            pl.BlockSpec((tm, HEAD_DIM), lambda i, j: (i % nlt, 0)),
            pl.BlockSpec((tm, HEAD_DIM), lambda i, j: (i % nlt, 0)),
            pl.BlockSpec((1, HEAD_DIM), lambda i, j: (0, 0)),
            pl.BlockSpec((1, HEAD_DIM), lambda i, j: (0, 0)),
        ],
        out_specs=[
            pl.BlockSpec((tm, SEG_W), lambda i, j: (i, j)),
            pl.BlockSpec((tm, LANES), lambda i, j: (i, 0)),
            pl.BlockSpec((N_GATE_COLS, tm), lambda i, j: (0, i)),
        ],
        out_shape=[
            jax.ShapeDtypeStruct((T, N_SEG * SEG_W), BF16),
            jax.ShapeDtypeStruct((T, LANES), F32),
            jax.ShapeDtypeStruct((N_GATE_COLS, T), F32),
        ],
        scratch_shapes=[pltpu.VMEM((tm, D_MODEL), BF16)],
        compiler_params=_cparams(("arbitrary", "arbitrary")),
        name="inproj",
    )(x2d, nw, w_main, wif, wift, cos_t, sin_t, qn, kn)


def _ret_state_kernel(k_ref, v_ref, kb_ref, cdb_ref, sb_ref, s_scr):
    c = pl.program_id(1)

    @pl.when(c == 0)
    def _():
        s_scr[...] = jnp.zeros_like(s_scr)

    sb_ref[0] = s_scr[...].astype(BF16)
    kk = (k_ref[...].astype(F32) * kb_ref[...]).astype(BF16)
    v = v_ref[...]
    for h in range(N_HEADS):
        kv = _dot_tn(kk[:, _hs(h)], v[:, _hs(h)])
        s_scr[h] = s_scr[h] * cdb_ref[h:h + 1, :] + kv


def _ret_out_kernel(q_ref, k_ref, v_ref, g_ref, sb_ref, dmat_ref, qf_ref, qb_ref, kf_ref, cdf_ref, nw_ref,
                    o_ref, s_scr):
    c = pl.program_id(1)

    @pl.when(c == 0)
    def _():
        s_scr[...] = jnp.zeros_like(s_scr)

    q = q_ref[...]
    k = k_ref[...]
    v = v_ref[...]
    qf32 = q.astype(F32)
    q_fwd = (qf32 * qf_ref[...]).astype(BF16)
    q_bwd = (qf32 * qb_ref[...]).astype(BF16)
    k_end = (k.astype(F32) * kf_ref[...]).astype(BF16)
    g = g_ref[...].astype(F32)
    for h in range(N_HEADS):
        hs = _hs(h)
        s = _dot_nt(q[:, hs], k[:, hs]) * dmat_ref[h]
        o = _dot(s.astype(BF16), v[:, hs])
        o = o + _dot(q_fwd[:, hs], s_scr[h].astype(BF16))
        o = o + _dot(q_bwd[:, hs], sb_ref[0, h])
        s_scr[h] = s_scr[h] * cdf_ref[h:h + 1, :] + _dot_tn(k_end[:, hs], v[:, hs])
        y = o * lax.rsqrt(jnp.mean(o * o, -1, keepdims=True) + EPS) * nw_ref[h:h + 1, :]
        gh = g[:, hs]
        o_ref[:, hs] = (y * (gh * _sigmoid(gh))).astype(BF16)


def _retention(proj, B, L, ret_decay, ret_norm_w):
    n = L // CHUNK
    T = B * L
    lg = jax.nn.log_sigmoid(ret_decay.astype(F32))
    idx = jnp.arange(CHUNK, dtype=F32)
    diff = idx[:, None] - idx[None, :]
    dmat = jnp.where(diff >= 0, jnp.exp(jnp.maximum(diff, 0.0) * lg[0][:, None, None]),
                     jnp.exp(jnp.maximum(-diff, 0.0) * lg[1][:, None, None]))

    def lane_tab(e):
        return jnp.repeat(jnp.exp(e).T, HEAD_DIM, axis=1)

    qf_tab = lane_tab((idx + 1.0)[None, :] * lg[0][:, None])
    qb_tab = lane_tab((CHUNK - idx)[None, :] * lg[1][:, None])
    kf_tab = lane_tab((CHUNK - 1.0 - idx)[None, :] * lg[0][:, None])
    kb_tab = lane_tab(idx[None, :] * lg[1][:, None])
    cdf = jnp.broadcast_to(jnp.exp(CHUNK * lg[0])[:, None], (N_HEADS, HEAD_DIM))
    cdb = jnp.broadcast_to(jnp.exp(CHUNK * lg[1])[:, None], (N_HEADS, HEAD_DIM))

    def seg(s, rev=False):
        if rev:
            return pl.BlockSpec((CHUNK, SEG_W), lambda b, c: (b * n + n - 1 - c, s))
        return pl.BlockSpec((CHUNK, SEG_W), lambda b, c: (b * n + c, s))

    full = lambda shape: pl.BlockSpec(shape, lambda b, c: (0,) * len(shape))
    st_shape = (1, N_HEADS, HEAD_DIM, HEAD_DIM)

    state = dict(
        kernel=_ret_state_kernel,
        in_specs=[seg(SEG_RK, True), seg(SEG_RV, True), full((CHUNK, MIX_W)), full((N_HEADS, HEAD_DIM))],
        args=[proj, proj, kb_tab, cdb],
        out_specs=[pl.BlockSpec(st_shape, lambda b, c: (b * n + n - 1 - c, 0, 0, 0))],
        out_shape=[jax.ShapeDtypeStruct((B * n, N_HEADS, HEAD_DIM, HEAD_DIM), BF16)],
        scratch=[pltpu.VMEM((N_HEADS, HEAD_DIM, HEAD_DIM), F32)])

    def out(sb):
        return dict(
            kernel=_ret_out_kernel,
            in_specs=[seg(SEG_RQ), seg(SEG_RK), seg(SEG_RV), seg(SEG_RG),
                      pl.BlockSpec(st_shape, lambda b, c: (b * n + c, 0, 0, 0)),
                      full((N_HEADS, CHUNK, CHUNK)), full((CHUNK, MIX_W)), full((CHUNK, MIX_W)),
                      full((CHUNK, MIX_W)), full((N_HEADS, HEAD_DIM)), full((N_HEADS, HEAD_DIM))],
            args=[proj, proj, proj, proj, sb, dmat, qf_tab, qb_tab, kf_tab, cdf, ret_norm_w.astype(F32)],
            out_specs=[pl.BlockSpec((CHUNK, MIX_W), lambda b, c: (b * n + c, 0))],
            out_shape=[jax.ShapeDtypeStruct((T, MIX_W), BF16)],
            scratch=[pltpu.VMEM((N_HEADS, HEAD_DIM, HEAD_DIM), F32)])

    return state, out


def _na_index_tables(nqb):
    rows = NA_QROWS * nqb
    qi = np.arange(CHUNK)
    ki = np.arange(NA_KW)

    def one(qb):
        kb = NA_QROWS * int(np.clip(qb - 2, 0, nqb - NA_KBLKS))
        r = (NA_QROWS * qb + qi // GRID_W)[:, None]
        qc = (qi % GRID_W)[:, None]
        kr = (kb + ki // GRID_W)[None, :]
        kc = (ki % GRID_W)[None, :]
        rs = np.clip(r - WIN_ROWS // 2, 0, rows - WIN_ROWS)
        cs = np.clip(qc - WIN_COLS // 2, 0, GRID_W - WIN_COLS)
        valid = (kr >= rs) & (kr < rs + WIN_ROWS) & (kc >= cs) & (kc < cs + WIN_COLS)
        dr = np.clip(kr - r + WIN_ROWS - 1, 0, 2 * WIN_ROWS - 2)
        dc = np.clip(kc - qc + WIN_COLS - 1, 0, 2 * WIN_COLS - 2)
        return dr + 0 * dc, dc + 0 * dr, valid

    reps = [0, 1, 2, nqb - 2, nqb - 1]
    tabs = [one(qb) for qb in reps]
    for qb in range(2, nqb - 2):
        t = one(qb)
        assert all(np.array_equal(a, b) for a, b in zip(t, tabs[2]))
    return tuple(np.stack([t[i] for t in tabs]) for i in range(3))


def _na_kernel(q_ref, k0, k1, k2, k3, k4, v0, v1, v2, v3, v4, bias_ref, o_ref):
    k_refs = (k0, k1, k2, k3, k4)
    v_refs = (v0, v1, v2, v3, v4)
    for h in range(N_HEADS):
        hs = _hs(h)
        q = q_ref[:, hs]
        k_all = jnp.concatenate([r[:, hs] for r in k_refs], 0)
        v_all = jnp.concatenate([r[:, hs] for r in v_refs], 0)
        s = _dot_nt(q, k_all) + bias_ref[0, h]
        p = jnp.exp(s - jnp.max(s, -1, keepdims=True))
        o = _dot(p.astype(BF16), v_all)
        o_ref[:, hs] = (o / jnp.sum(p, -1, keepdims=True)).astype(BF16)


def _neighborhood(proj, B, L, rpb):
    nqb = L // CHUNK
    assert nqb >= NA_KBLKS and L % (GRID_W * NA_QROWS) == 0
    T = B * L
    dr, dc, valid = _na_index_tables(nqb)
    n_cls = dr.shape[0]
    kr_n = NA_KW // GRID_W
    dr6 = dr.reshape(n_cls, NA_QROWS, GRID_W, kr_n, GRID_W)
    dc6 = dc.reshape(n_cls, NA_QROWS, GRID_W, kr_n, GRID_W)
    dr_s = dr6[:, :, 0, :, 0]
    dc_s = dc6[0, 0, :, 0, :]
    assert np.array_equal(dr6, np.broadcast_to(dr_s[:, :, None, :, None], dr6.shape))
    assert np.array_equal(dc6, np.broadcast_to(dc_s[None, None, :, None, :], dc6.shape))
    sel_r = (dr_s[..., None] == np.arange(2 * WIN_ROWS - 1)).astype(np.float32)
    sel_c = (np.arange(2 * WIN_COLS - 1)[:, None, None] == dc_s[None]).astype(np.float32)
    hp = lax.Precision.HIGHEST
    t = jnp.einsum('cajr,hrd->cajhd', sel_r, rpb.astype(F32), precision=hp)
    bias = jnp.einsum('cajhd,dqk->chaqjk', t, sel_c, precision=hp).reshape(n_cls, N_HEADS, CHUNK, NA_KW)
    bias = jnp.where(jnp.asarray(valid)[:, None], bias, NEG)

    def kspec(s, j):
        return pl.BlockSpec((CHUNK, SEG_W),
                            lambda b, qb: (b * nqb + jnp.clip(qb - 2, 0, nqb - NA_KBLKS) + j, s))

    def cls(qb):
        return jnp.where(qb < 2, qb, jnp.where(qb >= nqb - 2, qb - (nqb - 5), 2))

    return dict(
        kernel=_na_kernel,
        in_specs=[pl.BlockSpec((CHUNK, SEG_W), lambda b, qb: (b * nqb + qb, SEG_NQ))]
        + [kspec(SEG_NK, j) for j in range(NA_KBLKS)]
        + [kspec(SEG_NV, j) for j in range(NA_KBLKS)]
        + [pl.BlockSpec((1, N_HEADS, CHUNK, NA_KW), lambda b, qb: (cls(qb), 0, 0, 0))],
        args=[proj] * (1 + 2 * NA_KBLKS) + [bias],
        out_specs=[pl.BlockSpec((CHUNK, MIX_W), lambda b, qb: (b * nqb + qb, 0))],
        out_shape=[jax.ShapeDtypeStruct((T, MIX_W), BF16)],
        scratch=[])


HALO = 16


def _log_sigmoid(x):
    return jnp.minimum(x, 0.0) - jnp.log1p(jnp.exp(-jnp.abs(x)))


def _conv_silu(x_ref, prev_ref, next_ref, w_ref, col0, c, n):
    x = x_ref[...].astype(F32)
    row = lax.broadcasted_iota(jnp.int32, x.shape, 0)
    prev_row = prev_ref[HALO - 1:HALO, :].astype(F32) * (c > 0).astype(F32)
    next_row = next_ref[0:1, :].astype(F32) * (c < n - 1).astype(F32)
    x_prev = jnp.where(row == 0, prev_row, pltpu.roll(x, 1, 0))
    x_next = jnp.where(row == CHUNK - 1, next_row, pltpu.roll(x, CHUNK - 1, 0))
    cs = slice(col0, col0 + MIX_W)
    y = x_prev * w_ref[0:1, cs] + x * w_ref[1:2, cs] + x_next * w_ref[2:3, cs]
    return y * _sigmoid(y)


def _tri():
    r = lax.broadcasted_iota(jnp.int32, (CHUNK, CHUNK), 0)
    c = lax.broadcasted_iota(jnp.int32, (CHUNK, CHUNK), 1)
    return (c <= r).astype(F32), (c >= r).astype(F32)


def _hp_dot(a, b):
    return jnp.dot(a, b, preferred_element_type=F32, precision=lax.Precision.HIGHEST)


FWD_LANE = 2 * N_HEADS
BWD_LANE = 3 * N_HEADS


def _gate_dense(gc_ref, gr_ref, bc_ref, br_ref):
    low, up = _tri()
    g_col = gc_ref[...] + bc_ref[...]
    lf_col = _log_sigmoid(g_col)
    lf_row = _log_sigmoid(gr_ref[...] + br_ref[...])
    b_col = _hp_dot(low, lf_col)
    b_row = _hp_dot(lf_row, up)
    tot_row = b_col[CHUNK - 1:CHUNK, :]
    tot_col = b_row[:, CHUNK - 1:CHUNK]
    lane = lax.broadcasted_iota(jnp.int32, (CHUNK, LANES), 1)
    row = lax.broadcasted_iota(jnp.int32, (N_GATE_COLS, CHUNK), 0)
    bb_col = jnp.where(lane >= BWD_LANE, tot_row - b_col + lf_col, b_col)
    bb_row = jnp.where(row >= BWD_LANE, tot_col - b_row + lf_row, b_row)
    ck = pltpu.roll(g_col, FWD_LANE, 1) - bb_col
    a = tot_row + ck
    m_loc = jnp.max(a, 0, keepdims=True)
    ea = jnp.exp(a - m_loc)
    return bb_row, ck, tot_row, m_loc, ea


def _lane_bcast(x, l):
    return jnp.broadcast_to(x[:, l:l + 1], (x.shape[0], HEAD_DIM))


def _ml_state_update(kc_h, ct_loc_fn, ea, tot_row, m_loc, l, c_scr, n_scr, m_scr, h):
    m_p = m_scr[h:h + 1, 0:1]
    g_tot = tot_row[:, l:l + 1]
    m_l = m_loc[:, l:l + 1]
    kw = kc_h * _lane_bcast(ea, l)
    ct_loc = ct_loc_fn(kw.astype(BF16))
    n_loc = jnp.sum(kw, 0, keepdims=True)
    m_new = jnp.maximum(g_tot + m_p, m_l)
    sp = jnp.exp(g_tot + m_p - m_new)
    sl = jnp.exp(m_l - m_new)
    c_scr[h] = sp * c_scr[h] + sl * ct_loc
    n_scr[h:h + 1, :] = sp * n_scr[h:h + 1, :] + sl * n_loc
    m_scr[h:h + 1, :] = jnp.broadcast_to(m_new, (1, HEAD_DIM))


def _ml_init(c, c_scr, n_scr, m_scr):
    @pl.when(c == 0)
    def _():
        c_scr[...] = jnp.zeros_like(c_scr)
        n_scr[...] = jnp.zeros_like(n_scr)
        m_scr[...] = jnp.full_like(m_scr, NEG)


def _ml_state_kernel(k_ref, kp_ref, kn_ref, v_ref, gc_ref, gr_ref, cw_ref, bc_ref, br_ref,
                     cb_ref, nm_ref, c_scr, n_scr, m_scr, *, n):
    step = pl.program_id(1)
    _ml_init(step, c_scr, n_scr, m_scr)
    c = n - 1 - step
    cb_ref[0] = c_scr[...].astype(BF16)
    nm_ref[0, 0:N_HEADS, :] = n_scr[...]
    nm_ref[0, N_HEADS:2 * N_HEADS, :] = m_scr[...]

    kc = _conv_silu(k_ref, kp_ref, kn_ref, cw_ref, MIX_W, c, n) * (HEAD_DIM ** -0.5)
    v = v_ref[...]
    _, _, tot_row, m_loc, ea = _gate_dense(gc_ref, gr_ref, bc_ref, br_ref)
    for h in range(N_HEADS):
        v_h = v[:, _hs(h)]
        _ml_state_update(kc[:, _hs(h)], lambda kw: _dot_tn(v_h, kw), ea, tot_row, m_loc, BWD_LANE + h,
                         c_scr, n_scr, m_scr, h)


def _ml_direction(st, mask, b_q, ck_b, m_p, qn, vt_bf, ct_bf, q_bf):
    dmat = jnp.where(mask, b_q + ck_b, NEG)
    inter = b_q + m_p
    m_row = jnp.maximum(jnp.max(dmat, 0, keepdims=True), inter)
    s = st * jnp.exp(dmat - m_row)
    e_int = jnp.exp(inter - m_row)
    num = _dot(vt_bf, s.astype(BF16)) + e_int * _dot_nt(ct_bf, q_bf)
    den = jnp.sum(s, 0, keepdims=True) + e_int * qn
    return num / jnp.maximum(jnp.abs(den), jnp.exp(-m_row))


def _ml_out_kernel(q_ref, qp_ref, qn_ref, k_ref, kp_ref, kn_ref, v_ref, o_ref_in, gc_ref, gr_ref,
                   cb_ref, nm_ref, cw_ref, bc_ref, br_ref, nwt_ref,
                   out_ref, c_scr, n_scr, m_scr, *, n):
    c = pl.program_id(1)
    _ml_init(c, c_scr, n_scr, m_scr)
    qc = _conv_silu(q_ref, qp_ref, qn_ref, cw_ref, 0, c, n)
    kc = _conv_silu(k_ref, kp_ref, kn_ref, cw_ref, MIX_W, c, n) * (HEAD_DIM ** -0.5)
    q_bf = qc.astype(BF16)
    k_bf = kc.astype(BF16)
    v = v_ref[...]
    og = o_ref_in[...].astype(F32)
    bb_row, ck, tot_row, m_loc, ea = _gate_dense(gc_ref, gr_ref, bc_ref, br_ref)
    key = lax.broadcasted_iota(jnp.int32, (CHUNK, CHUNK), 0)
    qry = lax.broadcasted_iota(jnp.int32, (CHUNK, CHUNK), 1)
    causal = key <= qry
    anti = key >= qry
    pad = jnp.zeros((8 - 2, HEAD_DIM), F32)
    for h in range(N_HEADS):
        hs = _hs(h)
        lf, lb = FWD_LANE + h, BWD_LANE + h
        qh = q_bf[:, hs]
        vt_bf = v[:, hs].astype(F32).T.astype(BF16)
        st = _dot_nt(k_bf[:, hs], qh)
        n_prev = jnp.concatenate([n_scr[h:h + 1, :], nm_ref[0, h:h + 1, :], pad], 0).astype(BF16)
        qn = _dot_nt(n_prev, qh)
        h_f = _ml_direction(st, causal, bb_row[lf:lf + 1, :], _lane_bcast(ck, lf), m_scr[h:h + 1, 0:1],
                            qn[0:1, :], vt_bf, c_scr[h].astype(BF16), qh)
        h_b = _ml_direction(st, anti, bb_row[lb:lb + 1, :], _lane_bcast(ck, lb),
                            nm_ref[0, N_HEADS + h:N_HEADS + h + 1, 0:1], qn[1:2, :], vt_bf, cb_ref[0, h], qh)
        _ml_state_update(kc[:, hs], lambda kw: _dot(vt_bf, kw), ea, tot_row, m_loc, lf, c_scr, n_scr, m_scr, h)
        ot = h_f + h_b
        yt = ot * lax.rsqrt(jnp.mean(ot * ot, 0, keepdims=True) + EPS) * nwt_ref[h]
        out_ref[:, hs] = (yt.T * _sigmoid(og[:, hs])).astype(BF16)


def _mlstm(proj, gates, gates_t, B, L, conv_w, ig_b, fg_b, norm_w):
    n = L // CHUNK
    T = B * L
    hb = CHUNK // HALO
    bias = jnp.concatenate([ig_b.astype(F32).reshape(-1), fg_b.astype(F32).reshape(-1)])
    bias_col = jnp.zeros((1, LANES), F32).at[0, :N_GATE_COLS].set(bias)
    bias_row = jnp.broadcast_to(bias[:, None], (N_GATE_COLS, CHUNK))
    cw = conv_w.astype(F32)
    norm_wt = jnp.broadcast_to(norm_w.astype(F32)[:, :, None], (N_HEADS, HEAD_DIM, CHUNK))

    def seg(s, rev):
        if rev:
            return pl.BlockSpec((CHUNK, SEG_W), lambda b, c: (b * n + n - 1 - c, s))
        return pl.BlockSpec((CHUNK, SEG_W), lambda b, c: (b * n + c, s))

    def chunk_of(c, rev):
        return n - 1 - c if rev else c

    def prev_spec(s, rev):
        return pl.BlockSpec((HALO, SEG_W),
                            lambda b, c: (b * n * hb + jnp.maximum(chunk_of(c, rev) * hb - 1, 0), s))

    def next_spec(s, rev):
        return pl.BlockSpec((HALO, SEG_W),
                            lambda b, c: (b * n * hb + jnp.minimum((chunk_of(c, rev) + 1) * hb, n * hb - 1), s))

    def gcol_spec(rev):
        return pl.BlockSpec((CHUNK, LANES), lambda b, c: (b * n + chunk_of(c, rev), 0))

    def grow_spec(rev):
        return pl.BlockSpec((N_GATE_COLS, CHUNK), lambda b, c: (0, b * n + chunk_of(c, rev)))

    full = lambda shape: pl.BlockSpec(shape, lambda b, c: (0,) * len(shape))
    st_shape = (1, N_HEADS, HEAD_DIM, HEAD_DIM)
    nm_shape = (1, 2 * N_HEADS, HEAD_DIM)
    scratch = [pltpu.VMEM((N_HEADS, HEAD_DIM, HEAD_DIM), F32), pltpu.VMEM((N_HEADS, HEAD_DIM), F32),
               pltpu.VMEM((N_HEADS, HEAD_DIM), F32)]
    consts = [full((ML_CONV_W, 2 * MIX_W)), full((1, LANES)), full((N_GATE_COLS, CHUNK))]

    state = dict(
        kernel=functools.partial(_ml_state_kernel, n=n),
        in_specs=[seg(SEG_MK, True), prev_spec(SEG_MK, True), next_spec(SEG_MK, True), seg(SEG_MV, True),
                  gcol_spec(True), grow_spec(True)] + consts,
        args=[proj, proj, proj, proj, gates, gates_t, cw, bias_col, bias_row],
        out_specs=[pl.BlockSpec(st_shape, lambda b, c: (b * n + n - 1 - c, 0, 0, 0)),
                   pl.BlockSpec(nm_shape, lambda b, c: (b * n + n - 1 - c, 0, 0))],
        out_shape=[jax.ShapeDtypeStruct((B * n, N_HEADS, HEAD_DIM, HEAD_DIM), BF16),
                   jax.ShapeDtypeStruct((B * n, 2 * N_HEADS, HEAD_DIM), F32)],
        scratch=scratch)

    def out(cb, nm):
        return dict(
            kernel=functools.partial(_ml_out_kernel, n=n),
            in_specs=[seg(SEG_MQ, False), prev_spec(SEG_MQ, False), next_spec(SEG_MQ, False),
                      seg(SEG_MK, False), prev_spec(SEG_MK, False), next_spec(SEG_MK, False),
                      seg(SEG_MV, False), seg(SEG_MO, False), gcol_spec(False), grow_spec(False),
                      pl.BlockSpec(st_shape, lambda b, c: (b * n + c, 0, 0, 0)),
                      pl.BlockSpec(nm_shape, lambda b, c: (b * n + c, 0, 0))]
            + consts + [full((N_HEADS, HEAD_DIM, CHUNK))],
            args=[proj, proj, proj, proj, proj, proj, proj, proj, gates, gates_t, cb, nm,
                  cw, bias_col, bias_row, norm_wt],
            out_specs=[pl.BlockSpec((CHUNK, MIX_W), lambda b, c: (b * n + c, 0))],
            out_shape=[jax.ShapeDtypeStruct((T, MIX_W), BF16)],
            scratch=scratch)

    return state, out


def _fused_call(parts, grid, name):
    n_in = [len(p["in_specs"]) for p in parts]
    n_out = [len(p["out_specs"]) for p in parts]
    n_scr = [len(p["scratch"]) for p in parts]

    def body(*refs):
        ins, outs, scr = refs[:sum(n_in)], refs[sum(n_in):sum(n_in) + sum(n_out)], refs[sum(n_in) + sum(n_out):]
        i = o = k = 0
        for p, a, b, c in zip(parts, n_in, n_out, n_scr):
            p["kernel"](*ins[i:i + a], *outs[o:o + b], *scr[k:k + c])
            i, o, k = i + a, o + b, k + c

    return pl.pallas_call(
        body,
        grid=grid,
        in_specs=[s for p in parts for s in p["in_specs"]],
        out_specs=[s for p in parts for s in p["out_specs"]],
        out_shape=[s for p in parts for s in p["out_shape"]],
        scratch_shapes=[s for p in parts for s in p["scratch"]],
        compiler_params=_cparams(("arbitrary", "arbitrary")),
        name=name,
    )(*[a for p in parts for a in p["args"]])


def _mixers(proj, gates, gates_t, B, L, p):
    grid = (B, L // CHUNK)
    ret_state, ret_out = _retention(proj, B, L, p["ret_decay"], p["ret_norm_w"])
    ml_state, ml_out = _mlstm(proj, gates, gates_t, B, L, p["ml_conv"], p["ml_igate_b"], p["ml_fgate_b"],
                              p["ml_norm_w"])
    sb, cb, nm = _fused_call([ret_state, ml_state], grid, "mixer_states")
    na = _neighborhood(proj, B, L, p["na_rpb"])
    return _fused_call([ret_out(sb), na, ml_out(cb, nm)], grid, "mixer_outputs")


def _branch_kernel(oret_ref, ona_ref, oml_ref, ga_ref, gb_ref, gc_ref, wb_ref, bgb_ref, m_ref):
    o_refs = (oret_ref, ona_ref, oml_ref)
    g_refs = (ga_ref, gb_ref, gc_ref)
    for n in range(D_MODEL // SEG_W):
        cs = slice(n * SEG_W, (n + 1) * SEG_W)
        acc = None
        for i in range(N_BRANCH):
            z = g_refs[i][:, cs].astype(F32) + bgb_ref[i:i + 1, cs]
            term = _sigmoid(z) * _dot(o_refs[i][...], wb_ref[i, :, cs])
            acc = term if acc is None else acc + term
        m_ref[:, cs] = acc.astype(BF16)


def _outproj_router_kernel(x_ref, m_ref, wout_ref, n2w_ref, rwh_ref, rwl_ref, rb_ref,
                           h_ref, hn_ref, route_ref, cnt_ref, cnt_scr):
    h = x_ref[...] + _dot(m_ref[...], wout_ref[...])
    h_ref[...] = h
    hn = h * lax.rsqrt(jnp.mean(h * h, -1, keepdims=True) + EPS) * n2w_ref[...]
    hn_ref[...] = hn

    hn_hi = hn.astype(BF16)
    hn_lo = (hn - hn_hi.astype(F32)).astype(BF16)
    logits = (_dot(hn_hi, rwh_ref[...]) + _dot(hn_lo, rwh_ref[...]) + _dot(hn_hi, rwl_ref[...])) + rb_ref[...]
    lane_i = lax.broadcasted_iota(jnp.int32, logits.shape, 1)
    lane = lane_i.astype(F32)
    lane_grp = jnp.right_shift(lane_i - N_GROUPS, 3).astype(F32)
    gmask = lane_i < N_GROUPS
    lg = jnp.where(gmask, logits, NEG)
    mg = jnp.max(lg, -1, keepdims=True)
    p_top = 1.0 / jnp.sum(jnp.where(gmask, jnp.exp(lg - mg), 0.0), -1, keepdims=True)
    grp = jnp.min(jnp.where(lg == mg, lane, float(LANES)), -1, keepdims=True)
    emask = (lane_i >= N_GROUPS) & (lane_i < N_GROUPS + N_EXPERTS) & (lane_grp == grp)
    le = jnp.where(emask, logits, NEG)
    m1 = jnp.max(le, -1, keepdims=True)
    i1 = jnp.min(jnp.where(le == m1, lane, float(LANES)), -1, keepdims=True)
    le2 = jnp.where(lane == i1, NEG, le)
    m2 = jnp.max(le2, -1, keepdims=True)
    i2 = jnp.min(jnp.where(le2 == m2, lane, float(LANES)), -1, keepdims=True)
    e2 = jnp.exp(m2 - m1)
    gate1 = p_top / (1.0 + e2)
    gate2 = p_top * e2 / (1.0 + e2)

    @pl.when(pl.program_id(0) == 0)
    def _():
        cnt_scr[...] = jnp.zeros_like(cnt_scr)

    tm = logits.shape[0]
    hit1 = lane == i1
    hit2 = lane == i2
    onehot = jnp.where(hit1 | hit2, 1.0, 0.0)
    r_i = lax.broadcasted_iota(jnp.int32, (tm, tm), 0)
    c_i = lax.broadcasted_iota(jnp.int32, (tm, tm), 1)
    before = jnp.where(c_i < r_i, 1.0, 0.0).astype(BF16)
    prior = _dot(before, onehot.astype(BF16)) + cnt_scr[...]
    rank1 = jnp.sum(jnp.where(hit1, prior, 0.0), -1, keepdims=True)
    rank2 = jnp.sum(jnp.where(hit2, prior, 0.0), -1, keepdims=True)
    cnt = cnt_scr[...] + jnp.sum(onehot, 0, keepdims=True)
    cnt_scr[...] = cnt
    cnt_ref[...] = jnp.broadcast_to(cnt, cnt_ref.shape)

    cols = (i1 - N_GROUPS, i2 - N_GROUPS, gate1, gate2, rank1, rank2)
    route = jnp.zeros_like(logits)
    for ci, val in enumerate(cols):
        route = jnp.where(lane_i == ci, val, route)
    route_ref[...] = route


MERGE_ROWS = 512


def _merge(x2d, o_ret, o_na, o_ml, proj, wb, bgb, wout, n2w, rw_hi, rw_lo, rb):
    T = x2d.shape[0]
    tm = MERGE_ROWS
    full = lambda shape: pl.BlockSpec(shape, lambda i: (0,) * len(shape), pipeline_mode=pl.Buffered(1))
    row = lambda w: pl.BlockSpec((tm, w), lambda i: (i, 0))
    gate = lambda s: pl.BlockSpec((tm, D_MODEL), lambda i: (i, s // 2))
    merged = pl.pallas_call(
        _branch_kernel,
        grid=(T // tm,),
        in_specs=[row(MIX_W), row(MIX_W), row(MIX_W), gate(SEG_GA), gate(SEG_GB), gate(SEG_GC),
                  full((N_BRANCH, MIX_W, D_MODEL)), full((N_BRANCH, D_MODEL))],
        out_specs=row(D_MODEL),
        out_shape=jax.ShapeDtypeStruct((T, D_MODEL), BF16),
        compiler_params=_cparams(("arbitrary",)),
        name="branch_merge",
    )(o_ret, o_na, o_ml, proj, proj, proj, wb, bgb)
    return pl.pallas_call(
        _outproj_router_kernel,
        grid=(T // tm,),
        in_specs=[row(D_MODEL), row(D_MODEL), full((D_MODEL, D_MODEL)), full((1, D_MODEL)),
                  full((D_MODEL, LANES)), full((D_MODEL, LANES)), full((1, LANES))],
        out_specs=[row(D_MODEL), row(D_MODEL), row(LANES), pl.BlockSpec((8, LANES), lambda i: (0, 0))],
        out_shape=[jax.ShapeDtypeStruct((T, D_MODEL), F32), jax.ShapeDtypeStruct((T, D_MODEL), F32),
                   jax.ShapeDtypeStruct((T, LANES), F32), jax.ShapeDtypeStruct((8, LANES), F32)],
        scratch_shapes=[pltpu.VMEM((1, LANES), F32)],
        compiler_params=_cparams(("arbitrary",)),
        name="outproj_router",
    )(x2d, merged, wout, n2w, rw_hi, rw_lo, rb)


def _row_copy(src_hbm, row, dst, r, sem):
    return pltpu.make_async_copy(src_hbm.at[pl.ds(row, 1), :], dst.at[pl.ds(r, 1), :], sem)


def _ffn_kernel(blk_e_ref, n_used_ref, src_ref, hn_hbm, wg_ref, wu_ref, wd_ref, y_ref, xbuf, sem):
    i = pl.program_id(0)
    n_used = n_used_ref[0]
    slot = i % 2

    def wait_block(s):
        pltpu.make_async_copy(hn_hbm.at[pl.ds(0, MOE_ROWS), :], xbuf.at[s], sem.at[s]).wait()

    @pl.when(i == 0)
    def _():
        def body(r, carry):
            _row_copy(hn_hbm, src_ref[r], xbuf.at[0], r, sem.at[0]).start()
            return carry
        lax.fori_loop(0, MOE_ROWS, body, 0, unroll=8)

    @pl.when(i < n_used)
    def _():
        for r in range(MOE_ROWS):
            _row_copy(hn_hbm, src_ref[(i + 1) * MOE_ROWS + r], xbuf.at[1 - slot], r, sem.at[1 - slot]).start()
        wait_block(slot)
        x = xbuf[slot].astype(BF16)
        a = _dot(x, wg_ref[0])
        hid = (a * _sigmoid(a)) * _dot(x, wu_ref[0])
        y_ref[...] = _dot(hid.astype(BF16), wd_ref[0])

    @pl.when(i >= n_used)
    def _():
        y_ref[...] = jnp.zeros_like(y_ref)

    @pl.when(i == n_used)
    def _():
        wait_block(slot)


def _expert_ffn(hn, src, blk_e, n_used, wg, wu, wd):
    n_blocks = src.shape[0] // MOE_ROWS
    grid_spec = pltpu.PrefetchScalarGridSpec(
        num_scalar_prefetch=3,
        grid=(n_blocks,),
        in_specs=[
            pl.BlockSpec(memory_space=pl.ANY),
            pl.BlockSpec((1, D_MODEL, D_EXPERT), lambda i, be, nu, sr: (be[i], 0, 0)),
            pl.BlockSpec((1, D_MODEL, D_EXPERT), lambda i, be, nu, sr: (be[i], 0, 0)),
            pl.BlockSpec((1, D_EXPERT, D_MODEL), lambda i, be, nu, sr: (be[i], 0, 0)),
        ],
        out_specs=pl.BlockSpec((MOE_ROWS, D_MODEL), lambda i, be, nu, sr: (i, 0)),
        scratch_shapes=[pltpu.VMEM((2, MOE_ROWS, D_MODEL), F32), pltpu.SemaphoreType.DMA((2,))],
    )
    return pl.pallas_call(
        _ffn_kernel,
        grid_spec=grid_spec,
        out_shape=jax.ShapeDtypeStruct((n_blocks * MOE_ROWS, D_MODEL), F32),
        compiler_params=_cparams(("arbitrary",), disable_bounds_checks=True),
        name="expert_ffn",
    )(blk_e, n_used, src, hn, wg, wu, wd)


COMBINE_ROWS = 256


def _combine_kernel(dest_ref, h_ref, route_ref, yb_hbm, out_ref, ybuf, sem):
    i = pl.program_id(0)
    slot = i % 2

    def start_row(tile, r, to_slot):
        a = (tile * COMBINE_ROWS + r) * TOP_K_INNER
        for k in range(TOP_K_INNER):
            _row_copy(yb_hbm, dest_ref[a + k], ybuf.at[to_slot, k], r, sem.at[to_slot]).start()

    @pl.when(i == 0)
    def _():
        def body(r, carry):
            start_row(0, r, 0)
            return carry
        lax.fori_loop(0, COMBINE_ROWS, body, 0, unroll=8)

    @pl.when(i + 1 < pl.num_programs(0))
    def _():
        for r in range(COMBINE_ROWS):
            start_row(i + 1, r, 1 - slot)

    for k in range(TOP_K_INNER):
        pltpu.make_async_copy(yb_hbm.at[pl.ds(0, COMBINE_ROWS), :], ybuf.at[slot, k], sem.at[slot]).wait()

    route = route_ref[...]
    out_ref[...] = h_ref[...] + route[:, 2:3] * ybuf[slot, 0] + route[:, 3:4] * ybuf[slot, 1]


def _combine(h, route, yb, dest):
    T = h.shape[0]
    grid_spec = pltpu.PrefetchScalarGridSpec(
        num_scalar_prefetch=1,
        grid=(T // COMBINE_ROWS,),
        in_specs=[
            pl.BlockSpec((COMBINE_ROWS, D_MODEL), lambda i, d: (i, 0)),
            pl.BlockSpec((COMBINE_ROWS, LANES), lambda i, d: (i, 0)),
            pl.BlockSpec(memory_space=pl.ANY),
        ],
        out_specs=pl.BlockSpec((COMBINE_ROWS, D_MODEL), lambda i, d: (i, 0)),
        scratch_shapes=[pltpu.VMEM((2, TOP_K_INNER, COMBINE_ROWS, D_MODEL), F32), pltpu.SemaphoreType.DMA((2,))],
    )
    return pl.pallas_call(
        _combine_kernel,
        grid_spec=grid_spec,
        out_shape=jax.ShapeDtypeStruct((T, D_MODEL), F32),
        compiler_params=_cparams(("arbitrary",), disable_bounds_checks=True),
        name="moe_combine",
    )(dest, h, route, yb)


def _moe(h, hn, route, counts, wg, wu, wd, layer):
    T = h.shape[0]
    n_assign = T * TOP_K_INNER
    expert = route[:, 0:2].astype(jnp.int32)
    rank = route[:, 4:6].astype(jnp.int32)
    cnt = counts[0, N_GROUPS:N_GROUPS + N_EXPERTS].astype(jnp.int32)
    padded = (cnt + MOE_ROWS - 1) // MOE_ROWS * MOE_ROWS
    pend = jnp.cumsum(padded)
    pstart = pend - padded
    dest = (pstart[expert] + rank).reshape(-1)
    n_blocks = (n_assign + N_EXPERTS * (MOE_ROWS - 1) + MOE_ROWS - 1) // MOE_ROWS + 1
    blk_start = jnp.arange(n_blocks, dtype=jnp.int32) * MOE_ROWS
    blk_e = jnp.minimum(jnp.sum((pend[None, :] <= blk_start[:, None]).astype(jnp.int32), 1), N_EXPERTS - 1)
    blk_e = blk_e + layer * N_EXPERTS
    n_used = (pend[-1] // MOE_ROWS).astype(jnp.int32).reshape(1)
    flat_t = jnp.repeat(jnp.arange(T, dtype=jnp.int32), TOP_K_INNER)
    src = jnp.zeros((n_blocks * MOE_ROWS,), jnp.int32).at[dest].set(flat_t)
    yb = _expert_ffn(hn, src, blk_e, n_used, wg, wu, wd)
    return _combine(h, route, yb, dest)


def _rope_tables(L):
    freqs = ROPE_BASE ** (-jnp.arange(0, HEAD_DIM, 2, dtype=F32) / HEAD_DIM)
    ang = jnp.arange(L, dtype=F32)[:, None] * freqs[None]
    cos, sin = jnp.cos(ang), jnp.sin(ang)
    return jnp.concatenate([cos, cos], -1), jnp.concatenate([-sin, sin], -1)


N_MIX_SEG = 11
N_BRANCH_SEG = N_BRANCH * D_MODEL // SEG_W


def _reorder_kernel(wm_ref, wg_ref, o_ref):
    j = pl.program_id(1)

    @pl.when(j < N_BRANCH_SEG)
    def _():
        o_ref[...] = wg_ref[...].astype(BF16)

    @pl.when(j >= N_BRANCH_SEG)
    def _():
        o_ref[...] = wm_ref[...].astype(BF16)


def _reorder_w_in(w_in):
    depth = w_in.shape[0]
    w_gate = w_in[:, :, N_MIX_SEG * SEG_W + N_GATE_COLS:]
    blk = (None, D_MODEL, SEG_W)
    return pl.pallas_call(
        _reorder_kernel,
        grid=(depth, N_SEG),
        in_specs=[pl.BlockSpec(blk, lambda l, j: (l, 0, jnp.maximum(j - N_BRANCH_SEG, 0))),
                  pl.BlockSpec(blk, lambda l, j: (l, 0, jnp.minimum(j, N_BRANCH_SEG - 1)))],
        out_specs=pl.BlockSpec(blk, lambda l, j: (l, 0, j)),
        out_shape=jax.ShapeDtypeStruct((depth, D_MODEL, N_SEG * SEG_W), BF16),
        compiler_params=_cparams(("arbitrary", "arbitrary")),
        name="reorder_w_in",
    )(w_in, w_gate)


def _cast_kernel(w_ref, o_ref):
    o_ref[...] = w_ref[...].astype(BF16)


def _cast_experts(w):
    depth, e, r, c = w.shape
    spec = pl.BlockSpec((1, r, c), lambda i: (i, 0, 0))
    return pl.pallas_call(
        _cast_kernel,
        grid=(depth * e,),
        in_specs=[spec],
        out_specs=spec,
        out_shape=jax.ShapeDtypeStruct((depth * e, r, c), BF16),
        compiler_params=_cparams(("arbitrary",)),
        name="cast_experts",
    )(w.reshape(depth * e, r, c))


def _prep_layer(p):
    w_in = p["w_in"]
    wif = w_in[:, N_MIX_SEG * SEG_W:N_MIX_SEG * SEG_W + N_GATE_COLS]
    wif_pad = jnp.zeros((D_MODEL, LANES), BF16).at[:, :N_GATE_COLS].set(wif.astype(BF16))
    rw = jnp.zeros((D_MODEL, LANES), F32).at[:, :N_GROUPS].set(p["router_g_w"]) \
        .at[:, N_GROUPS:N_GROUPS + N_EXPERTS].set(p["router_e_w"])
    rb = jnp.zeros((1, LANES), F32).at[0, :N_GROUPS].set(p["router_g_b"]) \
        .at[0, N_GROUPS:N_GROUPS + N_EXPERTS].set(p["router_e_b"])
    return dict(
        nw=p["norm1_w"].astype(F32).reshape(1, D_MODEL), wif=wif_pad, wift=wif.T.astype(BF16),
        qn=p["na_q_norm"].astype(F32).reshape(1, HEAD_DIM), kn=p["na_k_norm"].astype(F32).reshape(1, HEAD_DIM),
        wb=p["w_branch"].astype(BF16), bgb=p["branch_gate_b"].astype(F32), wout=p["w_out"].astype(BF16),
        n2w=p["norm2_w"].astype(F32).reshape(1, D_MODEL), rb=rb,
        rw_hi=rw.astype(BF16), rw_lo=(rw - rw.astype(BF16).astype(F32)).astype(BF16),
    )


def _layer(x2d, B, L, p, q, rope, w_main, experts, layer):
    proj, gates, gates_t = _inproj(x2d, L, q["nw"], w_main, layer, q["wif"], q["wift"], rope[0], rope[1],
                                   q["qn"], q["kn"])
    o_ret, o_na, o_ml = _mixers(proj, gates, gates_t, B, L, p)
    h, hn, route, counts = _merge(x2d, o_ret, o_na, o_ml, proj, q["wb"], q["bgb"], q["wout"], q["n2w"],
                                  q["rw_hi"], q["rw_lo"], q["rb"])
    return _moe(h, hn, route, counts, experts[0], experts[1], experts[2], layer)


_PARAM_NAMES = ("norm1_w", "w_in", "ret_decay", "ret_norm_w", "na_q_norm", "na_k_norm", "na_rpb", "ml_conv",
                "ml_igate_b", "ml_fgate_b", "ml_norm_w", "w_branch", "branch_gate_b", "w_out", "norm2_w",
                "router_g_w", "router_g_b", "router_e_w", "router_e_b", "exp_w_gate", "exp_w_up", "exp_w_down")


def _run(x, layers, preps, w_main, experts):
    B, L, D = x.shape
    rope = _rope_tables(L)
    x2d = x.reshape(B * L, D)
    for layer, (p, q) in enumerate(zip(layers, preps)):
        x2d = _layer(x2d, B, L, p, q, rope, w_main, experts, layer)
    return x2d.reshape(B, L, D)


def kernel(x_prompt, x_sample, norm1_w, w_in, ret_decay, ret_norm_w, na_q_norm, na_k_norm, na_rpb, ml_conv,
           ml_igate_b, ml_fgate_b, ml_norm_w, w_branch, branch_gate_b, w_out, norm2_w, router_g_w, router_g_b,
           router_e_w, router_e_b, exp_w_gate, exp_w_up, exp_w_down):
    stacked = (norm1_w, w_in, ret_decay, ret_norm_w, na_q_norm, na_k_norm, na_rpb, ml_conv, ml_igate_b,
               ml_fgate_b, ml_norm_w, w_branch, branch_gate_b, w_out, norm2_w, router_g_w, router_g_b,
               router_e_w, router_e_b, exp_w_gate, exp_w_up, exp_w_down)
    depth = w_in.shape[0]
    layers = [dict(zip(_PARAM_NAMES, (a[l] for a in stacked))) for l in range(depth)]
    preps = [_prep_layer(p) for p in layers]
    experts = tuple(_cast_experts(w) for w in (exp_w_gate, exp_w_up, exp_w_down))
    w_main = _reorder_w_in(w_in)
    return (_run(x_prompt, layers, preps, w_main, experts), _run(x_sample, layers, preps, w_main, experts))
```

```python
import functools

import numpy as np
import jax
import jax.numpy as jnp
from jax import lax
from jax.experimental import pallas as pl
from jax.experimental.pallas import tpu as pltpu

D_MODEL = 2048
HEAD_DIM = 128
MIX_W = D_MODEL // 2
N_HEADS = MIX_W // HEAD_DIM
N_BRANCH = 3
CHUNK = 128
ROPE_BASE = 10000.0
GRID_W = 64
WIN_ROWS = 8
WIN_COLS = 16
ML_CONV_W = 3
N_GROUPS = 4
EXPERTS_PER_GROUP = 8
N_EXPERTS = N_GROUPS * EXPERTS_PER_GROUP
TOP_K_INNER = 2
D_EXPERT = D_MODEL // 2
EPS = 1e-6
NEG = -1e30

F32 = jnp.float32
BF16 = jnp.bfloat16

LANES = 128
SEG_W = MIX_W
SEG_GA, SEG_GB, SEG_GC = 0, 2, 4
SEG_RQ, SEG_RK, SEG_RV, SEG_RG = 6, 7, 8, 9
SEG_NQ, SEG_NK, SEG_NV = 10, 11, 12
SEG_MQ, SEG_MK, SEG_MV, SEG_MO = 13, 14, 15, 16
N_SEG = 17
N_GATE_COLS = 4 * N_HEADS

NA_QROWS = 2
NA_KBLKS = 5
NA_KW = NA_KBLKS * CHUNK
MOE_ROWS = 256
VMEM_LIMIT = 56 * 1024 * 1024


def _cparams(sem, **kw):
    return pltpu.CompilerParams(dimension_semantics=sem, vmem_limit_bytes=VMEM_LIMIT, **kw)


def _sigmoid(z):
    return 0.5 * jnp.tanh(0.5 * z) + 0.5


def _hs(h):
    return slice(h * HEAD_DIM, (h + 1) * HEAD_DIM)


def _dot(a, b):
    return jnp.dot(a, b, preferred_element_type=F32)


def _dot_nt(a, b):
    return lax.dot_general(a, b, (((1,), (1,)), ((), ())), preferred_element_type=F32)


def _dot_tn(a, b):
    return lax.dot_general(a, b, (((0,), (0,)), ((), ())), preferred_element_type=F32)


def _inproj_kernel(x_ref, nw_ref, w_ref, wif_ref, wift_ref, cos_ref, sin_ref, qn_ref, kn_ref,
                   proj_ref, g_ref, gt_ref, xn_ref):
    j = pl.program_id(1)

    @pl.when(j == 0)
    def _():
        x = x_ref[...]
        y = x * lax.rsqrt(jnp.mean(x * x, -1, keepdims=True) + EPS) * nw_ref[...]
        xn = y.astype(BF16)
        xn_ref[...] = xn
        g_ref[...] = _dot(xn, wif_ref[...])
        gt_ref[...] = _dot_nt(wift_ref[...], xn)

    acc = _dot(xn_ref[...], w_ref[...])

    is_rope = (j == SEG_RQ) | (j == SEG_RK)
    is_norm = (j == SEG_NQ) | (j == SEG_NK)

    @pl.when(is_rope)
    def _():
        scale = jnp.where(j == SEG_RK, HEAD_DIM ** -0.5, 1.0).astype(F32)
        cos = cos_ref[...]
        sin = sin_ref[...]
        for h in range(N_HEADS):
            xh = acc[:, _hs(h)]
            r = xh * cos + pltpu.roll(xh, HEAD_DIM // 2, 1) * sin
            proj_ref[:, _hs(h)] = (r * scale).astype(BF16)

    @pl.when(is_norm)
    def _():
        w = jnp.where(j == SEG_NQ, qn_ref[...] * (HEAD_DIM ** -0.5), kn_ref[...])
        for h in range(N_HEADS):
            xh = acc[:, _hs(h)]
            y = xh * lax.rsqrt(jnp.mean(xh * xh, -1, keepdims=True) + EPS) * w
            proj_ref[:, _hs(h)] = y.astype(BF16)

    @pl.when(jnp.logical_not(is_rope | is_norm))
    def _():
        proj_ref[...] = acc.astype(BF16)


def _inproj(x2d, L, nw, w_main, layer, wif, wift, cos_t, sin_t, qn, kn):
    T = x2d.shape[0]
    tm = min(1024, L)
    nlt = L // tm
    return pl.pallas_call(
        _inproj_kernel,
        grid=(T // tm, N_SEG),
        in_specs=[
            pl.BlockSpec((tm, D_MODEL), lambda i, j: (i, 0)),
            pl.BlockSpec((1, D_MODEL), lambda i, j: (0, 0)),
            pl.BlockSpec((None, D_MODEL, SEG_W), lambda i, j: (layer, 0, j)),
            pl.BlockSpec((D_MODEL, LANES), lambda i, j: (0, 0)),
            pl.BlockSpec((N_GATE_COLS, D_MODEL), lambda i, j: (0, 0)),
            pl.BlockSpec((tm, HEAD_DIM), lambda i, j: (i % nlt, 0)),
            pl.BlockSpec((tm, HEAD_DIM), lambda i, j: (i % nlt, 0)),
            pl.BlockSpec((1, HEAD_DIM), lambda i, j: (0, 0)),
            pl.BlockSpec((1, HEAD_DIM), lambda i, j: (0, 0)),
        ],
        out_specs=[
            pl.BlockSpec((tm, SEG_W), lambda i, j: (i, j)),
            pl.BlockSpec((tm, LANES), lambda i, j: (i, 0)),
            pl.BlockSpec((N_GATE_COLS, tm), lambda i, j: (0, i)),
        ],
        out_shape=[
            jax.ShapeDtypeStruct((T, N_SEG * SEG_W), BF16),
            jax.ShapeDtypeStruct((T, LANES), F32),
            jax.ShapeDtypeStruct((N_GATE_COLS, T), F32),
        ],
        scratch_shapes=[pltpu.VMEM((tm, D_MODEL), BF16)],
        compiler_params=_cparams(("arbitrary", "arbitrary")),
        name="inproj",
    )(x2d, nw, w_main, wif, wift, cos_t, sin_t, qn, kn)


def _ret_state_kernel(k_ref, v_ref, kb_ref, cdb_ref, sb_ref, s_scr):
    c = pl.program_id(1)

    @pl.when(c == 0)
    def _():
        s_scr[...] = jnp.zeros_like(s_scr)

    sb_ref[0] = s_scr[...].astype(BF16)
    kk = (k_ref[...].astype(F32) * kb_ref[...]).astype(BF16)
    v = v_ref[...]
    for h in range(N_HEADS):
        kv = _dot_tn(kk[:, _hs(h)], v[:, _hs(h)])
        s_scr[h] = s_scr[h] * cdb_ref[h:h + 1, :] + kv


def _ret_out_kernel(q_ref, k_ref, v_ref, g_ref, sb_ref, dmat_ref, qf_ref, qb_ref, kf_ref, cdf_ref, nw_ref,
                    o_ref, s_scr):
    c = pl.program_id(1)

    @pl.when(c == 0)
    def _():
        s_scr[...] = jnp.zeros_like(s_scr)

    q = q_ref[...]
    k = k_ref[...]
    v = v_ref[...]
    qf32 = q.astype(F32)
    q_fwd = (qf32 * qf_ref[...]).astype(BF16)
    q_bwd = (qf32 * qb_ref[...]).astype(BF16)
    k_end = (k.astype(F32) * kf_ref[...]).astype(BF16)
    g = g_ref[...].astype(F32)
    for h in range(N_HEADS):
        hs = _hs(h)
        s = _dot_nt(q[:, hs], k[:, hs]) * dmat_ref[h]
        o = _dot(s.astype(BF16), v[:, hs])
        o = o + _dot(q_fwd[:, hs], s_scr[h].astype(BF16))
        o = o + _dot(q_bwd[:, hs], sb_ref[0, h])
        s_scr[h] = s_scr[h] * cdf_ref[h:h + 1, :] + _dot_tn(k_end[:, hs], v[:, hs])
        y = o * lax.rsqrt(jnp.mean(o * o, -1, keepdims=True) + EPS) * nw_ref[h:h + 1, :]
        gh = g[:, hs]
        o_ref[:, hs] = (y * (gh * _sigmoid(gh))).astype(BF16)


def _retention(proj, B, L, ret_decay, ret_norm_w):
    n = L // CHUNK
    T = B * L
    lg = jax.nn.log_sigmoid(ret_decay.astype(F32))
    idx = jnp.arange(CHUNK, dtype=F32)
    diff = idx[:, None] - idx[None, :]
    dmat = jnp.where(diff >= 0, jnp.exp(jnp.maximum(diff, 0.0) * lg[0][:, None, None]),
                     jnp.exp(jnp.maximum(-diff, 0.0) * lg[1][:, None, None]))

    def lane_tab(e):
        return jnp.repeat(jnp.exp(e).T, HEAD_DIM, axis=1)

    qf_tab = lane_tab((idx + 1.0)[None, :] * lg[0][:, None])
    qb_tab = lane_tab((CHUNK - idx)[None, :] * lg[1][:, None])
    kf_tab = lane_tab((CHUNK - 1.0 - idx)[None, :] * lg[0][:, None])
    kb_tab = lane_tab(idx[None, :] * lg[1][:, None])
    cdf = jnp.broadcast_to(jnp.exp(CHUNK * lg[0])[:, None], (N_HEADS, HEAD_DIM))
    cdb = jnp.broadcast_to(jnp.exp(CHUNK * lg[1])[:, None], (N_HEADS, HEAD_DIM))

    def seg(s, rev=False):
        if rev:
            return pl.BlockSpec((CHUNK, SEG_W), lambda b, c: (b * n + n - 1 - c, s))
        return pl.BlockSpec((CHUNK, SEG_W), lambda b, c: (b * n + c, s))

    full = lambda shape: pl.BlockSpec(shape, lambda b, c: (0,) * len(shape))
    st_shape = (1, N_HEADS, HEAD_DIM, HEAD_DIM)

    state = dict(
        kernel=_ret_state_kernel,
        in_specs=[seg(SEG_RK, True), seg(SEG_RV, True), full((CHUNK, MIX_W)), full((N_HEADS, HEAD_DIM))],
        args=[proj, proj, kb_tab, cdb],
        out_specs=[pl.BlockSpec(st_shape, lambda b, c: (b * n + n - 1 - c, 0, 0, 0))],
        out_shape=[jax.ShapeDtypeStruct((B * n, N_HEADS, HEAD_DIM, HEAD_DIM), BF16)],
        scratch=[pltpu.VMEM((N_HEADS, HEAD_DIM, HEAD_DIM), F32)])

    def out(sb):
        return dict(
            kernel=_ret_out_kernel,
            in_specs=[seg(SEG_RQ), seg(SEG_RK), seg(SEG_RV), seg(SEG_RG),
                      pl.BlockSpec(st_shape, lambda b, c: (b * n + c, 0, 0, 0)),
                      full((N_HEADS, CHUNK, CHUNK)), full((CHUNK, MIX_W)), full((CHUNK, MIX_W)),
                      full((CHUNK, MIX_W)), full((N_HEADS, HEAD_DIM)), full((N_HEADS, HEAD_DIM))],
            args=[proj, proj, proj, proj, sb, dmat, qf_tab, qb_tab, kf_tab, cdf, ret_norm_w.astype(F32)],
            out_specs=[pl.BlockSpec((CHUNK, MIX_W), lambda b, c: (b * n + c, 0))],
            out_shape=[jax.ShapeDtypeStruct((T, MIX_W), BF16)],
            scratch=[pltpu.VMEM((N_HEADS, HEAD_DIM, HEAD_DIM), F32)])

    return state, out


def _na_index_tables(nqb):
    rows = NA_QROWS * nqb
    qi = np.arange(CHUNK)
    ki = np.arange(NA_KW)

    def one(qb):
        kb = NA_QROWS * int(np.clip(qb - 2, 0, nqb - NA_KBLKS))
        r = (NA_QROWS * qb + qi // GRID_W)[:, None]
        qc = (qi % GRID_W)[:, None]
        kr = (kb + ki // GRID_W)[None, :]
        kc = (ki % GRID_W)[None, :]
        rs = np.clip(r - WIN_ROWS // 2, 0, rows - WIN_ROWS)
        cs = np.clip(qc - WIN_COLS // 2, 0, GRID_W - WIN_COLS)
        valid = (kr >= rs) & (kr < rs + WIN_ROWS) & (kc >= cs) & (kc < cs + WIN_COLS)
        dr = np.clip(kr - r + WIN_ROWS - 1, 0, 2 * WIN_ROWS - 2)
        dc = np.clip(kc - qc + WIN_COLS - 1, 0, 2 * WIN_COLS - 2)
        return dr + 0 * dc, dc + 0 * dr, valid

    reps = [0, 1, 2, nqb - 2, nqb - 1]
    tabs = [one(qb) for qb in reps]
    for qb in range(2, nqb - 2):
        t = one(qb)
        assert all(np.array_equal(a, b) for a, b in zip(t, tabs[2]))
    return tuple(np.stack([t[i] for t in tabs]) for i in range(3))


def _na_kernel(q_ref, k0, k1, k2, k3, k4, v0, v1, v2, v3, v4, bias_ref, o_ref):
    k_refs = (k0, k1, k2, k3, k4)
    v_refs = (v0, v1, v2, v3, v4)
    for h in range(N_HEADS):
        hs = _hs(h)
        q = q_ref[:, hs]
        k_all = jnp.concatenate([r[:, hs] for r in k_refs], 0)
        v_all = jnp.concatenate([r[:, hs] for r in v_refs], 0)
        s = _dot_nt(q, k_all) + bias_ref[0, h]
        p = jnp.exp(s - jnp.max(s, -1, keepdims=True))
        o = _dot(p.astype(BF16), v_all)
        o_ref[:, hs] = (o / jnp.sum(p, -1, keepdims=True)).astype(BF16)


def _neighborhood(proj, B, L, rpb):
    nqb = L // CHUNK
    assert nqb >= NA_KBLKS and L % (GRID_W * NA_QROWS) == 0
    T = B * L
    dr, dc, valid = _na_index_tables(nqb)
    n_cls = dr.shape[0]
    kr_n = NA_KW // GRID_W
    dr6 = dr.reshape(n_cls, NA_QROWS, GRID_W, kr_n, GRID_W)
    dc6 = dc.reshape(n_cls, NA_QROWS, GRID_W, kr_n, GRID_W)
    dr_s = dr6[:, :, 0, :, 0]
    dc_s = dc6[0, 0, :, 0, :]
    assert np.array_equal(dr6, np.broadcast_to(dr_s[:, :, None, :, None], dr6.shape))
    assert np.array_equal(dc6, np.broadcast_to(dc_s[None, None, :, None, :], dc6.shape))
    sel_r = (dr_s[..., None] == np.arange(2 * WIN_ROWS - 1)).astype(np.float32)
    sel_c = (np.arange(2 * WIN_COLS - 1)[:, None, None] == dc_s[None]).astype(np.float32)
    hp = lax.Precision.HIGHEST
    t = jnp.einsum('cajr,hrd->cajhd', sel_r, rpb.astype(F32), precision=hp)
    bias = jnp.einsum('cajhd,dqk->chaqjk', t, sel_c, precision=hp).reshape(n_cls, N_HEADS, CHUNK, NA_KW)
    bias = jnp.where(jnp.asarray(valid)[:, None], bias, NEG)

    def kspec(s, j):
        return pl.BlockSpec((CHUNK, SEG_W),
                            lambda b, qb: (b * nqb + jnp.clip(qb - 2, 0, nqb - NA_KBLKS) + j, s))

    def cls(qb):
        return jnp.where(qb < 2, qb, jnp.where(qb >= nqb - 2, qb - (nqb - 5), 2))

    return dict(
        kernel=_na_kernel,
        in_specs=[pl.BlockSpec((CHUNK, SEG_W), lambda b, qb: (b * nqb + qb, SEG_NQ))]
        + [kspec(SEG_NK, j) for j in range(NA_KBLKS)]
        + [kspec(SEG_NV, j) for j in range(NA_KBLKS)]
        + [pl.BlockSpec((1, N_HEADS, CHUNK, NA_KW), lambda b, qb: (cls(qb), 0, 0, 0))],
        args=[proj] * (1 + 2 * NA_KBLKS) + [bias],
        out_specs=[pl.BlockSpec((CHUNK, MIX_W), lambda b, qb: (b * nqb + qb, 0))],
        out_shape=[jax.ShapeDtypeStruct((T, MIX_W), BF16)],
        scratch=[])


HALO = 16


def _log_sigmoid(x):
    return jnp.minimum(x, 0.0) - jnp.log1p(jnp.exp(-jnp.abs(x)))


def _conv_silu(x_ref, prev_ref, next_ref, w_ref, col0, c, n):
    x = x_ref[...].astype(F32)
    row = lax.broadcasted_iota(jnp.int32, x.shape, 0)
    prev_row = prev_ref[HALO - 1:HALO, :].astype(F32) * (c > 0).astype(F32)
    next_row = next_ref[0:1, :].astype(F32) * (c < n - 1).astype(F32)
    x_prev = jnp.where(row == 0, prev_row, pltpu.roll(x, 1, 0))
    x_next = jnp.where(row == CHUNK - 1, next_row, pltpu.roll(x, CHUNK - 1, 0))
    cs = slice(col0, col0 + MIX_W)
    y = x_prev * w_ref[0:1, cs] + x * w_ref[1:2, cs] + x_next * w_ref[2:3, cs]
    return y * _sigmoid(y)


def _tri():
    r = lax.broadcasted_iota(jnp.int32, (CHUNK, CHUNK), 0)
    c = lax.broadcasted_iota(jnp.int32, (CHUNK, CHUNK), 1)
    return (c <= r).astype(F32), (c >= r).astype(F32)


def _hp_dot(a, b):
    return jnp.dot(a, b, preferred_element_type=F32, precision=lax.Precision.HIGHEST)


FWD_LANE = 2 * N_HEADS
BWD_LANE = 3 * N_HEADS


def _gate_dense(gc_ref, gr_ref, bc_ref, br_ref):
    low, up = _tri()
    g_col = gc_ref[...] + bc_ref[...]
    lf_col = _log_sigmoid(g_col)
    lf_row = _log_sigmoid(gr_ref[...] + br_ref[...])
    b_col = _hp_dot(low, lf_col)
    b_row = _hp_dot(lf_row, up)
    tot_row = b_col[CHUNK - 1:CHUNK, :]
    tot_col = b_row[:, CHUNK - 1:CHUNK]
    lane = lax.broadcasted_iota(jnp.int32, (CHUNK, LANES), 1)
    row = lax.broadcasted_iota(jnp.int32, (N_GATE_COLS, CHUNK), 0)
    bb_col = jnp.where(lane >= BWD_LANE, tot_row - b_col + lf_col, b_col)
    bb_row = jnp.where(row >= BWD_LANE, tot_col - b_row + lf_row, b_row)
    ck = pltpu.roll(g_col, FWD_LANE, 1) - bb_col
    a = tot_row + ck
    m_loc = jnp.max(a, 0, keepdims=True)
    ea = jnp.exp(a - m_loc)
    return bb_row, ck, tot_row, m_loc, ea


def _lane_bcast(x, l):
    return jnp.broadcast_to(x[:, l:l + 1], (x.shape[0], HEAD_DIM))


def _ml_state_update(kc_h, ct_loc_fn, ea, tot_row, m_loc, l, c_scr, n_scr, m_scr, h):
    m_p = m_scr[h:h + 1, 0:1]
    g_tot = tot_row[:, l:l + 1]
    m_l = m_loc[:, l:l + 1]
    kw = kc_h * _lane_bcast(ea, l)
    ct_loc = ct_loc_fn(kw.astype(BF16))
    n_loc = jnp.sum(kw, 0, keepdims=True)
    m_new = jnp.maximum(g_tot + m_p, m_l)
    sp = jnp.exp(g_tot + m_p - m_new)
    sl = jnp.exp(m_l - m_new)
    c_scr[h] = sp * c_scr[h] + sl * ct_loc
    n_scr[h:h + 1, :] = sp * n_scr[h:h + 1, :] + sl * n_loc
    m_scr[h:h + 1, :] = jnp.broadcast_to(m_new, (1, HEAD_DIM))


def _ml_init(c, c_scr, n_scr, m_scr):
    @pl.when(c == 0)
    def _():
        c_scr[...] = jnp.zeros_like(c_scr)
        n_scr[...] = jnp.zeros_like(n_scr)
        m_scr[...] = jnp.full_like(m_scr, NEG)


def _ml_state_kernel(k_ref, kp_ref, kn_ref, v_ref, gc_ref, gr_ref, cw_ref, bc_ref, br_ref,
                     cb_ref, nm_ref, c_scr, n_scr, m_scr, *, n):
    step = pl.program_id(1)
    _ml_init(step, c_scr, n_scr, m_scr)
    c = n - 1 - step
    cb_ref[0] = c_scr[...].astype(BF16)
    nm_ref[0, 0:N_HEADS, :] = n_scr[...]
    nm_ref[0, N_HEADS:2 * N_HEADS, :] = m_scr[...]

    kc = _conv_silu(k_ref, kp_ref, kn_ref, cw_ref, MIX_W, c, n) * (HEAD_DIM ** -0.5)
    v = v_ref[...]
    _, _, tot_row, m_loc, ea = _gate_dense(gc_ref, gr_ref, bc_ref, br_ref)
    for h in range(N_HEADS):
        v_h = v[:, _hs(h)]
        _ml_state_update(kc[:, _hs(h)], lambda kw: _dot_tn(v_h, kw), ea, tot_row, m_loc, BWD_LANE + h,
                         c_scr, n_scr, m_scr, h)


def _ml_direction(st, mask, b_q, ck_b, m_p, qn, vt_bf, ct_bf, q_bf):
    dmat = jnp.where(mask, b_q + ck_b, NEG)
    inter = b_q + m_p
    m_row = jnp.maximum(jnp.max(dmat, 0, keepdims=True), inter)
    s = st * jnp.exp(dmat - m_row)
    e_int = jnp.exp(inter - m_row)
    num = _dot(vt_bf, s.astype(BF16)) + e_int * _dot_nt(ct_bf, q_bf)
    den = jnp.sum(s, 0, keepdims=True) + e_int * qn
    return num / jnp.maximum(jnp.abs(den), jnp.exp(-m_row))


def _ml_out_kernel(q_ref, qp_ref, qn_ref, k_ref, kp_ref, kn_ref, v_ref, o_ref_in, gc_ref, gr_ref,
                   cb_ref, nm_ref, cw_ref, bc_ref, br_ref, nwt_ref,
                   out_ref, c_scr, n_scr, m_scr, *, n):
    c = pl.program_id(1)
    _ml_init(c, c_scr, n_scr, m_scr)
    qc = _conv_silu(q_ref, qp_ref, qn_ref, cw_ref, 0, c, n)
    kc = _conv_silu(k_ref, kp_ref, kn_ref, cw_ref, MIX_W, c, n) * (HEAD_DIM ** -0.5)
    q_bf = qc.astype(BF16)
    k_bf = kc.astype(BF16)
    v = v_ref[...]
    og = o_ref_in[...].astype(F32)
    bb_row, ck, tot_row, m_loc, ea = _gate_dense(gc_ref, gr_ref, bc_ref, br_ref)
    key = lax.broadcasted_iota(jnp.int32, (CHUNK, CHUNK), 0)
    qry = lax.broadcasted_iota(jnp.int32, (CHUNK, CHUNK), 1)
    causal = key <= qry
    anti = key >= qry
    pad = jnp.zeros((8 - 2, HEAD_DIM), F32)
    for h in range(N_HEADS):
        hs = _hs(h)
        lf, lb = FWD_LANE + h, BWD_LANE + h
        qh = q_bf[:, hs]
        vt_bf = v[:, hs].astype(F32).T.astype(BF16)
        st = _dot_nt(k_bf[:, hs], qh)
        n_prev = jnp.concatenate([n_scr[h:h + 1, :], nm_ref[0, h:h + 1, :], pad], 0).astype(BF16)
        qn = _dot_nt(n_prev, qh)
        h_f = _ml_direction(st, causal, bb_row[lf:lf + 1, :], _lane_bcast(ck, lf), m_scr[h:h + 1, 0:1],
                            qn[0:1, :], vt_bf, c_scr[h].astype(BF16), qh)
        h_b = _ml_direction(st, anti, bb_row[lb:lb + 1, :], _lane_bcast(ck, lb),
                            nm_ref[0, N_HEADS + h:N_HEADS + h + 1, 0:1], qn[1:2, :], vt_bf, cb_ref[0, h], qh)
        _ml_state_update(kc[:, hs], lambda kw: _dot(vt_bf, kw), ea, tot_row, m_loc, lf, c_scr, n_scr, m_scr, h)
        ot = h_f + h_b
        yt = ot * lax.rsqrt(jnp.mean(ot * ot, 0, keepdims=True) + EPS) * nwt_ref[h]
        out_ref[:, hs] = (yt.T * _sigmoid(og[:, hs])).astype(BF16)


def _mlstm(proj, gates, gates_t, B, L, conv_w, ig_b, fg_b, norm_w):
    n = L // CHUNK
    T = B * L
    hb = CHUNK // HALO
    bias = jnp.concatenate([ig_b.astype(F32).reshape(-1), fg_b.astype(F32).reshape(-1)])
    bias_col = jnp.zeros((1, LANES), F32).at[0, :N_GATE_COLS].set(bias)
    bias_row = jnp.broadcast_to(bias[:, None], (N_GATE_COLS, CHUNK))
    cw = conv_w.astype(F32)
    norm_wt = jnp.broadcast_to(norm_w.astype(F32)[:, :, None], (N_HEADS, HEAD_DIM, CHUNK))

    def seg(s, rev):
        if rev:
            return pl.BlockSpec((CHUNK, SEG_W), lambda b, c: (b * n + n - 1 - c, s))
        return pl.BlockSpec((CHUNK, SEG_W), lambda b, c: (b * n + c, s))

    def chunk_of(c, rev):
        return n - 1 - c if rev else c

    def prev_spec(s, rev):
        return pl.BlockSpec((HALO, SEG_W),
                            lambda b, c: (b * n * hb + jnp.maximum(chunk_of(c, rev) * hb - 1, 0), s))

    def next_spec(s, rev):
        return pl.BlockSpec((HALO, SEG_W),
                            lambda b, c: (b * n * hb + jnp.minimum((chunk_of(c, rev) + 1) * hb, n * hb - 1), s))

    def gcol_spec(rev):
        return pl.BlockSpec((CHUNK, LANES), lambda b, c: (b * n + chunk_of(c, rev), 0))

    def grow_spec(rev):
        return pl.BlockSpec((N_GATE_COLS, CHUNK), lambda b, c: (0, b * n + chunk_of(c, rev)))

    full = lambda shape: pl.BlockSpec(shape, lambda b, c: (0,) * len(shape))
    st_shape = (1, N_HEADS, HEAD_DIM, HEAD_DIM)
    nm_shape = (1, 2 * N_HEADS, HEAD_DIM)
    scratch = [pltpu.VMEM((N_HEADS, HEAD_DIM, HEAD_DIM), F32), pltpu.VMEM((N_HEADS, HEAD_DIM), F32),
               pltpu.VMEM((N_HEADS, HEAD_DIM), F32)]
    consts = [full((ML_CONV_W, 2 * MIX_W)), full((1, LANES)), full((N_GATE_COLS, CHUNK))]

    state = dict(
        kernel=functools.partial(_ml_state_kernel, n=n),
        in_specs=[seg(SEG_MK, True), prev_spec(SEG_MK, True), next_spec(SEG_MK, True), seg(SEG_MV, True),
                  gcol_spec(True), grow_spec(True)] + consts,
        args=[proj, proj, proj, proj, gates, gates_t, cw, bias_col, bias_row],
        out_specs=[pl.BlockSpec(st_shape, lambda b, c: (b * n + n - 1 - c, 0, 0, 0)),
                   pl.BlockSpec(nm_shape, lambda b, c: (b * n + n - 1 - c, 0, 0))],
        out_shape=[jax.ShapeDtypeStruct((B * n, N_HEADS, HEAD_DIM, HEAD_DIM), BF16),
                   jax.ShapeDtypeStruct((B * n, 2 * N_HEADS, HEAD_DIM), F32)],
        scratch=scratch)

    def out(cb, nm):
        return dict(
            kernel=functools.partial(_ml_out_kernel, n=n),
            in_specs=[seg(SEG_MQ, False), prev_spec(SEG_MQ, False), next_spec(SEG_MQ, False),
                      seg(SEG_MK, False), prev_spec(SEG_MK, False), next_spec(SEG_MK, False),
                      seg(SEG_MV, False), seg(SEG_MO, False), gcol_spec(False), grow_spec(False),
                      pl.BlockSpec(st_shape, lambda b, c: (b * n + c, 0, 0, 0)),
                      pl.BlockSpec(nm_shape, lambda b, c: (b * n + c, 0, 0))]
            + consts + [full((N_HEADS, HEAD_DIM, CHUNK))],
            args=[proj, proj, proj, proj, proj, proj, proj, proj, gates, gates_t, cb, nm,
                  cw, bias_col, bias_row, norm_wt],
            out_specs=[pl.BlockSpec((CHUNK, MIX_W), lambda b, c: (b * n + c, 0))],
            out_shape=[jax.ShapeDtypeStruct((T, MIX_W), BF16)],
            scratch=scratch)

    return state, out


def _fused_call(parts, grid, name):
    n_in = [len(p["in_specs"]) for p in parts]
    n_out = [len(p["out_specs"]) for p in parts]
    n_scr = [len(p["scratch"]) for p in parts]

    def body(*refs):
        ins, outs, scr = refs[:sum(n_in)], refs[sum(n_in):sum(n_in) + sum(n_out)], refs[sum(n_in) + sum(n_out):]
        i = o = k = 0
        for p, a, b, c in zip(parts, n_in, n_out, n_scr):
            p["kernel"](*ins[i:i + a], *outs[o:o + b], *scr[k:k + c])
            i, o, k = i + a, o + b, k + c

    return pl.pallas_call(
        body,
        grid=grid,
        in_specs=[s for p in parts for s in p["in_specs"]],
        out_specs=[s for p in parts for s in p["out_specs"]],
        out_shape=[s for p in parts for s in p["out_shape"]],
        scratch_shapes=[s for p in parts for s in p["scratch"]],
        compiler_params=_cparams(("arbitrary", "arbitrary")),
        name=name,
    )(*[a for p in parts for a in p["args"]])


def _mixers(proj, gates, gates_t, B, L, p):
    grid = (B, L // CHUNK)
    ret_state, ret_out = _retention(proj, B, L, p["ret_decay"], p["ret_norm_w"])
    ml_state, ml_out = _mlstm(proj, gates, gates_t, B, L, p["ml_conv"], p["ml_igate_b"], p["ml_fgate_b"],
                              p["ml_norm_w"])
    sb, cb, nm = _fused_call([ret_state, ml_state], grid, "mixer_states")
    na = _neighborhood(proj, B, L, p["na_rpb"])
    return _fused_call([ret_out(sb), na, ml_out(cb, nm)], grid, "mixer_outputs")


def _branch_kernel(oret_ref, ona_ref, oml_ref, ga_ref, gb_ref, gc_ref, wb_ref, bgb_ref, m_ref):
    o_refs = (oret_ref, ona_ref, oml_ref)
    g_refs = (ga_ref, gb_ref, gc_ref)
    for n in range(D_MODEL // SEG_W):
        cs = slice(n * SEG_W, (n + 1) * SEG_W)
        acc = None
        for i in range(N_BRANCH):
            z = g_refs[i][:, cs].astype(F32) + bgb_ref[i:i + 1, cs]
            term = _sigmoid(z) * _dot(o_refs[i][...], wb_ref[i, :, cs])
            acc = term if acc is None else acc + term
        m_ref[:, cs] = acc.astype(BF16)


def _outproj_router_kernel(x_ref, m_ref, wout_ref, n2w_ref, rwh_ref, rwl_ref, rb_ref,
                           h_ref, hn_ref, route_ref, cnt_ref, cnt_scr):
    h = x_ref[...] + _dot(m_ref[...], wout_ref[...])
    h_ref[...] = h
    hn = h * lax.rsqrt(jnp.mean(h * h, -1, keepdims=True) + EPS) * n2w_ref[...]
    hn_ref[...] = hn

    hn_hi = hn.astype(BF16)
    hn_lo = (hn - hn_hi.astype(F32)).astype(BF16)
    logits = (_dot(hn_hi, rwh_ref[...]) + _dot(hn_lo, rwh_ref[...]) + _dot(hn_hi, rwl_ref[...])) + rb_ref[...]
    lane_i = lax.broadcasted_iota(jnp.int32, logits.shape, 1)
    lane = lane_i.astype(F32)
    lane_grp = jnp.right_shift(lane_i - N_GROUPS, 3).astype(F32)
    gmask = lane_i < N_GROUPS
    lg = jnp.where(gmask, logits, NEG)
    mg = jnp.max(lg, -1, keepdims=True)
    p_top = 1.0 / jnp.sum(jnp.where(gmask, jnp.exp(lg - mg), 0.0), -1, keepdims=True)
    grp = jnp.min(jnp.where(lg == mg, lane, float(LANES)), -1, keepdims=True)
    emask = (lane_i >= N_GROUPS) & (lane_i < N_GROUPS + N_EXPERTS) & (lane_grp == grp)
    le = jnp.where(emask, logits, NEG)
    m1 = jnp.max(le, -1, keepdims=True)
    i1 = jnp.min(jnp.where(le == m1, lane, float(LANES)), -1, keepdims=True)
    le2 = jnp.where(lane == i1, NEG, le)
    m2 = jnp.max(le2, -1, keepdims=True)
    i2 = jnp.min(jnp.where(le2 == m2, lane, float(LANES)), -1, keepdims=True)
    e2 = jnp.exp(m2 - m1)
    gate1 = p_top / (1.0 + e2)
    gate2 = p_top * e2 / (1.0 + e2)

    @pl.when(pl.program_id(0) == 0)
    def _():
        cnt_scr[...] = jnp.zeros_like(cnt_scr)

    tm = logits.shape[0]
    hit1 = lane == i1
    hit2 = lane == i2
    onehot = jnp.where(hit1 | hit2, 1.0, 0.0)
    r_i = lax.broadcasted_iota(jnp.int32, (tm, tm), 0)
    c_i = lax.broadcasted_iota(jnp.int32, (tm, tm), 1)
    before = jnp.where(c_i < r_i, 1.0, 0.0).astype(BF16)
    prior = _dot(before, onehot.astype(BF16)) + cnt_scr[...]
    rank1 = jnp.sum(jnp.where(hit1, prior, 0.0), -1, keepdims=True)
    rank2 = jnp.sum(jnp.where(hit2, prior, 0.0), -1, keepdims=True)
    cnt = cnt_scr[...] + jnp.sum(onehot, 0, keepdims=True)
    cnt_scr[...] = cnt
    cnt_ref[...] = jnp.broadcast_to(cnt, cnt_ref.shape)

    cols = (i1 - N_GROUPS, i2 - N_GROUPS, gate1, gate2, rank1, rank2)
    route = jnp.zeros_like(logits)
    for ci, val in enumerate(cols):
        route = jnp.where(lane_i == ci, val, route)
    route_ref[...] = route


MERGE_ROWS = 512


def _merge(x2d, o_ret, o_na, o_ml, proj, wb, bgb, wout, n2w, rw_hi, rw_lo, rb):
    T = x2d.shape[0]
    tm = MERGE_ROWS
    full = lambda shape: pl.BlockSpec(shape, lambda i: (0,) * len(shape), pipeline_mode=pl.Buffered(1))
    row = lambda w: pl.BlockSpec((tm, w), lambda i: (i, 0))
    gate = lambda s: pl.BlockSpec((tm, D_MODEL), lambda i: (i, s // 2))
    merged = pl.pallas_call(
        _branch_kernel,
        grid=(T // tm,),
        in_specs=[row(MIX_W), row(MIX_W), row(MIX_W), gate(SEG_GA), gate(SEG_GB), gate(SEG_GC),
                  full((N_BRANCH, MIX_W, D_MODEL)), full((N_BRANCH, D_MODEL))],
        out_specs=row(D_MODEL),
        out_shape=jax.ShapeDtypeStruct((T, D_MODEL), BF16),
        compiler_params=_cparams(("arbitrary",)),
        name="branch_merge",
    )(o_ret, o_na, o_ml, proj, proj, proj, wb, bgb)
    return pl.pallas_call(
        _outproj_router_kernel,
        grid=(T // tm,),
        in_specs=[row(D_MODEL), row(D_MODEL), full((D_MODEL, D_MODEL)), full((1, D_MODEL)),
                  full((D_MODEL, LANES)), full((D_MODEL, LANES)), full((1, LANES))],
        out_specs=[row(D_MODEL), row(D_MODEL), row(LANES), pl.BlockSpec((8, LANES), lambda i: (0, 0))],
        out_shape=[jax.ShapeDtypeStruct((T, D_MODEL), F32), jax.ShapeDtypeStruct((T, D_MODEL), F32),
                   jax.ShapeDtypeStruct((T, LANES), F32), jax.ShapeDtypeStruct((8, LANES), F32)],
        scratch_shapes=[pltpu.VMEM((1, LANES), F32)],
        compiler_params=_cparams(("arbitrary",)),
        name="outproj_router",
    )(x2d, merged, wout, n2w, rw_hi, rw_lo, rb)


def _row_copy(src_hbm, row, dst, r, sem):
    return pltpu.make_async_copy(src_hbm.at[pl.ds(row, 1), :], dst.at[pl.ds(r, 1), :], sem)


def _ffn_kernel(blk_e_ref, n_used_ref, src_ref, hn_hbm, wg_ref, wu_ref, wd_ref, y_ref, xbuf, sem):
    i = pl.program_id(0)
    n_used = n_used_ref[0]
    slot = i % 2

    def wait_block(s):
        pltpu.make_async_copy(hn_hbm.at[pl.ds(0, MOE_ROWS), :], xbuf.at[s], sem.at[s]).wait()

    @pl.when(i == 0)
    def _():
        def body(r, carry):
            _row_copy(hn_hbm, src_ref[r], xbuf.at[0], r, sem.at[0]).start()
            return carry
        lax.fori_loop(0, MOE_ROWS, body, 0, unroll=8)

    @pl.when(i < n_used)
    def _():
        for r in range(MOE_ROWS):
            _row_copy(hn_hbm, src_ref[(i + 1) * MOE_ROWS + r], xbuf.at[1 - slot], r, sem.at[1 - slot]).start()
        wait_block(slot)
        x = xbuf[slot].astype(BF16)
        a = _dot(x, wg_ref[0])
        hid = (a * _sigmoid(a)) * _dot(x, wu_ref[0])
        y_ref[...] = _dot(hid.astype(BF16), wd_ref[0])

    @pl.when(i >= n_used)
    def _():
        y_ref[...] = jnp.zeros_like(y_ref)

    @pl.when(i == n_used)
    def _():
        wait_block(slot)


def _expert_ffn(hn, src, blk_e, n_used, wg, wu, wd):
    n_blocks = src.shape[0] // MOE_ROWS
    grid_spec = pltpu.PrefetchScalarGridSpec(
        num_scalar_prefetch=3,
        grid=(n_blocks,),
        in_specs=[
            pl.BlockSpec(memory_space=pl.ANY),
            pl.BlockSpec((1, D_MODEL, D_EXPERT), lambda i, be, nu, sr: (be[i], 0, 0)),
            pl.BlockSpec((1, D_MODEL, D_EXPERT), lambda i, be, nu, sr: (be[i], 0, 0)),
            pl.BlockSpec((1, D_EXPERT, D_MODEL), lambda i, be, nu, sr: (be[i], 0, 0)),
        ],
        out_specs=pl.BlockSpec((MOE_ROWS, D_MODEL), lambda i, be, nu, sr: (i, 0)),
        scratch_shapes=[pltpu.VMEM((2, MOE_ROWS, D_MODEL), F32), pltpu.SemaphoreType.DMA((2,))],
    )
    return pl.pallas_call(
        _ffn_kernel,
        grid_spec=grid_spec,
        out_shape=jax.ShapeDtypeStruct((n_blocks * MOE_ROWS, D_MODEL), F32),
        compiler_params=_cparams(("arbitrary",), disable_bounds_checks=True),
        name="expert_ffn",
    )(blk_e, n_used, src, hn, wg, wu, wd)


COMBINE_ROWS = 256


def _combine_kernel(dest_ref, h_ref, route_ref, yb_hbm, out_ref, ybuf, sem):
    i = pl.program_id(0)
    slot = i % 2

    def start_row(tile, r, to_slot):
        a = (tile * COMBINE_ROWS + r) * TOP_K_INNER
        for k in range(TOP_K_INNER):
            _row_copy(yb_hbm, dest_ref[a + k], ybuf.at[to_slot, k], r, sem.at[to_slot]).start()

    @pl.when(i == 0)
    def _():
        def body(r, carry):
            start_row(0, r, 0)
            return carry
        lax.fori_loop(0, COMBINE_ROWS, body, 0, unroll=8)

    @pl.when(i + 1 < pl.num_programs(0))
    def _():
        for r in range(COMBINE_ROWS):
            start_row(i + 1, r, 1 - slot)

    for k in range(TOP_K_INNER):
        pltpu.make_async_copy(yb_hbm.at[pl.ds(0, COMBINE_ROWS), :], ybuf.at[slot, k], sem.at[slot]).wait()

    route = route_ref[...]
    out_ref[...] = h_ref[...] + route[:, 2:3] * ybuf[slot, 0] + route[:, 3:4] * ybuf[slot, 1]


def _combine(h, route, yb, dest):
    T = h.shape[0]
    grid_spec = pltpu.PrefetchScalarGridSpec(
        num_scalar_prefetch=1,
        grid=(T // COMBINE_ROWS,),
        in_specs=[
            pl.BlockSpec((COMBINE_ROWS, D_MODEL), lambda i, d: (i, 0)),
            pl.BlockSpec((COMBINE_ROWS, LANES), lambda i, d: (i, 0)),
            pl.BlockSpec(memory_space=pl.ANY),
        ],
        out_specs=pl.BlockSpec((COMBINE_ROWS, D_MODEL), lambda i, d: (i, 0)),
        scratch_shapes=[pltpu.VMEM((2, TOP_K_INNER, COMBINE_ROWS, D_MODEL), F32), pltpu.SemaphoreType.DMA((2,))],
    )
    return pl.pallas_call(
        _combine_kernel,
        grid_spec=grid_spec,
        out_shape=jax.ShapeDtypeStruct((T, D_MODEL), F32),
        compiler_params=_cparams(("arbitrary",), disable_bounds_checks=True),
        name="moe_combine",
    )(dest, h, route, yb)


def _moe(h, hn, route, counts, wg, wu, wd, layer):
    T = h.shape[0]
    n_assign = T * TOP_K_INNER
    expert = route[:, 0:2].astype(jnp.int32)
    rank = route[:, 4:6].astype(jnp.int32)
    cnt = counts[0, N_GROUPS:N_GROUPS + N_EXPERTS].astype(jnp.int32)
    padded = (cnt + MOE_ROWS - 1) // MOE_ROWS * MOE_ROWS
    pend = jnp.cumsum(padded)
    pstart = pend - padded
    dest = (pstart[expert] + rank).reshape(-1)
    n_blocks = (n_assign + N_EXPERTS * (MOE_ROWS - 1) + MOE_ROWS - 1) // MOE_ROWS + 1
    blk_start = jnp.arange(n_blocks, dtype=jnp.int32) * MOE_ROWS
    blk_e = jnp.minimum(jnp.sum((pend[None, :] <= blk_start[:, None]).astype(jnp.int32), 1), N_EXPERTS - 1)
    blk_e = blk_e + layer * N_EXPERTS
    n_used = (pend[-1] // MOE_ROWS).astype(jnp.int32).reshape(1)
    flat_t = jnp.repeat(jnp.arange(T, dtype=jnp.int32), TOP_K_INNER)
    src = jnp.zeros((n_blocks * MOE_ROWS,), jnp.int32).at[dest].set(flat_t)
    yb = _expert_ffn(hn, src, blk_e, n_used, wg, wu, wd)
    return _combine(h, route, yb, dest)


def _rope_tables(L):
    freqs = ROPE_BASE ** (-jnp.arange(0, HEAD_DIM, 2, dtype=F32) / HEAD_DIM)
    ang = jnp.arange(L, dtype=F32)[:, None] * freqs[None]
    cos, sin = jnp.cos(ang), jnp.sin(ang)
    return jnp.concatenate([cos, cos], -1), jnp.concatenate([-sin, sin], -1)


N_MIX_SEG = 11


REORDER_COLS = SEG_W // 2
N_BRANCH_BLK = N_BRANCH * D_MODEL // REORDER_COLS


def _reorder_kernel(wm_ref, wg_ref, o_ref):
    j = pl.program_id(1)

    @pl.when(j < N_BRANCH_BLK)
    def _():
        o_ref[...] = wg_ref[...].T.astype(BF16)

    @pl.when(j >= N_BRANCH_BLK)
    def _():
        o_ref[...] = wm_ref[...].T.astype(BF16)


def _reorder_w_in(w_in_t):
    depth = w_in_t.shape[0]
    w_gate_t = w_in_t[:, N_MIX_SEG * SEG_W + N_GATE_COLS:, :]
    blk = (None, REORDER_COLS, D_MODEL)
    return pl.pallas_call(
        _reorder_kernel,
        grid=(depth, N_SEG * SEG_W // REORDER_COLS),
        in_specs=[pl.BlockSpec(blk, lambda l, j: (l, jnp.maximum(j - N_BRANCH_BLK, 0), 0)),
                  pl.BlockSpec(blk, lambda l, j: (l, jnp.minimum(j, N_BRANCH_BLK - 1), 0))],
        out_specs=pl.BlockSpec((None, D_MODEL, REORDER_COLS), lambda l, j: (l, 0, j)),
        out_shape=jax.ShapeDtypeStruct((depth, D_MODEL, N_SEG * SEG_W), BF16),
        compiler_params=_cparams(("arbitrary", "arbitrary")),
        name="reorder_w_in",
    )(w_in_t, w_gate_t)


def _cast_kernel(w_ref, o_ref):
    o_ref[...] = w_ref[...].astype(BF16)


def _cast_experts(w):
    depth, e, r, c = w.shape
    spec = pl.BlockSpec((1, r, c), lambda i: (i, 0, 0))
    return pl.pallas_call(
        _cast_kernel,
        grid=(depth * e,),
        in_specs=[spec],
        out_specs=spec,
        out_shape=jax.ShapeDtypeStruct((depth * e, r, c), BF16),
        compiler_params=_cparams(("arbitrary",)),
        name="cast_experts",
    )(w.reshape(depth * e, r, c))


def _prep_layer(p):
    wift = p["wif_t"].astype(BF16)
    wif_pad = jnp.zeros((D_MODEL, LANES), BF16).at[:, :N_GATE_COLS].set(wift.T)
    rw = jnp.zeros((D_MODEL, LANES), F32).at[:, :N_GROUPS].set(p["router_g_w"]) \
        .at[:, N_GROUPS:N_GROUPS + N_EXPERTS].set(p["router_e_w"])
    rb = jnp.zeros((1, LANES), F32).at[0, :N_GROUPS].set(p["router_g_b"]) \
        .at[0, N_GROUPS:N_GROUPS + N_EXPERTS].set(p["router_e_b"])
    return dict(
        nw=p["norm1_w"].astype(F32).reshape(1, D_MODEL), wif=wif_pad, wift=wift,
        qn=p["na_q_norm"].astype(F32).reshape(1, HEAD_DIM), kn=p["na_k_norm"].astype(F32).reshape(1, HEAD_DIM),
        wb=p["w_branch"].astype(BF16), bgb=p["branch_gate_b"].astype(F32), wout=p["w_out"].astype(BF16),
        n2w=p["norm2_w"].astype(F32).reshape(1, D_MODEL), rb=rb,
        rw_hi=rw.astype(BF16), rw_lo=(rw - rw.astype(BF16).astype(F32)).astype(BF16),
    )


def _layer(x2d, B, L, p, q, rope, w_main, experts, layer):
    proj, gates, gates_t = _inproj(x2d, L, q["nw"], w_main, layer, q["wif"], q["wift"], rope[0], rope[1],
                                   q["qn"], q["kn"])
    o_ret, o_na, o_ml = _mixers(proj, gates, gates_t, B, L, p)
    h, hn, route, counts = _merge(x2d, o_ret, o_na, o_ml, proj, q["wb"], q["bgb"], q["wout"], q["n2w"],
                                  q["rw_hi"], q["rw_lo"], q["rb"])
    return _moe(h, hn, route, counts, experts[0], experts[1], experts[2], layer)


_PARAM_NAMES = ("norm1_w", "w_in", "ret_decay", "ret_norm_w", "na_q_norm", "na_k_norm", "na_rpb", "ml_conv",
                "ml_igate_b", "ml_fgate_b", "ml_norm_w", "w_branch", "branch_gate_b", "w_out", "norm2_w",
                "router_g_w", "router_g_b", "router_e_w", "router_e_b", "exp_w_gate", "exp_w_up", "exp_w_down")


def _run(x, layers, preps, w_main, experts):
    B, L, D = x.shape
    rope = _rope_tables(L)
    x2d = x.reshape(B * L, D)
    for layer, (p, q) in enumerate(zip(layers, preps)):
        x2d = _layer(x2d, B, L, p, q, rope, w_main, experts, layer)
    return x2d.reshape(B, L, D)


def kernel(x_prompt, x_sample, norm1_w, w_in, ret_decay, ret_norm_w, na_q_norm, na_k_norm, na_rpb, ml_conv,
           ml_igate_b, ml_fgate_b, ml_norm_w, w_branch, branch_gate_b, w_out, norm2_w, router_g_w, router_g_b,
           router_e_w, router_e_b, exp_w_gate, exp_w_up, exp_w_down):
    stacked = (norm1_w, w_in, ret_decay, ret_norm_w, na_q_norm, na_k_norm, na_rpb, ml_conv, ml_igate_b,
               ml_fgate_b, ml_norm_w, w_branch, branch_gate_b, w_out, norm2_w, router_g_w, router_g_b,
               router_e_w, router_e_b, exp_w_gate, exp_w_up, exp_w_down)
    depth = w_in.shape[0]
    w_in_t = jnp.swapaxes(w_in, 1, 2)
    wif_t = w_in_t[:, N_MIX_SEG * SEG_W:N_MIX_SEG * SEG_W + N_GATE_COLS, :]
    layers = [dict(zip(_PARAM_NAMES, (None if a is w_in else a[l] for a in stacked)), wif_t=wif_t[l])
              for l in range(depth)]
    preps = [_prep_layer(p) for p in layers]
    experts = tuple(_cast_experts(w) for w in (exp_w_gate, exp_w_up, exp_w_down))
    w_main = _reorder_w_in(w_in_t)
    return (_run(x_prompt, layers, preps, w_main, experts), _run(x_sample, layers, preps, w_main, experts))
```

```python
import functools

import numpy as np
import jax
import jax.numpy as jnp
from jax import lax
from jax.experimental import pallas as pl
from jax.experimental.pallas import tpu as pltpu

D_MODEL = 2048
HEAD_DIM = 128
MIX_W = D_MODEL // 2
N_HEADS = MIX_W // HEAD_DIM
N_BRANCH = 3
CHUNK = 128
ROPE_BASE = 10000.0
GRID_W = 64
WIN_ROWS = 8
WIN_COLS = 16
ML_CONV_W = 3
N_GROUPS = 4
EXPERTS_PER_GROUP = 8
N_EXPERTS = N_GROUPS * EXPERTS_PER_GROUP
TOP_K_INNER = 2
D_EXPERT = D_MODEL // 2
EPS = 1e-6
NEG = -1e30

F32 = jnp.float32
BF16 = jnp.bfloat16

LANES = 128
SEG_W = MIX_W
SEG_GA, SEG_GB, SEG_GC = 0, 2, 4
SEG_RQ, SEG_RK, SEG_RV, SEG_RG = 6, 7, 8, 9
SEG_NQ, SEG_NK, SEG_NV = 10, 11, 12
SEG_MQ, SEG_MK, SEG_MV, SEG_MO = 13, 14, 15, 16
N_SEG = 17
N_GATE_COLS = 4 * N_HEADS

NA_QROWS = 2
NA_KBLKS = 5
NA_KW = NA_KBLKS * CHUNK
MOE_ROWS = 256
VMEM_LIMIT = 56 * 1024 * 1024


def _cparams(sem, **kw):
    return pltpu.CompilerParams(dimension_semantics=sem, vmem_limit_bytes=VMEM_LIMIT, **kw)


def _sigmoid(z):
    return 0.5 * jnp.tanh(0.5 * z) + 0.5


def _hs(h):
    return slice(h * HEAD_DIM, (h + 1) * HEAD_DIM)


def _heads(head, defer):
    if defer:
        return head
    for h in range(N_HEADS):
        head(h)


def _dot(a, b):
    return jnp.dot(a, b, preferred_element_type=F32)


def _dot_nt(a, b):
    return lax.dot_general(a, b, (((1,), (1,)), ((), ())), preferred_element_type=F32)


def _dot_tn(a, b):
    return lax.dot_general(a, b, (((0,), (0,)), ((), ())), preferred_element_type=F32)


def _inproj_kernel(x_ref, nw_ref, w_ref, wif_ref, wift_ref, cos_ref, sin_ref, qn_ref, kn_ref,
                   proj_ref, g_ref, gt_ref, xn_ref):
    j = pl.program_id(1)

    @pl.when(j == 0)
    def _():
        x = x_ref[...]
        y = x * lax.rsqrt(jnp.mean(x * x, -1, keepdims=True) + EPS) * nw_ref[...]
        xn = y.astype(BF16)
        xn_ref[...] = xn
        g_ref[...] = _dot(xn, wif_ref[...].astype(BF16))
        gt_ref[...] = _dot_nt(wift_ref[...].astype(BF16), xn)

    acc = _dot(xn_ref[...], w_ref[...])

    is_rope = (j == SEG_RQ) | (j == SEG_RK)
    is_norm = (j == SEG_NQ) | (j == SEG_NK)

    @pl.when(is_rope)
    def _():
        scale = jnp.where(j == SEG_RK, HEAD_DIM ** -0.5, 1.0).astype(F32)
        cos = cos_ref[...]
        sin = sin_ref[...]
        for h in range(N_HEADS):
            xh = acc[:, _hs(h)]
            r = xh * cos + pltpu.roll(xh, HEAD_DIM // 2, 1) * sin
            proj_ref[:, _hs(h)] = (r * scale).astype(BF16)

    @pl.when(is_norm)
    def _():
        w = jnp.where(j == SEG_NQ, qn_ref[...] * (HEAD_DIM ** -0.5), kn_ref[...])
        for h in range(N_HEADS):
            xh = acc[:, _hs(h)]
            y = xh * lax.rsqrt(jnp.mean(xh * xh, -1, keepdims=True) + EPS) * w
            proj_ref[:, _hs(h)] = y.astype(BF16)

    @pl.when(jnp.logical_not(is_rope | is_norm))
    def _():
        proj_ref[...] = acc.astype(BF16)


def _inproj(x2d, L, nw, w_main, layer, wif, wift, cos_t, sin_t, qn, kn):
    T = x2d.shape[0]
    tm = min(1024, L)
    nlt = L // tm
    return pl.pallas_call(
        _inproj_kernel,
        grid=(T // tm, N_SEG),
        in_specs=[
            pl.BlockSpec((tm, D_MODEL), lambda i, j: (i, 0)),
            pl.BlockSpec((1, D_MODEL), lambda i, j: (0, 0)),
            pl.BlockSpec((None, D_MODEL, SEG_W), lambda i, j: (layer, 0, j)),
            pl.BlockSpec((D_MODEL, LANES), lambda i, j: (0, 0)),
            pl.BlockSpec((N_GATE_COLS, D_MODEL), lambda i, j: (0, 0)),
            pl.BlockSpec((tm, HEAD_DIM), lambda i, j: (i % nlt, 0)),
            pl.BlockSpec((tm, HEAD_DIM), lambda i, j: (i % nlt, 0)),
            pl.BlockSpec((1, HEAD_DIM), lambda i, j: (0, 0)),
            pl.BlockSpec((1, HEAD_DIM), lambda i, j: (0, 0)),
        ],
        out_specs=[
            pl.BlockSpec((tm, SEG_W), lambda i, j: (i, j)),
            pl.BlockSpec((tm, LANES), lambda i, j: (i, 0)),
            pl.BlockSpec((N_GATE_COLS, tm), lambda i, j: (0, i)),
        ],
        out_shape=[
            jax.ShapeDtypeStruct((T, N_SEG * SEG_W), BF16),
            jax.ShapeDtypeStruct((T, LANES), F32),
            jax.ShapeDtypeStruct((N_GATE_COLS, T), F32),
        ],
        scratch_shapes=[pltpu.VMEM((tm, D_MODEL), BF16)],
        compiler_params=_cparams(("arbitrary", "arbitrary")),
        name="inproj",
    )(x2d, nw, w_main, wif, wift, cos_t, sin_t, qn, kn)


def _ret_state_kernel(k_ref, v_ref, kb_ref, cdb_ref, sb_ref, s_scr):
    c = pl.program_id(1)

    @pl.when(c == 0)
    def _():
        s_scr[...] = jnp.zeros_like(s_scr)

    sb_ref[0] = s_scr[...].astype(BF16)
    kk = (k_ref[...].astype(F32) * kb_ref[...]).astype(BF16)
    v = v_ref[...]
    for h in range(N_HEADS):
        kv = _dot_tn(kk[:, _hs(h)], v[:, _hs(h)])
        s_scr[h] = s_scr[h] * cdb_ref[h:h + 1, :] + kv


def _ret_out_kernel(q_ref, k_ref, v_ref, g_ref, sb_ref, dmat_ref, qf_ref, qb_ref, kf_ref, cdf_ref, nw_ref,
                    o_ref, s_scr, defer=False):
    c = pl.program_id(1)

    @pl.when(c == 0)
    def _():
        s_scr[...] = jnp.zeros_like(s_scr)

    q = q_ref[...]
    k = k_ref[...]
    v = v_ref[...]
    qf32 = q.astype(F32)
    q_fwd = (qf32 * qf_ref[...]).astype(BF16)
    q_bwd = (qf32 * qb_ref[...]).astype(BF16)
    k_end = (k.astype(F32) * kf_ref[...]).astype(BF16)
    g = g_ref[...].astype(F32)

    def head(h):
        hs = _hs(h)
        s = _dot_nt(q[:, hs], k[:, hs]) * dmat_ref[h]
        o = _dot(s.astype(BF16), v[:, hs])
        o = o + _dot(q_fwd[:, hs], s_scr[h].astype(BF16))
        o = o + _dot(q_bwd[:, hs], sb_ref[0, h])
        s_scr[h] = s_scr[h] * cdf_ref[h:h + 1, :] + _dot_tn(k_end[:, hs], v[:, hs])
        y = o * lax.rsqrt(jnp.mean(o * o, -1, keepdims=True) + EPS) * nw_ref[h:h + 1, :]
        gh = g[:, hs]
        o_ref[:, hs] = (y * (gh * _sigmoid(gh))).astype(BF16)

    return _heads(head, defer)


def _retention(proj, B, L, ret_decay, ret_norm_w):
    n = L // CHUNK
    T = B * L
    lg = jax.nn.log_sigmoid(ret_decay.astype(F32))
    idx = jnp.arange(CHUNK, dtype=F32)
    diff = idx[:, None] - idx[None, :]
    dmat = jnp.where(diff >= 0, jnp.exp(jnp.maximum(diff, 0.0) * lg[0][:, None, None]),
                     jnp.exp(jnp.maximum(-diff, 0.0) * lg[1][:, None, None]))

    def lane_tab(e):
        return jnp.repeat(jnp.exp(e).T, HEAD_DIM, axis=1)

    qf_tab = lane_tab((idx + 1.0)[None, :] * lg[0][:, None])
    qb_tab = lane_tab((CHUNK - idx)[None, :] * lg[1][:, None])
    kf_tab = lane_tab((CHUNK - 1.0 - idx)[None, :] * lg[0][:, None])
    kb_tab = lane_tab(idx[None, :] * lg[1][:, None])
    cdf = jnp.broadcast_to(jnp.exp(CHUNK * lg[0])[:, None], (N_HEADS, HEAD_DIM))
    cdb = jnp.broadcast_to(jnp.exp(CHUNK * lg[1])[:, None], (N_HEADS, HEAD_DIM))

    def seg(s, rev=False):
        if rev:
            return pl.BlockSpec((CHUNK, SEG_W), lambda b, c: (b * n + n - 1 - c, s))
        return pl.BlockSpec((CHUNK, SEG_W), lambda b, c: (b * n + c, s))

    full = lambda shape: pl.BlockSpec(shape, lambda b, c: (0,) * len(shape))
    st_shape = (1, N_HEADS, HEAD_DIM, HEAD_DIM)

    state = dict(
        kernel=_ret_state_kernel,
        in_specs=[seg(SEG_RK, True), seg(SEG_RV, True), full((CHUNK, MIX_W)), full((N_HEADS, HEAD_DIM))],
        args=[proj, proj, kb_tab, cdb],
        out_specs=[pl.BlockSpec(st_shape, lambda b, c: (b * n + n - 1 - c, 0, 0, 0))],
        out_shape=[jax.ShapeDtypeStruct((B * n, N_HEADS, HEAD_DIM, HEAD_DIM), BF16)],
        scratch=[pltpu.VMEM((N_HEADS, HEAD_DIM, HEAD_DIM), F32)])

    def out(sb):
        return dict(
            kernel=_ret_out_kernel, per_head=True,
            in_specs=[seg(SEG_RQ), seg(SEG_RK), seg(SEG_RV), seg(SEG_RG),
                      pl.BlockSpec(st_shape, lambda b, c: (b * n + c, 0, 0, 0)),
                      full((N_HEADS, CHUNK, CHUNK)), full((CHUNK, MIX_W)), full((CHUNK, MIX_W)),
                      full((CHUNK, MIX_W)), full((N_HEADS, HEAD_DIM)), full((N_HEADS, HEAD_DIM))],
            args=[proj, proj, proj, proj, sb, dmat, qf_tab, qb_tab, kf_tab, cdf, ret_norm_w.astype(F32)],
            out_specs=[pl.BlockSpec((CHUNK, MIX_W), lambda b, c: (b * n + c, 0))],
            out_shape=[jax.ShapeDtypeStruct((T, MIX_W), BF16)],
            scratch=[pltpu.VMEM((N_HEADS, HEAD_DIM, HEAD_DIM), F32)])

    return state, out


def _na_index_tables(nqb):
    rows = NA_QROWS * nqb
    qi = np.arange(CHUNK)
    ki = np.arange(NA_KW)

    def one(qb):
        kb = NA_QROWS * int(np.clip(qb - 2, 0, nqb - NA_KBLKS))
        r = (NA_QROWS * qb + qi // GRID_W)[:, None]
        qc = (qi % GRID_W)[:, None]
        kr = (kb + ki // GRID_W)[None, :]
        kc = (ki % GRID_W)[None, :]
        rs = np.clip(r - WIN_ROWS // 2, 0, rows - WIN_ROWS)
        cs = np.clip(qc - WIN_COLS // 2, 0, GRID_W - WIN_COLS)
        valid = (kr >= rs) & (kr < rs + WIN_ROWS) & (kc >= cs) & (kc < cs + WIN_COLS)
        dr = np.clip(kr - r + WIN_ROWS - 1, 0, 2 * WIN_ROWS - 2)
        dc = np.clip(kc - qc + WIN_COLS - 1, 0, 2 * WIN_COLS - 2)
        return dr + 0 * dc, dc + 0 * dr, valid

    reps = [0, 1, 2, nqb - 2, nqb - 1]
    tabs = [one(qb) for qb in reps]
    for qb in range(2, nqb - 2):
        t = one(qb)
        assert all(np.array_equal(a, b) for a, b in zip(t, tabs[2]))
    return tuple(np.stack([t[i] for t in tabs]) for i in range(3))


def _na_kernel(q_ref, k0, k1, k2, k3, k4, v0, v1, v2, v3, v4, bias_ref, o_ref, defer=False):
    k_refs = (k0, k1, k2, k3, k4)
    v_refs = (v0, v1, v2, v3, v4)

    def head(h):
        hs = _hs(h)
        q = q_ref[:, hs]
        k_all = jnp.concatenate([r[:, hs] for r in k_refs], 0)
        v_all = jnp.concatenate([r[:, hs] for r in v_refs], 0)
        s = _dot_nt(q, k_all) + bias_ref[0, h]
        p = jnp.exp(s - jnp.max(s, -1, keepdims=True))
        o = _dot(p.astype(BF16), v_all)
        o_ref[:, hs] = (o / jnp.sum(p, -1, keepdims=True)).astype(BF16)

    return _heads(head, defer)


def _neighborhood(proj, B, L, rpb):
    nqb = L // CHUNK
    assert nqb >= NA_KBLKS and L % (GRID_W * NA_QROWS) == 0
    T = B * L
    dr, dc, valid = _na_index_tables(nqb)
    n_cls = dr.shape[0]
    kr_n = NA_KW // GRID_W
    dr6 = dr.reshape(n_cls, NA_QROWS, GRID_W, kr_n, GRID_W)
    dc6 = dc.reshape(n_cls, NA_QROWS, GRID_W, kr_n, GRID_W)
    dr_s = dr6[:, :, 0, :, 0]
    dc_s = dc6[0, 0, :, 0, :]
    assert np.array_equal(dr6, np.broadcast_to(dr_s[:, :, None, :, None], dr6.shape))
    assert np.array_equal(dc6, np.broadcast_to(dc_s[None, None, :, None, :], dc6.shape))
    sel_r = (dr_s[..., None] == np.arange(2 * WIN_ROWS - 1)).astype(np.float32)
    sel_c = (np.arange(2 * WIN_COLS - 1)[:, None, None] == dc_s[None]).astype(np.float32)
    hp = lax.Precision.HIGHEST
    t = jnp.einsum('cajr,hrd->cajhd', sel_r, rpb.astype(F32), precision=hp)
    bias = jnp.einsum('cajhd,dqk->chaqjk', t, sel_c, precision=hp).reshape(n_cls, N_HEADS, CHUNK, NA_KW)
    bias = jnp.where(jnp.asarray(valid)[:, None], bias, NEG)

    def kspec(s, j):
        return pl.BlockSpec((CHUNK, SEG_W),
                            lambda b, qb: (b * nqb + jnp.clip(qb - 2, 0, nqb - NA_KBLKS) + j, s))

    def cls(qb):
        return jnp.where(qb < 2, qb, jnp.where(qb >= nqb - 2, qb - (nqb - 5), 2))

    return dict(
        kernel=_na_kernel, per_head=True,
        in_specs=[pl.BlockSpec((CHUNK, SEG_W), lambda b, qb: (b * nqb + qb, SEG_NQ))]
        + [kspec(SEG_NK, j) for j in range(NA_KBLKS)]
        + [kspec(SEG_NV, j) for j in range(NA_KBLKS)]
        + [pl.BlockSpec((1, N_HEADS, CHUNK, NA_KW), lambda b, qb: (cls(qb), 0, 0, 0))],
        args=[proj] * (1 + 2 * NA_KBLKS) + [bias],
        out_specs=[pl.BlockSpec((CHUNK, MIX_W), lambda b, qb: (b * nqb + qb, 0))],
        out_shape=[jax.ShapeDtypeStruct((T, MIX_W), BF16)],
        scratch=[])


HALO = 16


def _log_sigmoid(x):
    return jnp.minimum(x, 0.0) - jnp.log1p(jnp.exp(-jnp.abs(x)))


def _conv_silu(x_ref, prev_ref, next_ref, w_ref, col0, c, n):
    x = x_ref[...].astype(F32)
    row = lax.broadcasted_iota(jnp.int32, x.shape, 0)
    prev_row = prev_ref[HALO - 1:HALO, :].astype(F32) * (c > 0).astype(F32)
    next_row = next_ref[0:1, :].astype(F32) * (c < n - 1).astype(F32)
    x_prev = jnp.where(row == 0, prev_row, pltpu.roll(x, 1, 0))
    x_next = jnp.where(row == CHUNK - 1, next_row, pltpu.roll(x, CHUNK - 1, 0))
    cs = slice(col0, col0 + MIX_W)
    y = x_prev * w_ref[0:1, cs] + x * w_ref[1:2, cs] + x_next * w_ref[2:3, cs]
    return y * _sigmoid(y)


def _tri():
    r = lax.broadcasted_iota(jnp.int32, (CHUNK, CHUNK), 0)
    c = lax.broadcasted_iota(jnp.int32, (CHUNK, CHUNK), 1)
    return (c <= r).astype(F32), (c >= r).astype(F32)


def _hp_dot(a, b):
    return jnp.dot(a, b, preferred_element_type=F32, precision=lax.Precision.HIGHEST)


FWD_LANE = 2 * N_HEADS
BWD_LANE = 3 * N_HEADS


def _gate_dense(gc_ref, gr_ref, bc_ref, br_ref):
    low, up = _tri()
    g_col = gc_ref[...] + bc_ref[...]
    lf_col = _log_sigmoid(g_col)
    lf_row = _log_sigmoid(gr_ref[...] + br_ref[...])
    b_col = _hp_dot(low, lf_col)
    b_row = _hp_dot(lf_row, up)
    tot_row = b_col[CHUNK - 1:CHUNK, :]
    tot_col = b_row[:, CHUNK - 1:CHUNK]
    lane = lax.broadcasted_iota(jnp.int32, (CHUNK, LANES), 1)
    row = lax.broadcasted_iota(jnp.int32, (N_GATE_COLS, CHUNK), 0)
    bb_col = jnp.where(lane >= BWD_LANE, tot_row - b_col + lf_col, b_col)
    bb_row = jnp.where(row >= BWD_LANE, tot_col - b_row + lf_row, b_row)
    ck = pltpu.roll(g_col, FWD_LANE, 1) - bb_col
    a = tot_row + ck
    m_loc = jnp.max(a, 0, keepdims=True)
    ea = jnp.exp(a - m_loc)
    return bb_row, ck, tot_row, m_loc, ea


def _lane_bcast(x, l):
    return jnp.broadcast_to(x[:, l:l + 1], (x.shape[0], HEAD_DIM))


def _ml_state_update(kc_h, ct_loc_fn, ea, tot_row, m_loc, l, c_scr, n_scr, m_scr, h):
    m_p = m_scr[h:h + 1, 0:1]
    g_tot = tot_row[:, l:l + 1]
    m_l = m_loc[:, l:l + 1]
    kw = kc_h * _lane_bcast(ea, l)
    ct_loc = ct_loc_fn(kw.astype(BF16))
    n_loc = jnp.sum(kw, 0, keepdims=True)
    m_new = jnp.maximum(g_tot + m_p, m_l)
    sp = jnp.exp(g_tot + m_p - m_new)
    sl = jnp.exp(m_l - m_new)
    c_scr[h] = sp * c_scr[h] + sl * ct_loc
    n_scr[h:h + 1, :] = sp * n_scr[h:h + 1, :] + sl * n_loc
    m_scr[h:h + 1, :] = jnp.broadcast_to(m_new, (1, HEAD_DIM))


def _ml_init(c, c_scr, n_scr, m_scr):
    @pl.when(c == 0)
    def _():
        c_scr[...] = jnp.zeros_like(c_scr)
        n_scr[...] = jnp.zeros_like(n_scr)
        m_scr[...] = jnp.full_like(m_scr, NEG)


def _ml_state_kernel(k_ref, kp_ref, kn_ref, v_ref, gc_ref, gr_ref, cw_ref, bc_ref, br_ref,
                     cb_ref, nm_ref, c_scr, n_scr, m_scr, *, n):
    step = pl.program_id(1)
    _ml_init(step, c_scr, n_scr, m_scr)
    c = n - 1 - step
    cb_ref[0] = c_scr[...].astype(BF16)
    nm_ref[0, 0:N_HEADS, :] = n_scr[...]
    nm_ref[0, N_HEADS:2 * N_HEADS, :] = m_scr[...]

    kc = _conv_silu(k_ref, kp_ref, kn_ref, cw_ref, MIX_W, c, n) * (HEAD_DIM ** -0.5)
    v = v_ref[...]
    _, _, tot_row, m_loc, ea = _gate_dense(gc_ref, gr_ref, bc_ref, br_ref)
    for h in range(N_HEADS):
        v_h = v[:, _hs(h)]
        _ml_state_update(kc[:, _hs(h)], lambda kw: _dot_tn(v_h, kw), ea, tot_row, m_loc, BWD_LANE + h,
                         c_scr, n_scr, m_scr, h)


def _ml_direction(st, mask, b_q, ck_b, m_p, qn, vt_bf, ct_bf, q_bf):
    dmat = jnp.where(mask, b_q + ck_b, NEG)
    inter = b_q + m_p
    m_row = jnp.maximum(jnp.max(dmat, 0, keepdims=True), inter)
    s = st * jnp.exp(dmat - m_row)
    e_int = jnp.exp(inter - m_row)
    num = _dot(vt_bf, s.astype(BF16)) + e_int * _dot_nt(ct_bf, q_bf)
    den = jnp.sum(s, 0, keepdims=True) + e_int * qn
    return num / jnp.maximum(jnp.abs(den), jnp.exp(-m_row))


def _ml_out_kernel(q_ref, qp_ref, qn_ref, k_ref, kp_ref, kn_ref, v_ref, o_ref_in, gc_ref, gr_ref,
                   cb_ref, nm_ref, cw_ref, bc_ref, br_ref, nwt_ref,
                   out_ref, c_scr, n_scr, m_scr, *, n, defer=False):
    c = pl.program_id(1)
    _ml_init(c, c_scr, n_scr, m_scr)
    qc = _conv_silu(q_ref, qp_ref, qn_ref, cw_ref, 0, c, n)
    kc = _conv_silu(k_ref, kp_ref, kn_ref, cw_ref, MIX_W, c, n) * (HEAD_DIM ** -0.5)
    q_bf = qc.astype(BF16)
    k_bf = kc.astype(BF16)
    v = v_ref[...]
    og = o_ref_in[...].astype(F32)
    bb_row, ck, tot_row, m_loc, ea = _gate_dense(gc_ref, gr_ref, bc_ref, br_ref)
    key = lax.broadcasted_iota(jnp.int32, (CHUNK, CHUNK), 0)
    qry = lax.broadcasted_iota(jnp.int32, (CHUNK, CHUNK), 1)
    causal = key <= qry
    anti = key >= qry
    pad = jnp.zeros((8 - 2, HEAD_DIM), F32)

    def head(h):
        hs = _hs(h)
        lf, lb = FWD_LANE + h, BWD_LANE + h
        qh = q_bf[:, hs]
        vt_bf = v[:, hs].astype(F32).T.astype(BF16)
        st = _dot_nt(k_bf[:, hs], qh)
        n_prev = jnp.concatenate([n_scr[h:h + 1, :], nm_ref[0, h:h + 1, :], pad], 0).astype(BF16)
        qn = _dot_nt(n_prev, qh)
        h_f = _ml_direction(st, causal, bb_row[lf:lf + 1, :], _lane_bcast(ck, lf), m_scr[h:h + 1, 0:1],
                            qn[0:1, :], vt_bf, c_scr[h].astype(BF16), qh)
        h_b = _ml_direction(st, anti, bb_row[lb:lb + 1, :], _lane_bcast(ck, lb),
                            nm_ref[0, N_HEADS + h:N_HEADS + h + 1, 0:1], qn[1:2, :], vt_bf, cb_ref[0, h], qh)
        _ml_state_update(kc[:, hs], lambda kw: _dot(vt_bf, kw), ea, tot_row, m_loc, lf, c_scr, n_scr, m_scr, h)
        ot = h_f + h_b
        yt = ot * lax.rsqrt(jnp.mean(ot * ot, 0, keepdims=True) + EPS) * nwt_ref[h]
        out_ref[:, hs] = (yt.T * _sigmoid(og[:, hs])).astype(BF16)

    return _heads(head, defer)


def _mlstm(proj, gates, gates_t, B, L, conv_w, ig_b, fg_b, norm_w):
    n = L // CHUNK
    T = B * L
    hb = CHUNK // HALO
    bias = jnp.concatenate([ig_b.astype(F32).reshape(-1), fg_b.astype(F32).reshape(-1)])
    bias_col = jnp.zeros((1, LANES), F32).at[0, :N_GATE_COLS].set(bias)
    bias_row = jnp.broadcast_to(bias[:, None], (N_GATE_COLS, CHUNK))
    cw = conv_w.astype(F32)
    norm_wt = jnp.broadcast_to(norm_w.astype(F32)[:, :, None], (N_HEADS, HEAD_DIM, CHUNK))

    def seg(s, rev):
        if rev:
            return pl.BlockSpec((CHUNK, SEG_W), lambda b, c: (b * n + n - 1 - c, s))
        return pl.BlockSpec((CHUNK, SEG_W), lambda b, c: (b * n + c, s))

    def chunk_of(c, rev):
        return n - 1 - c if rev else c

    def prev_spec(s, rev):
        return pl.BlockSpec((HALO, SEG_W),
                            lambda b, c: (b * n * hb + jnp.maximum(chunk_of(c, rev) * hb - 1, 0), s))

    def next_spec(s, rev):
        return pl.BlockSpec((HALO, SEG_W),
                            lambda b, c: (b * n * hb + jnp.minimum((chunk_of(c, rev) + 1) * hb, n * hb - 1), s))

    def gcol_spec(rev):
        return pl.BlockSpec((CHUNK, LANES), lambda b, c: (b * n + chunk_of(c, rev), 0))

    def grow_spec(rev):
        return pl.BlockSpec((N_GATE_COLS, CHUNK), lambda b, c: (0, b * n + chunk_of(c, rev)))

    full = lambda shape: pl.BlockSpec(shape, lambda b, c: (0,) * len(shape))
    st_shape = (1, N_HEADS, HEAD_DIM, HEAD_DIM)
    nm_shape = (1, 2 * N_HEADS, HEAD_DIM)
    scratch = [pltpu.VMEM((N_HEADS, HEAD_DIM, HEAD_DIM), F32), pltpu.VMEM((N_HEADS, HEAD_DIM), F32),
               pltpu.VMEM((N_HEADS, HEAD_DIM), F32)]
    consts = [full((ML_CONV_W, 2 * MIX_W)), full((1, LANES)), full((N_GATE_COLS, CHUNK))]

    state = dict(
        kernel=functools.partial(_ml_state_kernel, n=n),
        in_specs=[seg(SEG_MK, True), prev_spec(SEG_MK, True), next_spec(SEG_MK, True), seg(SEG_MV, True),
                  gcol_spec(True), grow_spec(True)] + consts,
        args=[proj, proj, proj, proj, gates, gates_t, cw, bias_col, bias_row],
        out_specs=[pl.BlockSpec(st_shape, lambda b, c: (b * n + n - 1 - c, 0, 0, 0)),
                   pl.BlockSpec(nm_shape, lambda b, c: (b * n + n - 1 - c, 0, 0))],
        out_shape=[jax.ShapeDtypeStruct((B * n, N_HEADS, HEAD_DIM, HEAD_DIM), BF16),
                   jax.ShapeDtypeStruct((B * n, 2 * N_HEADS, HEAD_DIM), F32)],
        scratch=scratch)

    def out(cb, nm):
        return dict(
            kernel=functools.partial(_ml_out_kernel, n=n), per_head=True,
            in_specs=[seg(SEG_MQ, False), prev_spec(SEG_MQ, False), next_spec(SEG_MQ, False),
                      seg(SEG_MK, False), prev_spec(SEG_MK, False), next_spec(SEG_MK, False),
                      seg(SEG_MV, False), seg(SEG_MO, False), gcol_spec(False), grow_spec(False),
                      pl.BlockSpec(st_shape, lambda b, c: (b * n + c, 0, 0, 0)),
                      pl.BlockSpec(nm_shape, lambda b, c: (b * n + c, 0, 0))]
            + consts + [full((N_HEADS, HEAD_DIM, CHUNK))],
            args=[proj, proj, proj, proj, proj, proj, proj, proj, gates, gates_t, cb, nm,
                  cw, bias_col, bias_row, norm_wt],
            out_specs=[pl.BlockSpec((CHUNK, MIX_W), lambda b, c: (b * n + c, 0))],
            out_shape=[jax.ShapeDtypeStruct((T, MIX_W), BF16)],
            scratch=scratch)

    return state, out


def _fused_call(parts, grid, name):
    n_in = [len(p["in_specs"]) for p in parts]
    n_out = [len(p["out_specs"]) for p in parts]
    n_scr = [len(p["scratch"]) for p in parts]

    def body(*refs):
        ins, outs, scr = refs[:sum(n_in)], refs[sum(n_in):sum(n_in) + sum(n_out)], refs[sum(n_in) + sum(n_out):]
        i = o = k = 0
        heads = []
        for p, a, b, c in zip(parts, n_in, n_out, n_scr):
            args = (*ins[i:i + a], *outs[o:o + b], *scr[k:k + c])
            if p.get("per_head"):
                heads.append(p["kernel"](*args, defer=True))
            else:
                p["kernel"](*args)
            i, o, k = i + a, o + b, k + c
        for h in range(N_HEADS):
            for head in heads:
                head(h)

    return pl.pallas_call(
        body,
        grid=grid,
        in_specs=[s for p in parts for s in p["in_specs"]],
        out_specs=[s for p in parts for s in p["out_specs"]],
        out_shape=[s for p in parts for s in p["out_shape"]],
        scratch_shapes=[s for p in parts for s in p["scratch"]],
        compiler_params=_cparams(("arbitrary", "arbitrary")),
        name=name,
    )(*[a for p in parts for a in p["args"]])


def _mixers(proj, gates, gates_t, B, L, p):
    grid = (B, L // CHUNK)
    ret_state, ret_out = _retention(proj, B, L, p["ret_decay"], p["ret_norm_w"])
    ml_state, ml_out = _mlstm(proj, gates, gates_t, B, L, p["ml_conv"], p["ml_igate_b"], p["ml_fgate_b"],
                              p["ml_norm_w"])
    sb, cb, nm = _fused_call([ret_state, ml_state], grid, "mixer_states")
    na = _neighborhood(proj, B, L, p["na_rpb"])
    return _fused_call([ret_out(sb), na, ml_out(cb, nm)], grid, "mixer_outputs")


def _branch_kernel(oret_ref, ona_ref, oml_ref, ga_ref, gb_ref, gc_ref, wb_ref, bgb_ref, m_ref):
    o_refs = (oret_ref, ona_ref, oml_ref)
    g_refs = (ga_ref, gb_ref, gc_ref)
    for n in range(D_MODEL // SEG_W):
        cs = slice(n * SEG_W, (n + 1) * SEG_W)
        acc = None
        for i in range(N_BRANCH):
            z = g_refs[i][:, cs].astype(F32) + bgb_ref[i:i + 1, cs]
            term = _sigmoid(z) * _dot(o_refs[i][...], wb_ref[i, :, cs])
            acc = term if acc is None else acc + term
        m_ref[:, cs] = acc.astype(BF16)


def _outproj_router_kernel(x_ref, m_ref, wout_ref, n2w_ref, rwhl_ref, rb_ref,
                           h_ref, hn_ref, route_ref, cnt_ref, cnt_scr):
    h = x_ref[...] + _dot(m_ref[...], wout_ref[...])
    h_ref[...] = h
    hn = h * lax.rsqrt(jnp.mean(h * h, -1, keepdims=True) + EPS) * n2w_ref[...]
    hn_ref[...] = hn

    hn_hi = hn.astype(BF16)
    hn_lo = (hn - hn_hi.astype(F32)).astype(BF16)
    hi = _dot(hn_hi, rwhl_ref[...])
    logits = (hi[:, :LANES] + _dot(hn_lo, rwhl_ref[:, :LANES]) + hi[:, LANES:]) + rb_ref[...]
    lane_i = lax.broadcasted_iota(jnp.int32, logits.shape, 1)
    lane = lane_i.astype(F32)
    lane_grp = jnp.right_shift(lane_i - N_GROUPS, 3).astype(F32)
    gmask = lane_i < N_GROUPS
    lg = jnp.where(gmask, logits, NEG)
    mg = jnp.max(lg, -1, keepdims=True)
    p_top = 1.0 / jnp.sum(jnp.where(gmask, jnp.exp(lg - mg), 0.0), -1, keepdims=True)
    grp = jnp.min(jnp.where(lg == mg, lane, float(LANES)), -1, keepdims=True)
    emask = (lane_i >= N_GROUPS) & (lane_i < N_GROUPS + N_EXPERTS) & (lane_grp == grp)
    le = jnp.where(emask, logits, NEG)
    m1 = jnp.max(le, -1, keepdims=True)
    i1 = jnp.min(jnp.where(le == m1, lane, float(LANES)), -1, keepdims=True)
    le2 = jnp.where(lane == i1, NEG, le)
    m2 = jnp.max(le2, -1, keepdims=True)
    i2 = jnp.min(jnp.where(le2 == m2, lane, float(LANES)), -1, keepdims=True)
    e2 = jnp.exp(m2 - m1)
    gate1 = p_top / (1.0 + e2)
    gate2 = p_top * e2 / (1.0 + e2)

    @pl.when(pl.program_id(0) == 0)
    def _():
        cnt_scr[...] = jnp.zeros_like(cnt_scr)

    tm = logits.shape[0]
    hit1 = lane == i1
    hit2 = lane == i2
    onehot = jnp.where(hit1 | hit2, 1.0, 0.0)
    r_i = lax.broadcasted_iota(jnp.int32, (tm, tm), 0)
    c_i = lax.broadcasted_iota(jnp.int32, (tm, tm), 1)
    before = jnp.where(c_i < r_i, 1.0, 0.0).astype(BF16)
    prior = _dot(before, onehot.astype(BF16)) + cnt_scr[...]
    rank1 = jnp.sum(jnp.where(hit1, prior, 0.0), -1, keepdims=True)
    rank2 = jnp.sum(jnp.where(hit2, prior, 0.0), -1, keepdims=True)
    cnt = cnt_scr[...] + jnp.sum(onehot, 0, keepdims=True)
    cnt_scr[...] = cnt
    cnt_ref[...] = jnp.broadcast_to(cnt, cnt_ref.shape)

    cols = (i1 - N_GROUPS, i2 - N_GROUPS, gate1, gate2, rank1, rank2)
    route = jnp.zeros_like(logits)
    for ci, val in enumerate(cols):
        route = jnp.where(lane_i == ci, val, route)
    route_ref[...] = route


MERGE_ROWS = 512


def _merge(x2d, o_ret, o_na, o_ml, proj, wb, bgb, wout, n2w, rw_hi, rw_lo, rb):
    T = x2d.shape[0]
    tm = MERGE_ROWS
    full = lambda shape: pl.BlockSpec(shape, lambda i: (0,) * len(shape), pipeline_mode=pl.Buffered(1))
    row = lambda w: pl.BlockSpec((tm, w), lambda i: (i, 0))
    gate = lambda s: pl.BlockSpec((tm, D_MODEL), lambda i: (i, s // 2))
    merged = pl.pallas_call(
        _branch_kernel,
        grid=(T // tm,),
        in_specs=[row(MIX_W), row(MIX_W), row(MIX_W), gate(SEG_GA), gate(SEG_GB), gate(SEG_GC),
                  full((N_BRANCH, MIX_W, D_MODEL)), full((N_BRANCH, D_MODEL))],
        out_specs=row(D_MODEL),
        out_shape=jax.ShapeDtypeStruct((T, D_MODEL), BF16),
        compiler_params=_cparams(("arbitrary",)),
        name="branch_merge",
    )(o_ret, o_na, o_ml, proj, proj, proj, wb, bgb)
    return pl.pallas_call(
        _outproj_router_kernel,
        grid=(T // tm,),
        in_specs=[row(D_MODEL), row(D_MODEL), full((D_MODEL, D_MODEL)), full((1, D_MODEL)),
                  full((D_MODEL, 2 * LANES)), full((1, LANES))],
        out_specs=[row(D_MODEL), row(D_MODEL), row(LANES), pl.BlockSpec((8, LANES), lambda i: (0, 0))],
        out_shape=[jax.ShapeDtypeStruct((T, D_MODEL), F32), jax.ShapeDtypeStruct((T, D_MODEL), F32),
                   jax.ShapeDtypeStruct((T, LANES), F32), jax.ShapeDtypeStruct((8, LANES), F32)],
        scratch_shapes=[pltpu.VMEM((1, LANES), F32)],
        compiler_params=_cparams(("arbitrary",)),
        name="outproj_router",
    )(x2d, merged, wout, n2w, jnp.concatenate([rw_hi, rw_lo], 1), rb)


def _row_copy(src_hbm, row, dst, r, sem):
    return pltpu.make_async_copy(src_hbm.at[pl.ds(row, 1), :], dst.at[pl.ds(r, 1), :], sem)


def _ffn_kernel(blk_e_ref, n_used_ref, src_ref, hn_hbm, wg_ref, wu_ref, wd_ref, y_ref, xbuf, sem):
    i = pl.program_id(0)
    n_used = n_used_ref[0]
    slot = i % 2

    def wait_block(s):
        pltpu.make_async_copy(hn_hbm.at[pl.ds(0, MOE_ROWS), :], xbuf.at[s], sem.at[s]).wait()

    @pl.when(i == 0)
    def _():
        def body(r, carry):
            _row_copy(hn_hbm, src_ref[r], xbuf.at[0], r, sem.at[0]).start()
            return carry
        lax.fori_loop(0, MOE_ROWS, body, 0, unroll=8)

    @pl.when(i < n_used)
    def _():
        for r in range(MOE_ROWS):
            _row_copy(hn_hbm, src_ref[(i + 1) * MOE_ROWS + r], xbuf.at[1 - slot], r, sem.at[1 - slot]).start()
        wait_block(slot)
        x = xbuf[slot].astype(BF16)
        a = _dot(x, wg_ref[0])
        hid = (a * _sigmoid(a)) * _dot(x, wu_ref[0])
        y_ref[...] = _dot(hid.astype(BF16), wd_ref[0])

    @pl.when(i >= n_used)
    def _():
        y_ref[...] = jnp.zeros_like(y_ref)

    @pl.when(i == n_used)
    def _():
        wait_block(slot)


def _expert_ffn(hn, src, blk_e, n_used, wg, wu, wd):
    n_blocks = src.shape[0] // MOE_ROWS
    grid_spec = pltpu.PrefetchScalarGridSpec(
        num_scalar_prefetch=3,
        grid=(n_blocks,),
        in_specs=[
            pl.BlockSpec(memory_space=pl.ANY),
            pl.BlockSpec((1, D_MODEL, D_EXPERT), lambda i, be, nu, sr: (be[i], 0, 0)),
            pl.BlockSpec((1, D_MODEL, D_EXPERT), lambda i, be, nu, sr: (be[i], 0, 0)),
            pl.BlockSpec((1, D_EXPERT, D_MODEL), lambda i, be, nu, sr: (be[i], 0, 0)),
        ],
        out_specs=pl.BlockSpec((MOE_ROWS, D_MODEL), lambda i, be, nu, sr: (i, 0)),
        scratch_shapes=[pltpu.VMEM((2, MOE_ROWS, D_MODEL), F32), pltpu.SemaphoreType.DMA((2,))],
    )
    return pl.pallas_call(
        _ffn_kernel,
        grid_spec=grid_spec,
        out_shape=jax.ShapeDtypeStruct((n_blocks * MOE_ROWS, D_MODEL), F32),
        compiler_params=_cparams(("arbitrary",), disable_bounds_checks=True),
        name="expert_ffn",
    )(blk_e, n_used, src, hn, wg, wu, wd)


COMBINE_ROWS = 256


def _combine_kernel(dest_ref, h_ref, route_ref, yb_hbm, out_ref, ybuf, sem):
    i = pl.program_id(0)
    slot = i % 2

    def start_row(tile, r, to_slot):
        a = (tile * COMBINE_ROWS + r) * TOP_K_INNER
        for k in range(TOP_K_INNER):
            _row_copy(yb_hbm, dest_ref[a + k], ybuf.at[to_slot, k], r, sem.at[to_slot]).start()

    @pl.when(i == 0)
    def _():
        def body(r, carry):
            start_row(0, r, 0)
            return carry
        lax.fori_loop(0, COMBINE_ROWS, body, 0, unroll=8)

    @pl.when(i + 1 < pl.num_programs(0))
    def _():
        for r in range(COMBINE_ROWS):
            start_row(i + 1, r, 1 - slot)

    for k in range(TOP_K_INNER):
        pltpu.make_async_copy(yb_hbm.at[pl.ds(0, COMBINE_ROWS), :], ybuf.at[slot, k], sem.at[slot]).wait()

    route = route_ref[...]
    out_ref[...] = h_ref[...] + route[:, 2:3] * ybuf[slot, 0] + route[:, 3:4] * ybuf[slot, 1]


def _combine(h, route, yb, dest):
    T = h.shape[0]
    grid_spec = pltpu.PrefetchScalarGridSpec(
        num_scalar_prefetch=1,
        grid=(T // COMBINE_ROWS,),
        in_specs=[
            pl.BlockSpec((COMBINE_ROWS, D_MODEL), lambda i, d: (i, 0)),
            pl.BlockSpec((COMBINE_ROWS, LANES), lambda i, d: (i, 0)),
            pl.BlockSpec(memory_space=pl.ANY),
        ],
        out_specs=pl.BlockSpec((COMBINE_ROWS, D_MODEL), lambda i, d: (i, 0)),
        scratch_shapes=[pltpu.VMEM((2, TOP_K_INNER, COMBINE_ROWS, D_MODEL), F32), pltpu.SemaphoreType.DMA((2,))],
    )
    return pl.pallas_call(
        _combine_kernel,
        grid_spec=grid_spec,
        out_shape=jax.ShapeDtypeStruct((T, D_MODEL), F32),
        compiler_params=_cparams(("arbitrary",), disable_bounds_checks=True),
        name="moe_combine",
    )(dest, h, route, yb)


def _moe(h, hn, route, counts, wg, wu, wd, layer):
    T = h.shape[0]
    n_assign = T * TOP_K_INNER
    expert = route[:, 0:2].astype(jnp.int32)
    rank = route[:, 4:6].astype(jnp.int32)
    cnt = counts[0, N_GROUPS:N_GROUPS + N_EXPERTS].astype(jnp.int32)
    padded = (cnt + MOE_ROWS - 1) // MOE_ROWS * MOE_ROWS
    pend = jnp.cumsum(padded)
    pstart = pend - padded
    dest = (pstart[expert] + rank).reshape(-1)
    n_blocks = (n_assign + N_EXPERTS * (MOE_ROWS - 1) + MOE_ROWS - 1) // MOE_ROWS + 1
    blk_start = jnp.arange(n_blocks, dtype=jnp.int32) * MOE_ROWS
    blk_e = jnp.minimum(jnp.sum((pend[None, :] <= blk_start[:, None]).astype(jnp.int32), 1), N_EXPERTS - 1)
    blk_e = blk_e + layer * N_EXPERTS
    n_used = (pend[-1] // MOE_ROWS).astype(jnp.int32).reshape(1)
    flat_t = jnp.repeat(jnp.arange(T, dtype=jnp.int32), TOP_K_INNER)
    src = jnp.zeros((n_blocks * MOE_ROWS,), jnp.int32).at[dest].set(flat_t)
    yb = _expert_ffn(hn, src, blk_e, n_used, wg, wu, wd)
    return _combine(h, route, yb, dest)


def _rope_tables(L):
    freqs = ROPE_BASE ** (-jnp.arange(0, HEAD_DIM, 2, dtype=F32) / HEAD_DIM)
    ang = jnp.arange(L, dtype=F32)[:, None] * freqs[None]
    cos, sin = jnp.cos(ang), jnp.sin(ang)
    return jnp.concatenate([cos, cos], -1), jnp.concatenate([-sin, sin], -1)


N_MIX_SEG = 11


REORDER_COLS = SEG_W // 2
N_BRANCH_BLK = N_BRANCH * D_MODEL // REORDER_COLS


def _reorder_kernel(wm_ref, wg_ref, o_ref):
    j = pl.program_id(1)

    @pl.when(j < N_BRANCH_BLK)
    def _():
        o_ref[...] = wg_ref[0].T.astype(BF16)

    @pl.when(j >= N_BRANCH_BLK)
    def _():
        o_ref[...] = wm_ref[...].T.astype(BF16)


def _reorder_w_in(w_in_t):
    depth = w_in_t.shape[0]
    gate_row0 = N_MIX_SEG * SEG_W + N_GATE_COLS
    blk = (None, REORDER_COLS, D_MODEL)
    return pl.pallas_call(
        _reorder_kernel,
        grid=(depth, N_SEG * SEG_W // REORDER_COLS),
        in_specs=[pl.BlockSpec(blk, lambda l, j: (l, jnp.maximum(j - N_BRANCH_BLK, 0), 0)),
                  pl.BlockSpec((pl.Element(1), pl.Element(REORDER_COLS), pl.Element(D_MODEL)),
                               lambda l, j: (l, pl.multiple_of(
                                   gate_row0 + jnp.minimum(j, N_BRANCH_BLK - 1) * REORDER_COLS, 32), 0))],
        out_specs=pl.BlockSpec((None, D_MODEL, REORDER_COLS), lambda l, j: (l, 0, j)),
        out_shape=jax.ShapeDtypeStruct((depth, D_MODEL, N_SEG * SEG_W), BF16),
        compiler_params=_cparams(("arbitrary", "arbitrary")),
        name="reorder_w_in",
    )(w_in_t, w_in_t)


def _cast_kernel(w_ref, o_ref):
    o_ref[...] = w_ref[...].astype(BF16)


def _cast_experts(w):
    depth, e, r, c = w.shape
    spec = pl.BlockSpec((1, r, c), lambda i: (i, 0, 0))
    return pl.pallas_call(
        _cast_kernel,
        grid=(depth * e,),
        in_specs=[spec],
        out_specs=spec,
        out_shape=jax.ShapeDtypeStruct((depth * e, r, c), BF16),
        compiler_params=_cparams(("arbitrary",)),
        name="cast_experts",
    )(w.reshape(depth * e, r, c))


def _prep_layer(p):
    wift = p["wif_t"]
    wif_pad = jnp.zeros((D_MODEL, LANES), F32).at[:, :N_GATE_COLS].set(wift.T)
    rw = jnp.zeros((D_MODEL, LANES), F32).at[:, :N_GROUPS].set(p["router_g_w"]) \
        .at[:, N_GROUPS:N_GROUPS + N_EXPERTS].set(p["router_e_w"])
    rb = jnp.zeros((1, LANES), F32).at[0, :N_GROUPS].set(p["router_g_b"]) \
        .at[0, N_GROUPS:N_GROUPS + N_EXPERTS].set(p["router_e_b"])
    return dict(
        nw=p["norm1_w"].astype(F32).reshape(1, D_MODEL), wif=wif_pad, wift=wift,
        qn=p["na_q_norm"].astype(F32).reshape(1, HEAD_DIM), kn=p["na_k_norm"].astype(F32).reshape(1, HEAD_DIM),
        wb=p["w_branch"].astype(BF16), bgb=p["branch_gate_b"].astype(F32), wout=p["w_out"].astype(BF16),
        n2w=p["norm2_w"].astype(F32).reshape(1, D_MODEL), rb=rb,
        rw_hi=rw.astype(BF16), rw_lo=(rw - rw.astype(BF16).astype(F32)).astype(BF16),
    )


def _layer(x2d, B, L, p, q, rope, w_main, experts, layer):
    proj, gates, gates_t = _inproj(x2d, L, q["nw"], w_main, layer, q["wif"], q["wift"], rope[0], rope[1],
                                   q["qn"], q["kn"])
    o_ret, o_na, o_ml = _mixers(proj, gates, gates_t, B, L, p)
    h, hn, route, counts = _merge(x2d, o_ret, o_na, o_ml, proj, q["wb"], q["bgb"], q["wout"], q["n2w"],
                                  q["rw_hi"], q["rw_lo"], q["rb"])
    return _moe(h, hn, route, counts, experts[0], experts[1], experts[2], layer)


_PARAM_NAMES = ("norm1_w", "w_in", "ret_decay", "ret_norm_w", "na_q_norm", "na_k_norm", "na_rpb", "ml_conv",
                "ml_igate_b", "ml_fgate_b", "ml_norm_w", "w_branch", "branch_gate_b", "w_out", "norm2_w",
                "router_g_w", "router_g_b", "router_e_w", "router_e_b", "exp_w_gate", "exp_w_up", "exp_w_down")


def _run(x, layers, preps, w_main, experts):
    B, L, D = x.shape
    rope = _rope_tables(L)
    x2d = x.reshape(B * L, D)
    for layer, (p, q) in enumerate(zip(layers, preps)):
        x2d = _layer(x2d, B, L, p, q, rope, w_main, experts, layer)
    return x2d.reshape(B, L, D)


def kernel(x_prompt, x_sample, norm1_w, w_in, ret_decay, ret_norm_w, na_q_norm, na_k_norm, na_rpb, ml_conv,
           ml_igate_b, ml_fgate_b, ml_norm_w, w_branch, branch_gate_b, w_out, norm2_w, router_g_w, router_g_b,
           router_e_w, router_e_b, exp_w_gate, exp_w_up, exp_w_down):
    stacked = (norm1_w, w_in, ret_decay, ret_norm_w, na_q_norm, na_k_norm, na_rpb, ml_conv, ml_igate_b,
               ml_fgate_b, ml_norm_w, w_branch, branch_gate_b, w_out, norm2_w, router_g_w, router_g_b,
               router_e_w, router_e_b, exp_w_gate, exp_w_up, exp_w_down)
    depth = w_in.shape[0]
    w_in_t = jnp.swapaxes(w_in, 1, 2)
    wif_t = w_in_t[:, N_MIX_SEG * SEG_W:N_MIX_SEG * SEG_W + N_GATE_COLS, :]
    layers = [dict(zip(_PARAM_NAMES, (None if a is w_in else a[l] for a in stacked)), wif_t=wif_t[l])
              for l in range(depth)]
    preps = [_prep_layer(p) for p in layers]
    experts = tuple(_cast_experts(w) for w in (exp_w_gate, exp_w_up, exp_w_down))
    w_main = _reorder_w_in(w_in_t)
    return (_run(x_prompt, layers, preps, w_main, experts), _run(x_sample, layers, preps, w_main, experts))
```

```python
import functools

import numpy as np
import jax
import jax.numpy as jnp
from jax import lax
from jax.experimental import pallas as pl
from jax.experimental.pallas import tpu as pltpu

D_MODEL = 2048
HEAD_DIM = 128
MIX_W = D_MODEL // 2
N_HEADS = MIX_W // HEAD_DIM
N_BRANCH = 3
CHUNK = 128
ROPE_BASE = 10000.0
GRID_W = 64
WIN_ROWS = 8
WIN_COLS = 16
ML_CONV_W = 3
N_GROUPS = 4
EXPERTS_PER_GROUP = 8
N_EXPERTS = N_GROUPS * EXPERTS_PER_GROUP
TOP_K_INNER = 2
D_EXPERT = D_MODEL // 2
EPS = 1e-6
NEG = -1e30

F32 = jnp.float32
BF16 = jnp.bfloat16

LANES = 128
SEG_W = MIX_W
SEG_GA, SEG_GB, SEG_GC = 0, 2, 4
SEG_RQ, SEG_RK, SEG_RV, SEG_RG = 6, 7, 8, 9
SEG_NQ, SEG_NK, SEG_NV = 10, 11, 12
SEG_MQ, SEG_MK, SEG_MV, SEG_MO = 13, 14, 15, 16
N_SEG = 17
N_GATE_COLS = 4 * N_HEADS

NA_QROWS = 2
NA_KBLKS = 5
NA_KW = NA_KBLKS * CHUNK
MOE_ROWS = 256
VMEM_LIMIT = 56 * 1024 * 1024


def _cparams(sem, **kw):
    return pltpu.CompilerParams(dimension_semantics=sem, vmem_limit_bytes=VMEM_LIMIT, **kw)


def _sigmoid(z):
    return 0.5 * jnp.tanh(0.5 * z) + 0.5


def _hs(h):
    return slice(h * HEAD_DIM, (h + 1) * HEAD_DIM)


def _heads(head, defer):
    if defer:
        return head
    for h in range(N_HEADS):
        head(h)


def _dot(a, b):
    return jnp.dot(a, b, preferred_element_type=F32)


def _dot_nt(a, b):
    return lax.dot_general(a, b, (((1,), (1,)), ((), ())), preferred_element_type=F32)


def _dot_tn(a, b):
    return lax.dot_general(a, b, (((0,), (0,)), ((), ())), preferred_element_type=F32)


def _inproj_kernel(x_ref, nw_ref, w_ref, wif_ref, wift_ref, cos_ref, sin_ref, qn_ref, kn_ref,
                   proj_ref, g_ref, gt_ref, xn_ref):
    j = pl.program_id(1)

    @pl.when(j == 0)
    def _():
        x = x_ref[...]
        y = x * lax.rsqrt(jnp.mean(x * x, -1, keepdims=True) + EPS) * nw_ref[...]
        xn = y.astype(BF16)
        xn_ref[...] = xn
        g_ref[...] = _dot(xn, wif_ref[...].astype(BF16))
        gt_ref[...] = _dot_nt(wift_ref[...].astype(BF16), xn)

    acc = _dot(xn_ref[...], w_ref[...])

    is_rope = (j == SEG_RQ) | (j == SEG_RK)
    is_norm = (j == SEG_NQ) | (j == SEG_NK)

    @pl.when(is_rope)
    def _():
        scale = jnp.where(j == SEG_RK, HEAD_DIM ** -0.5, 1.0).astype(F32)
        cos = cos_ref[...]
        sin = sin_ref[...]
        for h in range(N_HEADS):
            xh = acc[:, _hs(h)]
            r = xh * cos + pltpu.roll(xh, HEAD_DIM // 2, 1) * sin
            proj_ref[:, _hs(h)] = (r * scale).astype(BF16)

    @pl.when(is_norm)
    def _():
        w = jnp.where(j == SEG_NQ, qn_ref[...] * (HEAD_DIM ** -0.5), kn_ref[...])
        for h in range(N_HEADS):
            xh = acc[:, _hs(h)]
            y = xh * lax.rsqrt(jnp.mean(xh * xh, -1, keepdims=True) + EPS) * w
            proj_ref[:, _hs(h)] = y.astype(BF16)

    @pl.when(jnp.logical_not(is_rope | is_norm))
    def _():
        proj_ref[...] = acc.astype(BF16)


def _inproj(x2d, L, nw, w_main, layer, wif, wift, cos_t, sin_t, qn, kn):
    T = x2d.shape[0]
    tm = min(1024, L)
    nlt = L // tm
    return pl.pallas_call(
        _inproj_kernel,
        grid=(T // tm, N_SEG),
        in_specs=[
            pl.BlockSpec((tm, D_MODEL), lambda i, j: (i, 0)),
            pl.BlockSpec((1, D_MODEL), lambda i, j: (0, 0)),
            pl.BlockSpec((None, D_MODEL, SEG_W), lambda i, j: (layer, 0, j)),
            pl.BlockSpec((D_MODEL, LANES), lambda i, j: (0, 0)),
            pl.BlockSpec((N_GATE_COLS, D_MODEL), lambda i, j: (0, 0)),
            pl.BlockSpec((tm, HEAD_DIM), lambda i, j: (i % nlt, 0)),
            pl.BlockSpec((tm, HEAD_DIM), lambda i, j: (i % nlt, 0)),
            pl.BlockSpec((1, HEAD_DIM), lambda i, j: (0, 0)),
            pl.BlockSpec((1, HEAD_DIM), lambda i, j: (0, 0)),
        ],
        out_specs=[
            pl.BlockSpec((tm, SEG_W), lambda i, j: (i, j)),
            pl.BlockSpec((tm, LANES), lambda i, j: (i, 0)),
            pl.BlockSpec((N_GATE_COLS, tm), lambda i, j: (0, i)),
        ],
        out_shape=[
            jax.ShapeDtypeStruct((T, N_SEG * SEG_W), BF16),
            jax.ShapeDtypeStruct((T, LANES), F32),
            jax.ShapeDtypeStruct((N_GATE_COLS, T), F32),
        ],
        scratch_shapes=[pltpu.VMEM((tm, D_MODEL), BF16)],
        compiler_params=_cparams(("arbitrary", "arbitrary")),
        name="inproj",
    )(x2d, nw, w_main, wif, wift, cos_t, sin_t, qn, kn)


def _ret_state_kernel(k_ref, v_ref, kb_ref, cdb_ref, sb_ref, s_scr):
    c = pl.program_id(1)

    @pl.when(c == 0)
    def _():
        s_scr[...] = jnp.zeros_like(s_scr)

    sb_ref[0] = s_scr[...].astype(BF16)
    kk = (k_ref[...].astype(F32) * kb_ref[...]).astype(BF16)
    v = v_ref[...]
    for h in range(N_HEADS):
        kv = _dot_tn(kk[:, _hs(h)], v[:, _hs(h)])
        s_scr[h] = s_scr[h] * cdb_ref[h:h + 1, :] + kv


def _ret_out_kernel(q_ref, k_ref, v_ref, g_ref, sb_ref, dmat_ref, qf_ref, qb_ref, kf_ref, cdf_ref, nw_ref,
                    o_ref, s_scr, defer=False):
    c = pl.program_id(1)

    @pl.when(c == 0)
    def _():
        s_scr[...] = jnp.zeros_like(s_scr)

    q = q_ref[...]
    k = k_ref[...]
    v = v_ref[...]
    qf32 = q.astype(F32)
    q_fwd = (qf32 * qf_ref[...]).astype(BF16)
    q_bwd = (qf32 * qb_ref[...]).astype(BF16)
    k_end = (k.astype(F32) * kf_ref[...]).astype(BF16)
    g = g_ref[...].astype(F32)

    def head(h):
        hs = _hs(h)
        s = _dot_nt(q[:, hs], k[:, hs]) * dmat_ref[h]
        o = _dot(s.astype(BF16), v[:, hs])
        o = o + _dot(q_fwd[:, hs], s_scr[h].astype(BF16))
        o = o + _dot(q_bwd[:, hs], sb_ref[0, h])
        s_scr[h] = s_scr[h] * cdf_ref[h:h + 1, :] + _dot_tn(k_end[:, hs], v[:, hs])
        y = o * lax.rsqrt(jnp.mean(o * o, -1, keepdims=True) + EPS) * nw_ref[h:h + 1, :]
        gh = g[:, hs]
        o_ref[:, hs] = (y * (gh * _sigmoid(gh))).astype(BF16)

    return _heads(head, defer)


def _retention(proj, B, L, ret_decay, ret_norm_w):
    n = L // CHUNK
    T = B * L
    lg = jax.nn.log_sigmoid(ret_decay.astype(F32))
    idx = jnp.arange(CHUNK, dtype=F32)
    diff = idx[:, None] - idx[None, :]
    dmat = jnp.where(diff >= 0, jnp.exp(jnp.maximum(diff, 0.0) * lg[0][:, None, None]),
                     jnp.exp(jnp.maximum(-diff, 0.0) * lg[1][:, None, None]))

    def lane_tab(e):
        return jnp.repeat(jnp.exp(e).T, HEAD_DIM, axis=1)

    qf_tab = lane_tab((idx + 1.0)[None, :] * lg[0][:, None])
    qb_tab = lane_tab((CHUNK - idx)[None, :] * lg[1][:, None])
    kf_tab = lane_tab((CHUNK - 1.0 - idx)[None, :] * lg[0][:, None])
    kb_tab = lane_tab(idx[None, :] * lg[1][:, None])
    cdf = jnp.broadcast_to(jnp.exp(CHUNK * lg[0])[:, None], (N_HEADS, HEAD_DIM))
    cdb = jnp.broadcast_to(jnp.exp(CHUNK * lg[1])[:, None], (N_HEADS, HEAD_DIM))

    def seg(s, rev=False):
        if rev:
            return pl.BlockSpec((CHUNK, SEG_W), lambda b, c: (b * n + n - 1 - c, s))
        return pl.BlockSpec((CHUNK, SEG_W), lambda b, c: (b * n + c, s))

    full = lambda shape: pl.BlockSpec(shape, lambda b, c: (0,) * len(shape))
    st_shape = (1, N_HEADS, HEAD_DIM, HEAD_DIM)

    state = dict(
        kernel=_ret_state_kernel,
        in_specs=[seg(SEG_RK, True), seg(SEG_RV, True), full((CHUNK, MIX_W)), full((N_HEADS, HEAD_DIM))],
        args=[proj, proj, kb_tab, cdb],
        out_specs=[pl.BlockSpec(st_shape, lambda b, c: (b * n + n - 1 - c, 0, 0, 0))],
        out_shape=[jax.ShapeDtypeStruct((B * n, N_HEADS, HEAD_DIM, HEAD_DIM), BF16)],
        scratch=[pltpu.VMEM((N_HEADS, HEAD_DIM, HEAD_DIM), F32)])

    def out(sb):
        return dict(
            kernel=_ret_out_kernel, per_head=True,
            in_specs=[seg(SEG_RQ), seg(SEG_RK), seg(SEG_RV), seg(SEG_RG),
                      pl.BlockSpec(st_shape, lambda b, c: (b * n + c, 0, 0, 0)),
                      full((N_HEADS, CHUNK, CHUNK)), full((CHUNK, MIX_W)), full((CHUNK, MIX_W)),
                      full((CHUNK, MIX_W)), full((N_HEADS, HEAD_DIM)), full((N_HEADS, HEAD_DIM))],
            args=[proj, proj, proj, proj, sb, dmat, qf_tab, qb_tab, kf_tab, cdf, ret_norm_w.astype(F32)],
            out_specs=[pl.BlockSpec((CHUNK, MIX_W), lambda b, c: (b * n + c, 0))],
            out_shape=[jax.ShapeDtypeStruct((T, MIX_W), BF16)],
            scratch=[pltpu.VMEM((N_HEADS, HEAD_DIM, HEAD_DIM), F32)])

    return state, out


def _na_index_tables(nqb):
    rows = NA_QROWS * nqb
    qi = np.arange(CHUNK)
    ki = np.arange(NA_KW)

    def one(qb):
        kb = NA_QROWS * int(np.clip(qb - 2, 0, nqb - NA_KBLKS))
        r = (NA_QROWS * qb + qi // GRID_W)[:, None]
        qc = (qi % GRID_W)[:, None]
        kr = (kb + ki // GRID_W)[None, :]
        kc = (ki % GRID_W)[None, :]
        rs = np.clip(r - WIN_ROWS // 2, 0, rows - WIN_ROWS)
        cs = np.clip(qc - WIN_COLS // 2, 0, GRID_W - WIN_COLS)
        valid = (kr >= rs) & (kr < rs + WIN_ROWS) & (kc >= cs) & (kc < cs + WIN_COLS)
        dr = np.clip(kr - r + WIN_ROWS - 1, 0, 2 * WIN_ROWS - 2)
        dc = np.clip(kc - qc + WIN_COLS - 1, 0, 2 * WIN_COLS - 2)
        return dr + 0 * dc, dc + 0 * dr, valid

    reps = [0, 1, 2, nqb - 2, nqb - 1]
    tabs = [one(qb) for qb in reps]
    for qb in range(2, nqb - 2):
        t = one(qb)
        assert all(np.array_equal(a, b) for a, b in zip(t, tabs[2]))
    return tuple(np.stack([t[i] for t in tabs]) for i in range(3))


def _na_kernel(q_ref, k0, k1, k2, k3, k4, v0, v1, v2, v3, v4, bias_ref, o_ref, defer=False):
    k_refs = (k0, k1, k2, k3, k4)
    v_refs = (v0, v1, v2, v3, v4)

    def head(h):
        hs = _hs(h)
        q = q_ref[:, hs]
        k_all = jnp.concatenate([r[:, hs] for r in k_refs], 0)
        v_all = jnp.concatenate([r[:, hs] for r in v_refs], 0)
        s = _dot_nt(q, k_all) + bias_ref[0, h]
        p = jnp.exp(s - jnp.max(s, -1, keepdims=True))
        o = _dot(p.astype(BF16), v_all)
        o_ref[:, hs] = (o / jnp.sum(p, -1, keepdims=True)).astype(BF16)

    return _heads(head, defer)


def _neighborhood(proj, B, L, rpb):
    nqb = L // CHUNK
    assert nqb >= NA_KBLKS and L % (GRID_W * NA_QROWS) == 0
    T = B * L
    dr, dc, valid = _na_index_tables(nqb)
    n_cls = dr.shape[0]
    kr_n = NA_KW // GRID_W
    dr6 = dr.reshape(n_cls, NA_QROWS, GRID_W, kr_n, GRID_W)
    dc6 = dc.reshape(n_cls, NA_QROWS, GRID_W, kr_n, GRID_W)
    dr_s = dr6[:, :, 0, :, 0]
    dc_s = dc6[0, 0, :, 0, :]
    assert np.array_equal(dr6, np.broadcast_to(dr_s[:, :, None, :, None], dr6.shape))
    assert np.array_equal(dc6, np.broadcast_to(dc_s[None, None, :, None, :], dc6.shape))
    sel_r = (dr_s[..., None] == np.arange(2 * WIN_ROWS - 1)).astype(np.float32)
    sel_c = (np.arange(2 * WIN_COLS - 1)[:, None, None] == dc_s[None]).astype(np.float32)
    hp = lax.Precision.HIGHEST
    t = jnp.einsum('cajr,hrd->cajhd', sel_r, rpb.astype(F32), precision=hp)
    bias = jnp.einsum('cajhd,dqk->chaqjk', t, sel_c, precision=hp).reshape(n_cls, N_HEADS, CHUNK, NA_KW)
    bias = jnp.where(jnp.asarray(valid)[:, None], bias, NEG)

    def kspec(s, j):
        return pl.BlockSpec((CHUNK, SEG_W),
                            lambda b, qb: (b * nqb + jnp.clip(qb - 2, 0, nqb - NA_KBLKS) + j, s))

    def cls(qb):
        return jnp.where(qb < 2, qb, jnp.where(qb >= nqb - 2, qb - (nqb - 5), 2))

    return dict(
        kernel=_na_kernel, per_head=True,
        in_specs=[pl.BlockSpec((CHUNK, SEG_W), lambda b, qb: (b * nqb + qb, SEG_NQ))]
        + [kspec(SEG_NK, j) for j in range(NA_KBLKS)]
        + [kspec(SEG_NV, j) for j in range(NA_KBLKS)]
        + [pl.BlockSpec((1, N_HEADS, CHUNK, NA_KW), lambda b, qb: (cls(qb), 0, 0, 0))],
        args=[proj] * (1 + 2 * NA_KBLKS) + [bias],
        out_specs=[pl.BlockSpec((CHUNK, MIX_W), lambda b, qb: (b * nqb + qb, 0))],
        out_shape=[jax.ShapeDtypeStruct((T, MIX_W), BF16)],
        scratch=[])


HALO = 16


def _log_sigmoid(x):
    return jnp.minimum(x, 0.0) - jnp.log1p(jnp.exp(-jnp.abs(x)))


def _conv_silu(x_ref, prev_ref, next_ref, w_ref, col0, c, n):
    x = x_ref[...].astype(F32)
    row = lax.broadcasted_iota(jnp.int32, x.shape, 0)
    prev_row = prev_ref[HALO - 1:HALO, :].astype(F32) * (c > 0).astype(F32)
    next_row = next_ref[0:1, :].astype(F32) * (c < n - 1).astype(F32)
    x_prev = jnp.where(row == 0, prev_row, pltpu.roll(x, 1, 0))
    x_next = jnp.where(row == CHUNK - 1, next_row, pltpu.roll(x, CHUNK - 1, 0))
    cs = slice(col0, col0 + MIX_W)
    y = x_prev * w_ref[0:1, cs] + x * w_ref[1:2, cs] + x_next * w_ref[2:3, cs]
    return y * _sigmoid(y)


def _tri():
    r = lax.broadcasted_iota(jnp.int32, (CHUNK, CHUNK), 0)
    c = lax.broadcasted_iota(jnp.int32, (CHUNK, CHUNK), 1)
    return (c <= r).astype(F32), (c >= r).astype(F32)


def _hp_dot(a, b):
    return jnp.dot(a, b, preferred_element_type=F32, precision=lax.Precision.HIGHEST)


FWD_LANE = 2 * N_HEADS
BWD_LANE = 3 * N_HEADS


def _gate_dense(gc_ref, gr_ref, bc_ref, br_ref):
    low, up = _tri()
    g_col = gc_ref[...] + bc_ref[...]
    lf_col = _log_sigmoid(g_col)
    lf_row = _log_sigmoid(gr_ref[...] + br_ref[...])
    b_col = _hp_dot(low, lf_col)
    b_row = _hp_dot(lf_row, up)
    tot_row = b_col[CHUNK - 1:CHUNK, :]
    tot_col = b_row[:, CHUNK - 1:CHUNK]
    lane = lax.broadcasted_iota(jnp.int32, (CHUNK, LANES), 1)
    row = lax.broadcasted_iota(jnp.int32, (N_GATE_COLS, CHUNK), 0)
    bb_col = jnp.where(lane >= BWD_LANE, tot_row - b_col + lf_col, b_col)
    bb_row = jnp.where(row >= BWD_LANE, tot_col - b_row + lf_row, b_row)
    ck = pltpu.roll(g_col, FWD_LANE, 1) - bb_col
    a = tot_row + ck
    m_loc = jnp.max(a, 0, keepdims=True)
    ea = jnp.exp(a - m_loc)
    return bb_row, ck, tot_row, m_loc, ea


def _lane_bcast(x, l):
    return jnp.broadcast_to(x[:, l:l + 1], (x.shape[0], HEAD_DIM))


def _ml_state_update(kc_h, ct_loc_fn, ea, tot_row, m_loc, l, c_scr, n_scr, m_scr, h):
    m_p = m_scr[h:h + 1, 0:1]
    g_tot = tot_row[:, l:l + 1]
    m_l = m_loc[:, l:l + 1]
    kw = kc_h * _lane_bcast(ea, l)
    ct_loc = ct_loc_fn(kw.astype(BF16))
    n_loc = jnp.sum(kw, 0, keepdims=True)
    m_new = jnp.maximum(g_tot + m_p, m_l)
    sp = jnp.exp(g_tot + m_p - m_new)
    sl = jnp.exp(m_l - m_new)
    c_scr[h] = sp * c_scr[h] + sl * ct_loc
    n_scr[h:h + 1, :] = sp * n_scr[h:h + 1, :] + sl * n_loc
    m_scr[h:h + 1, :] = jnp.broadcast_to(m_new, (1, HEAD_DIM))


def _ml_init(c, c_scr, n_scr, m_scr):
    @pl.when(c == 0)
    def _():
        c_scr[...] = jnp.zeros_like(c_scr)
        n_scr[...] = jnp.zeros_like(n_scr)
        m_scr[...] = jnp.full_like(m_scr, NEG)


def _ml_state_kernel(k_ref, kp_ref, kn_ref, v_ref, gc_ref, gr_ref, cw_ref, bc_ref, br_ref,
                     cb_ref, nm_ref, c_scr, n_scr, m_scr, *, n):
    step = pl.program_id(1)
    _ml_init(step, c_scr, n_scr, m_scr)
    c = n - 1 - step
    cb_ref[0] = c_scr[...].astype(BF16)
    nm_ref[0, 0:N_HEADS, :] = n_scr[...]
    nm_ref[0, N_HEADS:2 * N_HEADS, :] = m_scr[...]

    kc = _conv_silu(k_ref, kp_ref, kn_ref, cw_ref, MIX_W, c, n) * (HEAD_DIM ** -0.5)
    v = v_ref[...]
    _, _, tot_row, m_loc, ea = _gate_dense(gc_ref, gr_ref, bc_ref, br_ref)
    for h in range(N_HEADS):
        v_h = v[:, _hs(h)]
        _ml_state_update(kc[:, _hs(h)], lambda kw: _dot_tn(v_h, kw), ea, tot_row, m_loc, BWD_LANE + h,
                         c_scr, n_scr, m_scr, h)


def _ml_direction(st, mask, b_q, ck_b, m_p, qn, vt_bf, ct_bf, q_bf):
    dmat = jnp.where(mask, b_q + ck_b, NEG)
    inter = b_q + m_p
    m_row = jnp.maximum(jnp.max(dmat, 0, keepdims=True), inter)
    s = st * jnp.exp(dmat - m_row)
    e_int = jnp.exp(inter - m_row)
    num = _dot(vt_bf, s.astype(BF16)) + e_int * _dot_nt(ct_bf, q_bf)
    den = jnp.sum(s, 0, keepdims=True) + e_int * qn
    return num / jnp.maximum(jnp.abs(den), jnp.exp(-m_row))


def _ml_out_kernel(q_ref, qp_ref, qn_ref, k_ref, kp_ref, kn_ref, v_ref, o_ref_in, gc_ref, gr_ref,
                   cb_ref, nm_ref, cw_ref, bc_ref, br_ref, nwt_ref,
                   out_ref, c_scr, n_scr, m_scr, *, n, defer=False):
    c = pl.program_id(1)
    _ml_init(c, c_scr, n_scr, m_scr)
    qc = _conv_silu(q_ref, qp_ref, qn_ref, cw_ref, 0, c, n)
    kc = _conv_silu(k_ref, kp_ref, kn_ref, cw_ref, MIX_W, c, n) * (HEAD_DIM ** -0.5)
    q_bf = qc.astype(BF16)
    k_bf = kc.astype(BF16)
    v = v_ref[...]
    og = o_ref_in[...].astype(F32)
    bb_row, ck, tot_row, m_loc, ea = _gate_dense(gc_ref, gr_ref, bc_ref, br_ref)
    key = lax.broadcasted_iota(jnp.int32, (CHUNK, CHUNK), 0)
    qry = lax.broadcasted_iota(jnp.int32, (CHUNK, CHUNK), 1)
    causal = key <= qry
    anti = key >= qry
    pad = jnp.zeros((8 - 2, HEAD_DIM), F32)

    def head(h):
        hs = _hs(h)
        lf, lb = FWD_LANE + h, BWD_LANE + h
        qh = q_bf[:, hs]
        vt_bf = v[:, hs].astype(F32).T.astype(BF16)
        st = _dot_nt(k_bf[:, hs], qh)
        n_prev = jnp.concatenate([n_scr[h:h + 1, :], nm_ref[0, h:h + 1, :], pad], 0).astype(BF16)
        qn = _dot_nt(n_prev, qh)
        h_f = _ml_direction(st, causal, bb_row[lf:lf + 1, :], _lane_bcast(ck, lf), m_scr[h:h + 1, 0:1],
                            qn[0:1, :], vt_bf, c_scr[h].astype(BF16), qh)
        h_b = _ml_direction(st, anti, bb_row[lb:lb + 1, :], _lane_bcast(ck, lb),
                            nm_ref[0, N_HEADS + h:N_HEADS + h + 1, 0:1], qn[1:2, :], vt_bf, cb_ref[0, h], qh)
        _ml_state_update(kc[:, hs], lambda kw: _dot(vt_bf, kw), ea, tot_row, m_loc, lf, c_scr, n_scr, m_scr, h)
        ot = h_f + h_b
        yt = ot * lax.rsqrt(jnp.mean(ot * ot, 0, keepdims=True) + EPS) * nwt_ref[h]
        out_ref[:, hs] = (yt.T * _sigmoid(og[:, hs])).astype(BF16)

    return _heads(head, defer)


def _mlstm(proj, gates, gates_t, B, L, conv_w, ig_b, fg_b, norm_w):
    n = L // CHUNK
    T = B * L
    hb = CHUNK // HALO
    bias = jnp.concatenate([ig_b.astype(F32).reshape(-1), fg_b.astype(F32).reshape(-1)])
    bias_col = jnp.zeros((1, LANES), F32).at[0, :N_GATE_COLS].set(bias)
    bias_row = jnp.broadcast_to(bias[:, None], (N_GATE_COLS, CHUNK))
    cw = conv_w.astype(F32)
    norm_wt = jnp.broadcast_to(norm_w.astype(F32)[:, :, None], (N_HEADS, HEAD_DIM, CHUNK))

    def seg(s, rev):
        if rev:
            return pl.BlockSpec((CHUNK, SEG_W), lambda b, c: (b * n + n - 1 - c, s))
        return pl.BlockSpec((CHUNK, SEG_W), lambda b, c: (b * n + c, s))

    def chunk_of(c, rev):
        return n - 1 - c if rev else c

    def prev_spec(s, rev):
        return pl.BlockSpec((HALO, SEG_W),
                            lambda b, c: (b * n * hb + jnp.maximum(chunk_of(c, rev) * hb - 1, 0), s))

    def next_spec(s, rev):
        return pl.BlockSpec((HALO, SEG_W),
                            lambda b, c: (b * n * hb + jnp.minimum((chunk_of(c, rev) + 1) * hb, n * hb - 1), s))

    def gcol_spec(rev):
        return pl.BlockSpec((CHUNK, LANES), lambda b, c: (b * n + chunk_of(c, rev), 0))

    def grow_spec(rev):
        return pl.BlockSpec((N_GATE_COLS, CHUNK), lambda b, c: (0, b * n + chunk_of(c, rev)))

    full = lambda shape: pl.BlockSpec(shape, lambda b, c: (0,) * len(shape))
    st_shape = (1, N_HEADS, HEAD_DIM, HEAD_DIM)
    nm_shape = (1, 2 * N_HEADS, HEAD_DIM)
    scratch = [pltpu.VMEM((N_HEADS, HEAD_DIM, HEAD_DIM), F32), pltpu.VMEM((N_HEADS, HEAD_DIM), F32),
               pltpu.VMEM((N_HEADS, HEAD_DIM), F32)]
    consts = [full((ML_CONV_W, 2 * MIX_W)), full((1, LANES)), full((N_GATE_COLS, CHUNK))]

    state = dict(
        kernel=functools.partial(_ml_state_kernel, n=n),
        in_specs=[seg(SEG_MK, True), prev_spec(SEG_MK, True), next_spec(SEG_MK, True), seg(SEG_MV, True),
                  gcol_spec(True), grow_spec(True)] + consts,
        args=[proj, proj, proj, proj, gates, gates_t, cw, bias_col, bias_row],
        out_specs=[pl.BlockSpec(st_shape, lambda b, c: (b * n + n - 1 - c, 0, 0, 0)),
                   pl.BlockSpec(nm_shape, lambda b, c: (b * n + n - 1 - c, 0, 0))],
        out_shape=[jax.ShapeDtypeStruct((B * n, N_HEADS, HEAD_DIM, HEAD_DIM), BF16),
                   jax.ShapeDtypeStruct((B * n, 2 * N_HEADS, HEAD_DIM), F32)],
        scratch=scratch)

    def out(cb, nm):
        return dict(
            kernel=functools.partial(_ml_out_kernel, n=n), per_head=True,
            in_specs=[seg(SEG_MQ, False), prev_spec(SEG_MQ, False), next_spec(SEG_MQ, False),
                      seg(SEG_MK, False), prev_spec(SEG_MK, False), next_spec(SEG_MK, False),
                      seg(SEG_MV, False), seg(SEG_MO, False), gcol_spec(False), grow_spec(False),
                      pl.BlockSpec(st_shape, lambda b, c: (b * n + c, 0, 0, 0)),
                      pl.BlockSpec(nm_shape, lambda b, c: (b * n + c, 0, 0))]
            + consts + [full((N_HEADS, HEAD_DIM, CHUNK))],
            args=[proj, proj, proj, proj, proj, proj, proj, proj, gates, gates_t, cb, nm,
                  cw, bias_col, bias_row, norm_wt],
            out_specs=[pl.BlockSpec((CHUNK, MIX_W), lambda b, c: (b * n + c, 0))],
            out_shape=[jax.ShapeDtypeStruct((T, MIX_W), BF16)],
            scratch=scratch)

    return state, out


def _fused_call(parts, grid, name):
    n_in = [len(p["in_specs"]) for p in parts]
    n_out = [len(p["out_specs"]) for p in parts]
    n_scr = [len(p["scratch"]) for p in parts]

    def body(*refs):
        ins, outs, scr = refs[:sum(n_in)], refs[sum(n_in):sum(n_in) + sum(n_out)], refs[sum(n_in) + sum(n_out):]
        i = o = k = 0
        heads = []
        for p, a, b, c in zip(parts, n_in, n_out, n_scr):
            args = (*ins[i:i + a], *outs[o:o + b], *scr[k:k + c])
            if p.get("per_head"):
                heads.append(p["kernel"](*args, defer=True))
            else:
                p["kernel"](*args)
            i, o, k = i + a, o + b, k + c
        for h in range(N_HEADS):
            for head in heads:
                head(h)

    return pl.pallas_call(
        body,
        grid=grid,
        in_specs=[s for p in parts for s in p["in_specs"]],
        out_specs=[s for p in parts for s in p["out_specs"]],
        out_shape=[s for p in parts for s in p["out_shape"]],
        scratch_shapes=[s for p in parts for s in p["scratch"]],
        compiler_params=_cparams(("arbitrary", "arbitrary")),
        name=name,
    )(*[a for p in parts for a in p["args"]])


def _mixers(proj, gates, gates_t, B, L, p):
    grid = (B, L // CHUNK)
    ret_state, ret_out = _retention(proj, B, L, p["ret_decay"], p["ret_norm_w"])
    ml_state, ml_out = _mlstm(proj, gates, gates_t, B, L, p["ml_conv"], p["ml_igate_b"], p["ml_fgate_b"],
                              p["ml_norm_w"])
    sb, cb, nm = _fused_call([ret_state, ml_state], grid, "mixer_states")
    na = _neighborhood(proj, B, L, p["na_rpb"])
    return _fused_call([ret_out(sb), na, ml_out(cb, nm)], grid, "mixer_outputs")


def _branch_kernel(oret_ref, ona_ref, oml_ref, ga_ref, gb_ref, gc_ref, wb_ref, bgb_ref, m_ref):
    o_refs = (oret_ref, ona_ref, oml_ref)
    g_refs = (ga_ref, gb_ref, gc_ref)
    for n in range(D_MODEL // SEG_W):
        cs = slice(n * SEG_W, (n + 1) * SEG_W)
        acc = None
        for i in range(N_BRANCH):
            z = g_refs[i][:, cs].astype(F32) + bgb_ref[i:i + 1, cs]
            term = _sigmoid(z) * _dot(o_refs[i][...], wb_ref[i, :, cs])
            acc = term if acc is None else acc + term
        m_ref[:, cs] = acc.astype(BF16)


def _outproj_router_kernel(x_ref, m_ref, wout_ref, n2w_ref, rwhl_ref, rb_ref,
                           h_ref, hn_ref, route_ref, cnt_ref, cnt_scr):
    h = x_ref[...] + _dot(m_ref[...], wout_ref[...])
    h_ref[...] = h
    hn = h * lax.rsqrt(jnp.mean(h * h, -1, keepdims=True) + EPS) * n2w_ref[...]
    hn_ref[...] = hn

    hn_hi = hn.astype(BF16)
    hn_lo = (hn - hn_hi.astype(F32)).astype(BF16)
    hi = _dot(hn_hi, rwhl_ref[...])
    logits = (hi[:, :LANES] + _dot(hn_lo, rwhl_ref[:, :LANES]) + hi[:, LANES:]) + rb_ref[...]
    lane_i = lax.broadcasted_iota(jnp.int32, logits.shape, 1)
    lane = lane_i.astype(F32)
    lane_grp = jnp.right_shift(lane_i - N_GROUPS, 3).astype(F32)
    gmask = lane_i < N_GROUPS
    lg = jnp.where(gmask, logits, NEG)
    mg = jnp.max(lg, -1, keepdims=True)
    p_top = 1.0 / jnp.sum(jnp.where(gmask, jnp.exp(lg - mg), 0.0), -1, keepdims=True)
    grp = jnp.min(jnp.where(lg == mg, lane, float(LANES)), -1, keepdims=True)
    emask = (lane_i >= N_GROUPS) & (lane_i < N_GROUPS + N_EXPERTS) & (lane_grp == grp)
    le = jnp.where(emask, logits, NEG)
    m1 = jnp.max(le, -1, keepdims=True)
    i1 = jnp.min(jnp.where(le == m1, lane, float(LANES)), -1, keepdims=True)
    le2 = jnp.where(lane == i1, NEG, le)
    m2 = jnp.max(le2, -1, keepdims=True)
    i2 = jnp.min(jnp.where(le2 == m2, lane, float(LANES)), -1, keepdims=True)
    e2 = jnp.exp(m2 - m1)
    gate1 = p_top / (1.0 + e2)
    gate2 = p_top * e2 / (1.0 + e2)

    @pl.when(pl.program_id(0) == 0)
    def _():
        cnt_scr[...] = jnp.zeros_like(cnt_scr)

    tm = logits.shape[0]
    hit1 = lane == i1
    hit2 = lane == i2
    onehot = jnp.where(hit1 | hit2, 1.0, 0.0)
    r_i = lax.broadcasted_iota(jnp.int32, (tm, tm), 0)
    c_i = lax.broadcasted_iota(jnp.int32, (tm, tm), 1)
    before = jnp.where(c_i < r_i, 1.0, 0.0).astype(BF16)
    prior = _dot(before, onehot.astype(BF16)) + cnt_scr[...]
    rank1 = jnp.sum(jnp.where(hit1, prior, 0.0), -1, keepdims=True)
    rank2 = jnp.sum(jnp.where(hit2, prior, 0.0), -1, keepdims=True)
    cnt = cnt_scr[...] + jnp.sum(onehot, 0, keepdims=True)
    cnt_scr[...] = cnt
    cnt_ref[...] = jnp.broadcast_to(cnt, cnt_ref.shape)

    cols = (i1 - N_GROUPS, i2 - N_GROUPS, gate1, gate2, rank1, rank2)
    route = jnp.zeros_like(logits)
    for ci, val in enumerate(cols):
        route = jnp.where(lane_i == ci, val, route)
    route_ref[...] = route


MERGE_ROWS = 512


def _merge(x2d, o_ret, o_na, o_ml, proj, wb, bgb, wout, n2w, rw_hi, rw_lo, rb):
    T = x2d.shape[0]
    tm = MERGE_ROWS
    full = lambda shape: pl.BlockSpec(shape, lambda i: (0,) * len(shape), pipeline_mode=pl.Buffered(1))
    row = lambda w: pl.BlockSpec((tm, w), lambda i: (i, 0))
    gate = lambda s: pl.BlockSpec((tm, D_MODEL), lambda i: (i, s // 2))
    merged = pl.pallas_call(
        _branch_kernel,
        grid=(T // tm,),
        in_specs=[row(MIX_W), row(MIX_W), row(MIX_W), gate(SEG_GA), gate(SEG_GB), gate(SEG_GC),
                  full((N_BRANCH, MIX_W, D_MODEL)), full((N_BRANCH, D_MODEL))],
        out_specs=row(D_MODEL),
        out_shape=jax.ShapeDtypeStruct((T, D_MODEL), BF16),
        compiler_params=_cparams(("arbitrary",)),
        name="branch_merge",
    )(o_ret, o_na, o_ml, proj, proj, proj, wb, bgb)
    return pl.pallas_call(
        _outproj_router_kernel,
        grid=(T // tm,),
        in_specs=[row(D_MODEL), row(D_MODEL), full((D_MODEL, D_MODEL)), full((1, D_MODEL)),
                  full((D_MODEL, 2 * LANES)), full((1, LANES))],
        out_specs=[row(D_MODEL), row(D_MODEL), row(LANES), pl.BlockSpec((8, LANES), lambda i: (0, 0))],
        out_shape=[jax.ShapeDtypeStruct((T, D_MODEL), F32), jax.ShapeDtypeStruct((T, D_MODEL), F32),
                   jax.ShapeDtypeStruct((T, LANES), F32), jax.ShapeDtypeStruct((8, LANES), F32)],
        scratch_shapes=[pltpu.VMEM((1, LANES), F32)],
        compiler_params=_cparams(("arbitrary",)),
        name="outproj_router",
    )(x2d, merged, wout, n2w, jnp.concatenate([rw_hi, rw_lo], 1), rb)


def _row_copy(src_hbm, row, dst, r, sem):
    return pltpu.make_async_copy(src_hbm.at[pl.ds(row, 1), :], dst.at[pl.ds(r, 1), :], sem)


def _ffn_kernel(blk_e_ref, n_used_ref, src_ref, hn_hbm, wg_ref, wu_ref, wd_ref, y_ref, xbuf0, xbuf1, a_scr, sem):
    i = pl.program_id(0)
    n_used = n_used_ref[0]
    slot = i % 2
    bufs = (xbuf0, xbuf1)

    def wait_block(s):
        pltpu.make_async_copy(hn_hbm.at[pl.ds(0, MOE_ROWS), :], bufs[s], sem.at[s]).wait()

    @pl.when(i == 0)
    def _():
        def body(r, carry):
            _row_copy(hn_hbm, src_ref[r], xbuf0, r, sem.at[0]).start()
            return carry
        lax.fori_loop(0, MOE_ROWS, body, 0, unroll=8)

    def gate_and_gather(s):
        wait_block(s)
        for r in range(MOE_ROWS):
            _row_copy(hn_hbm, src_ref[(i + 1) * MOE_ROWS + r], bufs[1 - s], r, sem.at[1 - s]).start()
        a_scr[...] = _dot(bufs[s][...].astype(BF16), wg_ref[0])

    def up_and_down(s):
        a = a_scr[...]
        hid = (a * _sigmoid(a)) * _dot(bufs[s][...].astype(BF16), wu_ref[0])
        y_ref[...] = _dot(hid.astype(BF16), wd_ref[0])

    for s in range(2):
        @pl.when((i < n_used) & (slot == s))
        def _():
            gate_and_gather(s)

            def rest(_, carry):
                up_and_down(s)
                return carry
            lax.fori_loop(0, jnp.minimum(n_used, 1), rest, 0)

        @pl.when((i == n_used) & (slot == s))
        def _():
            wait_block(s)

    @pl.when(i >= n_used)
    def _():
        y_ref[...] = jnp.zeros_like(y_ref)


def _expert_ffn(hn, src, blk_e, n_used, wg, wu, wd):
    n_blocks = src.shape[0] // MOE_ROWS
    grid_spec = pltpu.PrefetchScalarGridSpec(
        num_scalar_prefetch=3,
        grid=(n_blocks,),
        in_specs=[
            pl.BlockSpec(memory_space=pl.ANY),
            pl.BlockSpec((1, D_MODEL, D_EXPERT), lambda i, be, nu, sr: (be[i], 0, 0)),
            pl.BlockSpec((1, D_MODEL, D_EXPERT), lambda i, be, nu, sr: (be[i], 0, 0)),
            pl.BlockSpec((1, D_EXPERT, D_MODEL), lambda i, be, nu, sr: (be[i], 0, 0)),
        ],
        out_specs=pl.BlockSpec((MOE_ROWS, D_MODEL), lambda i, be, nu, sr: (i, 0)),
        scratch_shapes=[pltpu.VMEM((MOE_ROWS, D_MODEL), F32), pltpu.VMEM((MOE_ROWS, D_MODEL), F32),
                        pltpu.VMEM((MOE_ROWS, D_EXPERT), F32), pltpu.SemaphoreType.DMA((2,))],
    )
    return pl.pallas_call(
        _ffn_kernel,
        grid_spec=grid_spec,
        out_shape=jax.ShapeDtypeStruct((n_blocks * MOE_ROWS, D_MODEL), F32),
        compiler_params=_cparams(("arbitrary",), disable_bounds_checks=True),
        name="expert_ffn",
    )(blk_e, n_used, src, hn, wg, wu, wd)


COMBINE_ROWS = 256


def _combine_kernel(dest_ref, h_ref, route_ref, yb_hbm, out_ref, ybuf, sem):
    i = pl.program_id(0)
    slot = i % 2

    def start_row(tile, r, to_slot):
        a = (tile * COMBINE_ROWS + r) * TOP_K_INNER
        for k in range(TOP_K_INNER):
            _row_copy(yb_hbm, dest_ref[a + k], ybuf.at[to_slot, k], r, sem.at[to_slot]).start()

    @pl.when(i == 0)
    def _():
        def body(r, carry):
            start_row(0, r, 0)
            return carry
        lax.fori_loop(0, COMBINE_ROWS, body, 0, unroll=8)

    @pl.when(i + 1 < pl.num_programs(0))
    def _():
        for r in range(COMBINE_ROWS):
            start_row(i + 1, r, 1 - slot)

    for k in range(TOP_K_INNER):
        pltpu.make_async_copy(yb_hbm.at[pl.ds(0, COMBINE_ROWS), :], ybuf.at[slot, k], sem.at[slot]).wait()

    route = route_ref[...]
    out_ref[...] = h_ref[...] + route[:, 2:3] * ybuf[slot, 0] + route[:, 3:4] * ybuf[slot, 1]


def _combine(h, route, yb, dest):
    T = h.shape[0]
    grid_spec = pltpu.PrefetchScalarGridSpec(
        num_scalar_prefetch=1,
        grid=(T // COMBINE_ROWS,),
        in_specs=[
            pl.BlockSpec((COMBINE_ROWS, D_MODEL), lambda i, d: (i, 0)),
            pl.BlockSpec((COMBINE_ROWS, LANES), lambda i, d: (i, 0)),
            pl.BlockSpec(memory_space=pl.ANY),
        ],
        out_specs=pl.BlockSpec((COMBINE_ROWS, D_MODEL), lambda i, d: (i, 0)),
        scratch_shapes=[pltpu.VMEM((2, TOP_K_INNER, COMBINE_ROWS, D_MODEL), F32), pltpu.SemaphoreType.DMA((2,))],
    )
    return pl.pallas_call(
        _combine_kernel,
        grid_spec=grid_spec,
        out_shape=jax.ShapeDtypeStruct((T, D_MODEL), F32),
        compiler_params=_cparams(("arbitrary",), disable_bounds_checks=True),
        name="moe_combine",
    )(dest, h, route, yb)


def _moe(h, hn, route, counts, wg, wu, wd, layer):
    T = h.shape[0]
    n_assign = T * TOP_K_INNER
    expert = route[:, 0:2].astype(jnp.int32)
    rank = route[:, 4:6].astype(jnp.int32)
    cnt = counts[0, N_GROUPS:N_GROUPS + N_EXPERTS].astype(jnp.int32)
    padded = (cnt + MOE_ROWS - 1) // MOE_ROWS * MOE_ROWS
    pend = jnp.cumsum(padded)
    pstart = pend - padded
    dest = (pstart[expert] + rank).reshape(-1)
    n_blocks = (n_assign + N_EXPERTS * (MOE_ROWS - 1) + MOE_ROWS - 1) // MOE_ROWS + 1
    blk_start = jnp.arange(n_blocks, dtype=jnp.int32) * MOE_ROWS
    blk_e = jnp.minimum(jnp.sum((pend[None, :] <= blk_start[:, None]).astype(jnp.int32), 1), N_EXPERTS - 1)
    blk_e = blk_e + layer * N_EXPERTS
    n_used = (pend[-1] // MOE_ROWS).astype(jnp.int32).reshape(1)
    flat_t = jnp.repeat(jnp.arange(T, dtype=jnp.int32), TOP_K_INNER)
    src = jnp.zeros((n_blocks * MOE_ROWS,), jnp.int32).at[dest].set(flat_t)
    yb = _expert_ffn(hn, src, blk_e, n_used, wg, wu, wd)
    return _combine(h, route, yb, dest)


def _rope_tables(L):
    freqs = ROPE_BASE ** (-jnp.arange(0, HEAD_DIM, 2, dtype=F32) / HEAD_DIM)
    ang = jnp.arange(L, dtype=F32)[:, None] * freqs[None]
    cos, sin = jnp.cos(ang), jnp.sin(ang)
    return jnp.concatenate([cos, cos], -1), jnp.concatenate([-sin, sin], -1)


N_MIX_SEG = 11


REORDER_COLS = SEG_W // 2
N_BRANCH_BLK = N_BRANCH * D_MODEL // REORDER_COLS


def _reorder_kernel(wm_ref, wg_ref, o_ref):
    j = pl.program_id(1)

    @pl.when(j < N_BRANCH_BLK)
    def _():
        o_ref[...] = wg_ref[0].T.astype(BF16)

    @pl.when(j >= N_BRANCH_BLK)
    def _():
        o_ref[...] = wm_ref[...].T.astype(BF16)


def _reorder_w_in(w_in_t):
    depth = w_in_t.shape[0]
    gate_row0 = N_MIX_SEG * SEG_W + N_GATE_COLS
    blk = (None, REORDER_COLS, D_MODEL)
    return pl.pallas_call(
        _reorder_kernel,
        grid=(depth, N_SEG * SEG_W // REORDER_COLS),
        in_specs=[pl.BlockSpec(blk, lambda l, j: (l, jnp.maximum(j - N_BRANCH_BLK, 0), 0)),
                  pl.BlockSpec((pl.Element(1), pl.Element(REORDER_COLS), pl.Element(D_MODEL)),
                               lambda l, j: (l, pl.multiple_of(
                                   gate_row0 + jnp.minimum(j, N_BRANCH_BLK - 1) * REORDER_COLS, 32), 0))],
        out_specs=pl.BlockSpec((None, D_MODEL, REORDER_COLS), lambda l, j: (l, 0, j)),
        out_shape=jax.ShapeDtypeStruct((depth, D_MODEL, N_SEG * SEG_W), BF16),
        compiler_params=_cparams(("arbitrary", "arbitrary")),
        name="reorder_w_in",
    )(w_in_t, w_in_t)


def _cast_kernel(w_ref, o_ref):
    o_ref[...] = w_ref[...].astype(BF16)


def _cast_experts(w):
    depth, e, r, c = w.shape
    spec = pl.BlockSpec((1, r, c), lambda i: (i, 0, 0))
    return pl.pallas_call(
        _cast_kernel,
        grid=(depth * e,),
        in_specs=[spec],
        out_specs=spec,
        out_shape=jax.ShapeDtypeStruct((depth * e, r, c), BF16),
        compiler_params=_cparams(("arbitrary",)),
        name="cast_experts",
    )(w.reshape(depth * e, r, c))


def _prep_layer(p):
    wift = p["wif_t"]
    wif_pad = jnp.zeros((D_MODEL, LANES), F32).at[:, :N_GATE_COLS].set(wift.T)
    rw = jnp.zeros((D_MODEL, LANES), F32).at[:, :N_GROUPS].set(p["router_g_w"]) \
        .at[:, N_GROUPS:N_GROUPS + N_EXPERTS].set(p["router_e_w"])
    rb = jnp.zeros((1, LANES), F32).at[0, :N_GROUPS].set(p["router_g_b"]) \
        .at[0, N_GROUPS:N_GROUPS + N_EXPERTS].set(p["router_e_b"])
    return dict(
        nw=p["norm1_w"].astype(F32).reshape(1, D_MODEL), wif=wif_pad, wift=wift,
        qn=p["na_q_norm"].astype(F32).reshape(1, HEAD_DIM), kn=p["na_k_norm"].astype(F32).reshape(1, HEAD_DIM),
        wb=p["w_branch"].astype(BF16), bgb=p["branch_gate_b"].astype(F32), wout=p["w_out"].astype(BF16),
        n2w=p["norm2_w"].astype(F32).reshape(1, D_MODEL), rb=rb,
        rw_hi=rw.astype(BF16), rw_lo=(rw - rw.astype(BF16).astype(F32)).astype(BF16),
    )


def _layer(x2d, B, L, p, q, rope, w_main, experts, layer):
    proj, gates, gates_t = _inproj(x2d, L, q["nw"], w_main, layer, q["wif"], q["wift"], rope[0], rope[1],
                                   q["qn"], q["kn"])
    o_ret, o_na, o_ml = _mixers(proj, gates, gates_t, B, L, p)
    h, hn, route, counts = _merge(x2d, o_ret, o_na, o_ml, proj, q["wb"], q["bgb"], q["wout"], q["n2w"],
                                  q["rw_hi"], q["rw_lo"], q["rb"])
    return _moe(h, hn, route, counts, experts[0], experts[1], experts[2], layer)


_PARAM_NAMES = ("norm1_w", "w_in", "ret_decay", "ret_norm_w", "na_q_norm", "na_k_norm", "na_rpb", "ml_conv",
                "ml_igate_b", "ml_fgate_b", "ml_norm_w", "w_branch", "branch_gate_b", "w_out", "norm2_w",
                "router_g_w", "router_g_b", "router_e_w", "router_e_b", "exp_w_gate", "exp_w_up", "exp_w_down")


def _run(x, layers, preps, w_main, experts):
    B, L, D = x.shape
    rope = _rope_tables(L)
    x2d = x.reshape(B * L, D)
    for layer, (p, q) in enumerate(zip(layers, preps)):
        x2d = _layer(x2d, B, L, p, q, rope, w_main, experts, layer)
    return x2d.reshape(B, L, D)


def kernel(x_prompt, x_sample, norm1_w, w_in, ret_decay, ret_norm_w, na_q_norm, na_k_norm, na_rpb, ml_conv,
           ml_igate_b, ml_fgate_b, ml_norm_w, w_branch, branch_gate_b, w_out, norm2_w, router_g_w, router_g_b,
           router_e_w, router_e_b, exp_w_gate, exp_w_up, exp_w_down):
    stacked = (norm1_w, w_in, ret_decay, ret_norm_w, na_q_norm, na_k_norm, na_rpb, ml_conv, ml_igate_b,
               ml_fgate_b, ml_norm_w, w_branch, branch_gate_b, w_out, norm2_w, router_g_w, router_g_b,
               router_e_w, router_e_b, exp_w_gate, exp_w_up, exp_w_down)
    depth = w_in.shape[0]
    w_in_t = jnp.swapaxes(w_in, 1, 2)
    wif_t = w_in_t[:, N_MIX_SEG * SEG_W:N_MIX_SEG * SEG_W + N_GATE_COLS, :]
    layers = [dict(zip(_PARAM_NAMES, (None if a is w_in else a[l] for a in stacked)), wif_t=wif_t[l])
              for l in range(depth)]
    preps = [_prep_layer(p) for p in layers]
    experts = tuple(_cast_experts(w) for w in (exp_w_gate, exp_w_up, exp_w_down))
    w_main = _reorder_w_in(w_in_t)
    return (_run(x_prompt, layers, preps, w_main, experts), _run(x_sample, layers, preps, w_main, experts))
```

```python
import functools

import numpy as np
import jax
import jax.numpy as jnp
from jax import lax
from jax.experimental import pallas as pl
from jax.experimental.pallas import tpu as pltpu

D_MODEL = 2048
HEAD_DIM = 128
MIX_W = D_MODEL // 2
N_HEADS = MIX_W // HEAD_DIM
N_BRANCH = 3
CHUNK = 128
ROPE_BASE = 10000.0
GRID_W = 64
WIN_ROWS = 8
WIN_COLS = 16
ML_CONV_W = 3
N_GROUPS = 4
EXPERTS_PER_GROUP = 8
N_EXPERTS = N_GROUPS * EXPERTS_PER_GROUP
TOP_K_INNER = 2
D_EXPERT = D_MODEL // 2
EPS = 1e-6
NEG = -1e30

F32 = jnp.float32
BF16 = jnp.bfloat16

LANES = 128
SEG_W = MIX_W
SEG_GA, SEG_GB, SEG_GC = 0, 2, 4
SEG_RQ, SEG_RK, SEG_RV, SEG_RG = 6, 7, 8, 9
SEG_NQ, SEG_NK, SEG_NV = 10, 11, 12
SEG_MQ, SEG_MK, SEG_MV, SEG_MO = 13, 14, 15, 16
N_SEG = 17
N_GATE_COLS = 4 * N_HEADS

NA_QROWS = 2
NA_KBLKS = 5
NA_KW = NA_KBLKS * CHUNK
MOE_ROWS = 256
VMEM_LIMIT = 56 * 1024 * 1024


def _cparams(sem, **kw):
    return pltpu.CompilerParams(dimension_semantics=sem, vmem_limit_bytes=VMEM_LIMIT, **kw)


def _sigmoid(z):
    return 0.5 * jnp.tanh(0.5 * z) + 0.5


def _hs(h):
    return slice(h * HEAD_DIM, (h + 1) * HEAD_DIM)


def _heads(head, defer):
    if defer:
        return head
    for h in range(N_HEADS):
        head(h)


def _dot(a, b):
    return jnp.dot(a, b, preferred_element_type=F32)


def _dot_nt(a, b):
    return lax.dot_general(a, b, (((1,), (1,)), ((), ())), preferred_element_type=F32)


def _dot_tn(a, b):
    return lax.dot_general(a, b, (((0,), (0,)), ((), ())), preferred_element_type=F32)


def _inproj_kernel(x_ref, nw_ref, w_ref, wif_ref, wift_ref, cos_ref, sin_ref, qn_ref, kn_ref,
                   proj_ref, g_ref, gt_ref, xn_ref):
    j = pl.program_id(1)

    @pl.when(j == 0)
    def _():
        x = x_ref[...]
        y = x * lax.rsqrt(jnp.mean(x * x, -1, keepdims=True) + EPS) * nw_ref[...]
        xn = y.astype(BF16)
        xn_ref[...] = xn
        g_ref[...] = _dot(xn, wif_ref[...].astype(BF16))
        gt_ref[...] = _dot_nt(wift_ref[...].astype(BF16), xn)

    acc = _dot(xn_ref[...], w_ref[...])

    is_rope = (j == SEG_RQ) | (j == SEG_RK)
    is_norm = (j == SEG_NQ) | (j == SEG_NK)

    @pl.when(is_rope)
    def _():
        scale = jnp.where(j == SEG_RK, HEAD_DIM ** -0.5, 1.0).astype(F32)
        cos = cos_ref[...]
        sin = sin_ref[...]
        for h in range(N_HEADS):
            xh = acc[:, _hs(h)]
            r = xh * cos + pltpu.roll(xh, HEAD_DIM // 2, 1) * sin
            proj_ref[:, _hs(h)] = (r * scale).astype(BF16)

    @pl.when(is_norm)
    def _():
        w = jnp.where(j == SEG_NQ, qn_ref[...] * (HEAD_DIM ** -0.5), kn_ref[...])
        for h in range(N_HEADS):
            xh = acc[:, _hs(h)]
            y = xh * lax.rsqrt(jnp.mean(xh * xh, -1, keepdims=True) + EPS) * w
            proj_ref[:, _hs(h)] = y.astype(BF16)

    @pl.when(jnp.logical_not(is_rope | is_norm))
    def _():
        proj_ref[...] = acc.astype(BF16)


def _inproj(x2d, L, nw, w_main, layer, wif, wift, cos_t, sin_t, qn, kn):
    T = x2d.shape[0]
    tm = min(1024, L)
    nlt = L // tm
    return pl.pallas_call(
        _inproj_kernel,
        grid=(T // tm, N_SEG),
        in_specs=[
            pl.BlockSpec((tm, D_MODEL), lambda i, j: (i, 0)),
            pl.BlockSpec((1, D_MODEL), lambda i, j: (0, 0)),
            pl.BlockSpec((None, D_MODEL, SEG_W), lambda i, j: (layer, 0, j)),
            pl.BlockSpec((D_MODEL, LANES), lambda i, j: (0, 0)),
            pl.BlockSpec((N_GATE_COLS, D_MODEL), lambda i, j: (0, 0)),
            pl.BlockSpec((tm, HEAD_DIM), lambda i, j: (i % nlt, 0)),
            pl.BlockSpec((tm, HEAD_DIM), lambda i, j: (i % nlt, 0)),
            pl.BlockSpec((1, HEAD_DIM), lambda i, j: (0, 0)),
            pl.BlockSpec((1, HEAD_DIM), lambda i, j: (0, 0)),
        ],
        out_specs=[
            pl.BlockSpec((tm, SEG_W), lambda i, j: (i, j)),
            pl.BlockSpec((tm, LANES), lambda i, j: (i, 0)),
            pl.BlockSpec((N_GATE_COLS, tm), lambda i, j: (0, i)),
        ],
        out_shape=[
            jax.ShapeDtypeStruct((T, N_SEG * SEG_W), BF16),
            jax.ShapeDtypeStruct((T, LANES), F32),
            jax.ShapeDtypeStruct((N_GATE_COLS, T), F32),
        ],
        scratch_shapes=[pltpu.VMEM((tm, D_MODEL), BF16)],
        compiler_params=_cparams(("arbitrary", "arbitrary")),
        name="inproj",
    )(x2d, nw, w_main, wif, wift, cos_t, sin_t, qn, kn)


def _ret_state_kernel(k_ref, v_ref, kb_ref, cdb_ref, sb_ref, s_scr):
    c = pl.program_id(1)

    @pl.when(c == 0)
    def _():
        s_scr[...] = jnp.zeros_like(s_scr)

    sb_ref[0] = s_scr[...].astype(BF16)
    kk = (k_ref[...].astype(F32) * kb_ref[...]).astype(BF16)
    v = v_ref[...]
    for h in range(N_HEADS):
        kv = _dot_tn(kk[:, _hs(h)], v[:, _hs(h)])
        s_scr[h] = s_scr[h] * cdb_ref[h:h + 1, :] + kv


def _ret_out_kernel(q_ref, k_ref, v_ref, g_ref, sb_ref, dmat_ref, qf_ref, qb_ref, kf_ref, cdf_ref, nw_ref,
                    o_ref, s_scr, defer=False):
    c = pl.program_id(1)

    @pl.when(c == 0)
    def _():
        s_scr[...] = jnp.zeros_like(s_scr)

    q = q_ref[...]
    k = k_ref[...]
    v = v_ref[...]
    qf32 = q.astype(F32)
    q_fwd = (qf32 * qf_ref[...]).astype(BF16)
    q_bwd = (qf32 * qb_ref[...]).astype(BF16)
    k_end = (k.astype(F32) * kf_ref[...]).astype(BF16)
    g = g_ref[...].astype(F32)

    def head(h):
        hs = _hs(h)
        s = _dot_nt(q[:, hs], k[:, hs]) * dmat_ref[h]
        o = _dot(s.astype(BF16), v[:, hs])
        o = o + _dot(q_fwd[:, hs], s_scr[h].astype(BF16))
        o = o + _dot(q_bwd[:, hs], sb_ref[0, h])
        s_scr[h] = s_scr[h] * cdf_ref[h:h + 1, :] + _dot_tn(k_end[:, hs], v[:, hs])
        y = o * lax.rsqrt(jnp.mean(o * o, -1, keepdims=True) + EPS) * nw_ref[h:h + 1, :]
        gh = g[:, hs]
        o_ref[:, hs] = (y * (gh * _sigmoid(gh))).astype(BF16)

    return _heads(head, defer)


def _retention(proj, B, L, ret_decay, ret_norm_w):
    n = L // CHUNK
    T = B * L
    lg = jax.nn.log_sigmoid(ret_decay.astype(F32))
    idx = jnp.arange(CHUNK, dtype=F32)
    diff = idx[:, None] - idx[None, :]
    dmat = jnp.where(diff >= 0, jnp.exp(jnp.maximum(diff, 0.0) * lg[0][:, None, None]),
                     jnp.exp(jnp.maximum(-diff, 0.0) * lg[1][:, None, None]))

    def lane_tab(e):
        return jnp.repeat(jnp.exp(e).T, HEAD_DIM, axis=1)

    qf_tab = lane_tab((idx + 1.0)[None, :] * lg[0][:, None])
    qb_tab = lane_tab((CHUNK - idx)[None, :] * lg[1][:, None])
    kf_tab = lane_tab((CHUNK - 1.0 - idx)[None, :] * lg[0][:, None])
    kb_tab = lane_tab(idx[None, :] * lg[1][:, None])
    cdf = jnp.broadcast_to(jnp.exp(CHUNK * lg[0])[:, None], (N_HEADS, HEAD_DIM))
    cdb = jnp.broadcast_to(jnp.exp(CHUNK * lg[1])[:, None], (N_HEADS, HEAD_DIM))

    def seg(s, rev=False):
        if rev:
            return pl.BlockSpec((CHUNK, SEG_W), lambda b, c: (b * n + n - 1 - c, s))
        return pl.BlockSpec((CHUNK, SEG_W), lambda b, c: (b * n + c, s))

    full = lambda shape: pl.BlockSpec(shape, lambda b, c: (0,) * len(shape))
    st_shape = (1, N_HEADS, HEAD_DIM, HEAD_DIM)

    state = dict(
        kernel=_ret_state_kernel,
        in_specs=[seg(SEG_RK, True), seg(SEG_RV, True), full((CHUNK, MIX_W)), full((N_HEADS, HEAD_DIM))],
        args=[proj, proj, kb_tab, cdb],
        out_specs=[pl.BlockSpec(st_shape, lambda b, c: (b * n + n - 1 - c, 0, 0, 0))],
        out_shape=[jax.ShapeDtypeStruct((B * n, N_HEADS, HEAD_DIM, HEAD_DIM), BF16)],
        scratch=[pltpu.VMEM((N_HEADS, HEAD_DIM, HEAD_DIM), F32)])

    def out(sb):
        return dict(
            kernel=_ret_out_kernel, per_head=True,
            in_specs=[seg(SEG_RQ), seg(SEG_RK), seg(SEG_RV), seg(SEG_RG),
                      pl.BlockSpec(st_shape, lambda b, c: (b * n + c, 0, 0, 0)),
                      full((N_HEADS, CHUNK, CHUNK)), full((CHUNK, MIX_W)), full((CHUNK, MIX_W)),
                      full((CHUNK, MIX_W)), full((N_HEADS, HEAD_DIM)), full((N_HEADS, HEAD_DIM))],
            args=[proj, proj, proj, proj, sb, dmat, qf_tab, qb_tab, kf_tab, cdf, ret_norm_w.astype(F32)],
            out_specs=[pl.BlockSpec((CHUNK, MIX_W), lambda b, c: (b * n + c, 0))],
            out_shape=[jax.ShapeDtypeStruct((T, MIX_W), BF16)],
            scratch=[pltpu.VMEM((N_HEADS, HEAD_DIM, HEAD_DIM), F32)])

    return state, out


def _na_index_tables(nqb):
    rows = NA_QROWS * nqb
    qi = np.arange(CHUNK)
    ki = np.arange(NA_KW)

    def one(qb):
        kb = NA_QROWS * int(np.clip(qb - 2, 0, nqb - NA_KBLKS))
        r = (NA_QROWS * qb + qi // GRID_W)[:, None]
        qc = (qi % GRID_W)[:, None]
        kr = (kb + ki // GRID_W)[None, :]
        kc = (ki % GRID_W)[None, :]
        rs = np.clip(r - WIN_ROWS // 2, 0, rows - WIN_ROWS)
        cs = np.clip(qc - WIN_COLS // 2, 0, GRID_W - WIN_COLS)
        valid = (kr >= rs) & (kr < rs + WIN_ROWS) & (kc >= cs) & (kc < cs + WIN_COLS)
        dr = np.clip(kr - r + WIN_ROWS - 1, 0, 2 * WIN_ROWS - 2)
        dc = np.clip(kc - qc + WIN_COLS - 1, 0, 2 * WIN_COLS - 2)
        return dr + 0 * dc, dc + 0 * dr, valid

    reps = [0, 1, 2, nqb - 2, nqb - 1]
    tabs = [one(qb) for qb in reps]
    for qb in range(2, nqb - 2):
        t = one(qb)
        assert all(np.array_equal(a, b) for a, b in zip(t, tabs[2]))
    return tuple(np.stack([t[i] for t in tabs]) for i in range(3))


def _na_kernel(q_ref, k0, k1, k2, k3, k4, v0, v1, v2, v3, v4, bias_ref, o_ref, defer=False):
    k_refs = (k0, k1, k2, k3, k4)
    v_refs = (v0, v1, v2, v3, v4)

    def head(h):
        hs = _hs(h)
        q = q_ref[:, hs]
        k_all = jnp.concatenate([r[:, hs] for r in k_refs], 0)
        v_all = jnp.concatenate([r[:, hs] for r in v_refs], 0)
        s = _dot_nt(q, k_all) + bias_ref[0, h]
        p = jnp.exp(s - jnp.max(s, -1, keepdims=True))
        o = _dot(p.astype(BF16), v_all)
        o_ref[:, hs] = (o / jnp.sum(p, -1, keepdims=True)).astype(BF16)

    return _heads(head, defer)


def _neighborhood(proj, B, L, rpb):
    nqb = L // CHUNK
    assert nqb >= NA_KBLKS and L % (GRID_W * NA_QROWS) == 0
    T = B * L
    dr, dc, valid = _na_index_tables(nqb)
    n_cls = dr.shape[0]
    kr_n = NA_KW // GRID_W
    dr6 = dr.reshape(n_cls, NA_QROWS, GRID_W, kr_n, GRID_W)
    dc6 = dc.reshape(n_cls, NA_QROWS, GRID_W, kr_n, GRID_W)
    dr_s = dr6[:, :, 0, :, 0]
    dc_s = dc6[0, 0, :, 0, :]
    assert np.array_equal(dr6, np.broadcast_to(dr_s[:, :, None, :, None], dr6.shape))
    assert np.array_equal(dc6, np.broadcast_to(dc_s[None, None, :, None, :], dc6.shape))
    sel_r = (dr_s[..., None] == np.arange(2 * WIN_ROWS - 1)).astype(np.float32)
    sel_c = (np.arange(2 * WIN_COLS - 1)[:, None, None] == dc_s[None]).astype(np.float32)
    hp = lax.Precision.HIGHEST
    t = jnp.einsum('cajr,hrd->cajhd', sel_r, rpb.astype(F32), precision=hp)
    bias = jnp.einsum('cajhd,dqk->chaqjk', t, sel_c, precision=hp).reshape(n_cls, N_HEADS, CHUNK, NA_KW)
    bias = jnp.where(jnp.asarray(valid)[:, None], bias, NEG)

    def kspec(s, j):
        return pl.BlockSpec((CHUNK, SEG_W),
                            lambda b, qb: (b * nqb + jnp.clip(qb - 2, 0, nqb - NA_KBLKS) + j, s))

    def cls(qb):
        return jnp.where(qb < 2, qb, jnp.where(qb >= nqb - 2, qb - (nqb - 5), 2))

    return dict(
        kernel=_na_kernel, per_head=True,
        in_specs=[pl.BlockSpec((CHUNK, SEG_W), lambda b, qb: (b * nqb + qb, SEG_NQ))]
        + [kspec(SEG_NK, j) for j in range(NA_KBLKS)]
        + [kspec(SEG_NV, j) for j in range(NA_KBLKS)]
        + [pl.BlockSpec((1, N_HEADS, CHUNK, NA_KW), lambda b, qb: (cls(qb), 0, 0, 0))],
        args=[proj] * (1 + 2 * NA_KBLKS) + [bias],
        out_specs=[pl.BlockSpec((CHUNK, MIX_W), lambda b, qb: (b * nqb + qb, 0))],
        out_shape=[jax.ShapeDtypeStruct((T, MIX_W), BF16)],
        scratch=[])


HALO = 16


def _log_sigmoid(x):
    return jnp.minimum(x, 0.0) - jnp.log1p(jnp.exp(-jnp.abs(x)))


def _conv_silu(x_ref, prev_ref, next_ref, w_ref, col0, c, n):
    x = x_ref[...].astype(F32)
    row = lax.broadcasted_iota(jnp.int32, x.shape, 0)
    prev_row = prev_ref[HALO - 1:HALO, :].astype(F32) * (c > 0).astype(F32)
    next_row = next_ref[0:1, :].astype(F32) * (c < n - 1).astype(F32)
    x_prev = jnp.where(row == 0, prev_row, pltpu.roll(x, 1, 0))
    x_next = jnp.where(row == CHUNK - 1, next_row, pltpu.roll(x, CHUNK - 1, 0))
    cs = slice(col0, col0 + MIX_W)
    y = x_prev * w_ref[0:1, cs] + x * w_ref[1:2, cs] + x_next * w_ref[2:3, cs]
    return y * _sigmoid(y)


def _tri():
    r = lax.broadcasted_iota(jnp.int32, (CHUNK, CHUNK), 0)
    c = lax.broadcasted_iota(jnp.int32, (CHUNK, CHUNK), 1)
    return (c <= r).astype(F32), (c >= r).astype(F32)


def _hp_dot(a, b):
    return jnp.dot(a, b, preferred_element_type=F32, precision=lax.Precision.HIGHEST)


FWD_LANE = 2 * N_HEADS
BWD_LANE = 3 * N_HEADS


def _gate_dense(gc_ref, gr_ref, bc_ref, br_ref):
    low, up = _tri()
    g_col = gc_ref[...] + bc_ref[...]
    lf_col = _log_sigmoid(g_col)
    lf_row = _log_sigmoid(gr_ref[...] + br_ref[...])
    b_col = _hp_dot(low, lf_col)
    b_row = _hp_dot(lf_row, up)
    tot_row = b_col[CHUNK - 1:CHUNK, :]
    tot_col = b_row[:, CHUNK - 1:CHUNK]
    lane = lax.broadcasted_iota(jnp.int32, (CHUNK, LANES), 1)
    row = lax.broadcasted_iota(jnp.int32, (N_GATE_COLS, CHUNK), 0)
    bb_col = jnp.where(lane >= BWD_LANE, tot_row - b_col + lf_col, b_col)
    bb_row = jnp.where(row >= BWD_LANE, tot_col - b_row + lf_row, b_row)
    ck = pltpu.roll(g_col, FWD_LANE, 1) - bb_col
    a = tot_row + ck
    m_loc = jnp.max(a, 0, keepdims=True)
    ea = jnp.exp(a - m_loc)
    return bb_row, ck, tot_row, m_loc, ea


def _lane_bcast(x, l):
    return jnp.broadcast_to(x[:, l:l + 1], (x.shape[0], HEAD_DIM))


def _ml_state_update(kc_h, ct_loc_fn, ea, tot_row, m_loc, l, c_scr, n_scr, m_scr, h):
    m_p = m_scr[h:h + 1, 0:1]
    g_tot = tot_row[:, l:l + 1]
    m_l = m_loc[:, l:l + 1]
    kw = kc_h * _lane_bcast(ea, l)
    ct_loc = ct_loc_fn(kw.astype(BF16))
    n_loc = jnp.sum(kw, 0, keepdims=True)
    m_new = jnp.maximum(g_tot + m_p, m_l)
    sp = jnp.exp(g_tot + m_p - m_new)
    sl = jnp.exp(m_l - m_new)
    c_scr[h] = sp * c_scr[h] + sl * ct_loc
    n_scr[h:h + 1, :] = sp * n_scr[h:h + 1, :] + sl * n_loc
    m_scr[h:h + 1, :] = jnp.broadcast_to(m_new, (1, HEAD_DIM))


def _ml_init(c, c_scr, n_scr, m_scr):
    @pl.when(c == 0)
    def _():
        c_scr[...] = jnp.zeros_like(c_scr)
        n_scr[...] = jnp.zeros_like(n_scr)
        m_scr[...] = jnp.full_like(m_scr, NEG)


def _ml_state_kernel(k_ref, kp_ref, kn_ref, v_ref, gc_ref, gr_ref, cw_ref, bc_ref, br_ref,
                     cb_ref, nm_ref, c_scr, n_scr, m_scr, *, n):
    step = pl.program_id(1)
    _ml_init(step, c_scr, n_scr, m_scr)
    c = n - 1 - step
    cb_ref[0] = c_scr[...].astype(BF16)
    nm_ref[0, 0:N_HEADS, :] = n_scr[...]
    nm_ref[0, N_HEADS:2 * N_HEADS, :] = m_scr[...]

    kc = _conv_silu(k_ref, kp_ref, kn_ref, cw_ref, MIX_W, c, n) * (HEAD_DIM ** -0.5)
    v = v_ref[...]
    _, _, tot_row, m_loc, ea = _gate_dense(gc_ref, gr_ref, bc_ref, br_ref)
    for h in range(N_HEADS):
        v_h = v[:, _hs(h)]
        _ml_state_update(kc[:, _hs(h)], lambda kw: _dot_tn(v_h, kw), ea, tot_row, m_loc, BWD_LANE + h,
                         c_scr, n_scr, m_scr, h)


def _ml_direction(st, mask, b_q, ck_b, m_p, qn, vt_bf, ct_bf, q_bf):
    dmat = jnp.where(mask, b_q + ck_b, NEG)
    inter = b_q + m_p
    m_row = jnp.maximum(jnp.max(dmat, 0, keepdims=True), inter)
    s = st * jnp.exp(dmat - m_row)
    e_int = jnp.exp(inter - m_row)
    num = _dot(vt_bf, s.astype(BF16)) + e_int * _dot_nt(ct_bf, q_bf)
    den = jnp.sum(s, 0, keepdims=True) + e_int * qn
    return num / jnp.maximum(jnp.abs(den), jnp.exp(-m_row))


def _ml_out_kernel(q_ref, qp_ref, qn_ref, k_ref, kp_ref, kn_ref, v_ref, o_ref_in, gc_ref, gr_ref,
                   cb_ref, nm_ref, cw_ref, bc_ref, br_ref, nwt_ref,
                   out_ref, c_scr, n_scr, m_scr, *, n, defer=False):
    c = pl.program_id(1)
    _ml_init(c, c_scr, n_scr, m_scr)
    qc = _conv_silu(q_ref, qp_ref, qn_ref, cw_ref, 0, c, n)
    kc = _conv_silu(k_ref, kp_ref, kn_ref, cw_ref, MIX_W, c, n) * (HEAD_DIM ** -0.5)
    q_bf = qc.astype(BF16)
    k_bf = kc.astype(BF16)
    v = v_ref[...]
    og = o_ref_in[...].astype(F32)
    bb_row, ck, tot_row, m_loc, ea = _gate_dense(gc_ref, gr_ref, bc_ref, br_ref)
    key = lax.broadcasted_iota(jnp.int32, (CHUNK, CHUNK), 0)
    qry = lax.broadcasted_iota(jnp.int32, (CHUNK, CHUNK), 1)
    causal = key <= qry
    anti = key >= qry
    pad = jnp.zeros((8 - 2, HEAD_DIM), F32)

    def head(h):
        hs = _hs(h)
        lf, lb = FWD_LANE + h, BWD_LANE + h
        qh = q_bf[:, hs]
        vt_bf = v[:, hs].astype(F32).T.astype(BF16)
        st = _dot_nt(k_bf[:, hs], qh)
        n_prev = jnp.concatenate([n_scr[h:h + 1, :], nm_ref[0, h:h + 1, :], pad], 0).astype(BF16)
        qn = _dot_nt(n_prev, qh)
        h_f = _ml_direction(st, causal, bb_row[lf:lf + 1, :], _lane_bcast(ck, lf), m_scr[h:h + 1, 0:1],
                            qn[0:1, :], vt_bf, c_scr[h].astype(BF16), qh)
        h_b = _ml_direction(st, anti, bb_row[lb:lb + 1, :], _lane_bcast(ck, lb),
                            nm_ref[0, N_HEADS + h:N_HEADS + h + 1, 0:1], qn[1:2, :], vt_bf, cb_ref[0, h], qh)
        _ml_state_update(kc[:, hs], lambda kw: _dot(vt_bf, kw), ea, tot_row, m_loc, lf, c_scr, n_scr, m_scr, h)
        ot = h_f + h_b
        yt = ot * lax.rsqrt(jnp.mean(ot * ot, 0, keepdims=True) + EPS) * nwt_ref[h]
        out_ref[:, hs] = (yt.T * _sigmoid(og[:, hs])).astype(BF16)

    return _heads(head, defer)


def _mlstm(proj, gates, gates_t, B, L, conv_w, ig_b, fg_b, norm_w):
    n = L // CHUNK
    T = B * L
    hb = CHUNK // HALO
    bias = jnp.concatenate([ig_b.astype(F32).reshape(-1), fg_b.astype(F32).reshape(-1)])
    bias_col = jnp.zeros((1, LANES), F32).at[0, :N_GATE_COLS].set(bias)
    bias_row = jnp.broadcast_to(bias[:, None], (N_GATE_COLS, CHUNK))
    cw = conv_w.astype(F32)
    norm_wt = jnp.broadcast_to(norm_w.astype(F32)[:, :, None], (N_HEADS, HEAD_DIM, CHUNK))

    def seg(s, rev):
        if rev:
            return pl.BlockSpec((CHUNK, SEG_W), lambda b, c: (b * n + n - 1 - c, s))
        return pl.BlockSpec((CHUNK, SEG_W), lambda b, c: (b * n + c, s))

    def chunk_of(c, rev):
        return n - 1 - c if rev else c

    def prev_spec(s, rev):
        return pl.BlockSpec((HALO, SEG_W),
                            lambda b, c: (b * n * hb + jnp.maximum(chunk_of(c, rev) * hb - 1, 0), s))

    def next_spec(s, rev):
        return pl.BlockSpec((HALO, SEG_W),
                            lambda b, c: (b * n * hb + jnp.minimum((chunk_of(c, rev) + 1) * hb, n * hb - 1), s))

    def gcol_spec(rev):
        return pl.BlockSpec((CHUNK, LANES), lambda b, c: (b * n + chunk_of(c, rev), 0))

    def grow_spec(rev):
        return pl.BlockSpec((N_GATE_COLS, CHUNK), lambda b, c: (0, b * n + chunk_of(c, rev)))

    full = lambda shape: pl.BlockSpec(shape, lambda b, c: (0,) * len(shape))
    st_shape = (1, N_HEADS, HEAD_DIM, HEAD_DIM)
    nm_shape = (1, 2 * N_HEADS, HEAD_DIM)
    scratch = [pltpu.VMEM((N_HEADS, HEAD_DIM, HEAD_DIM), F32), pltpu.VMEM((N_HEADS, HEAD_DIM), F32),
               pltpu.VMEM((N_HEADS, HEAD_DIM), F32)]
    consts = [full((ML_CONV_W, 2 * MIX_W)), full((1, LANES)), full((N_GATE_COLS, CHUNK))]

    state = dict(
        kernel=functools.partial(_ml_state_kernel, n=n),
        in_specs=[seg(SEG_MK, True), prev_spec(SEG_MK, True), next_spec(SEG_MK, True), seg(SEG_MV, True),
                  gcol_spec(True), grow_spec(True)] + consts,
        args=[proj, proj, proj, proj, gates, gates_t, cw, bias_col, bias_row],
        out_specs=[pl.BlockSpec(st_shape, lambda b, c: (b * n + n - 1 - c, 0, 0, 0)),
                   pl.BlockSpec(nm_shape, lambda b, c: (b * n + n - 1 - c, 0, 0))],
        out_shape=[jax.ShapeDtypeStruct((B * n, N_HEADS, HEAD_DIM, HEAD_DIM), BF16),
                   jax.ShapeDtypeStruct((B * n, 2 * N_HEADS, HEAD_DIM), F32)],
        scratch=scratch)

    def out(cb, nm):
        return dict(
            kernel=functools.partial(_ml_out_kernel, n=n), per_head=True,
            in_specs=[seg(SEG_MQ, False), prev_spec(SEG_MQ, False), next_spec(SEG_MQ, False),
                      seg(SEG_MK, False), prev_spec(SEG_MK, False), next_spec(SEG_MK, False),
                      seg(SEG_MV, False), seg(SEG_MO, False), gcol_spec(False), grow_spec(False),
                      pl.BlockSpec(st_shape, lambda b, c: (b * n + c, 0, 0, 0)),
                      pl.BlockSpec(nm_shape, lambda b, c: (b * n + c, 0, 0))]
            + consts + [full((N_HEADS, HEAD_DIM, CHUNK))],
            args=[proj, proj, proj, proj, proj, proj, proj, proj, gates, gates_t, cb, nm,
                  cw, bias_col, bias_row, norm_wt],
            out_specs=[pl.BlockSpec((CHUNK, MIX_W), lambda b, c: (b * n + c, 0))],
            out_shape=[jax.ShapeDtypeStruct((T, MIX_W), BF16)],
            scratch=scratch)

    return state, out


def _fused_call(parts, grid, name):
    n_in = [len(p["in_specs"]) for p in parts]
    n_out = [len(p["out_specs"]) for p in parts]
    n_scr = [len(p["scratch"]) for p in parts]

    def body(*refs):
        ins, outs, scr = refs[:sum(n_in)], refs[sum(n_in):sum(n_in) + sum(n_out)], refs[sum(n_in) + sum(n_out):]
        i = o = k = 0
        heads = []
        for p, a, b, c in zip(parts, n_in, n_out, n_scr):
            args = (*ins[i:i + a], *outs[o:o + b], *scr[k:k + c])
            if p.get("per_head"):
                heads.append(p["kernel"](*args, defer=True))
            else:
                p["kernel"](*args)
            i, o, k = i + a, o + b, k + c
        for h in range(N_HEADS):
            for head in heads:
                head(h)

    return pl.pallas_call(
        body,
        grid=grid,
        in_specs=[s for p in parts for s in p["in_specs"]],
        out_specs=[s for p in parts for s in p["out_specs"]],
        out_shape=[s for p in parts for s in p["out_shape"]],
        scratch_shapes=[s for p in parts for s in p["scratch"]],
        compiler_params=_cparams(("arbitrary", "arbitrary")),
        name=name,
    )(*[a for p in parts for a in p["args"]])


def _mixers(proj, gates, gates_t, B, L, p):
    grid = (B, L // CHUNK)
    ret_state, ret_out = _retention(proj, B, L, p["ret_decay"], p["ret_norm_w"])
    ml_state, ml_out = _mlstm(proj, gates, gates_t, B, L, p["ml_conv"], p["ml_igate_b"], p["ml_fgate_b"],
                              p["ml_norm_w"])
    sb, cb, nm = _fused_call([ret_state, ml_state], grid, "mixer_states")
    na = _neighborhood(proj, B, L, p["na_rpb"])
    return _fused_call([ret_out(sb), na, ml_out(cb, nm)], grid, "mixer_outputs")


def _branch_kernel(oret_ref, ona_ref, oml_ref, ga_ref, gb_ref, gc_ref, wb_ref, bgb_ref, m_ref):
    o_refs = (oret_ref, ona_ref, oml_ref)
    g_refs = (ga_ref, gb_ref, gc_ref)
    for n in range(D_MODEL // SEG_W):
        cs = slice(n * SEG_W, (n + 1) * SEG_W)
        acc = None
        for i in range(N_BRANCH):
            z = g_refs[i][:, cs].astype(F32) + bgb_ref[i:i + 1, cs]
            term = _sigmoid(z) * _dot(o_refs[i][...], wb_ref[i, :, cs])
            acc = term if acc is None else acc + term
        m_ref[:, cs] = acc.astype(BF16)


def _outproj_router_kernel(x_ref, m_ref, wout_ref, n2w_ref, rwhl_ref, rb_ref,
                           h_ref, hn_ref, route_ref, cnt_ref, cnt_scr):
    h = x_ref[...] + _dot(m_ref[...], wout_ref[...])
    h_ref[...] = h
    hn = h * lax.rsqrt(jnp.mean(h * h, -1, keepdims=True) + EPS) * n2w_ref[...]
    bits = lax.bitcast_convert_type(hn, jnp.uint32)
    bf = (bits + (jnp.uint32(0x7FFF) + ((bits >> 16) & jnp.uint32(1)))) >> 16
    hn_ref[...] = bf[:, :D_MODEL // 2] | (bf[:, D_MODEL // 2:] << 16)

    hn_hi = hn.astype(BF16)
    hn_lo = (hn - hn_hi.astype(F32)).astype(BF16)
    hi = _dot(hn_hi, rwhl_ref[...])
    logits = (hi[:, :LANES] + _dot(hn_lo, rwhl_ref[:, :LANES]) + hi[:, LANES:]) + rb_ref[...]
    lane_i = lax.broadcasted_iota(jnp.int32, logits.shape, 1)
    lane = lane_i.astype(F32)
    lane_grp = jnp.right_shift(lane_i - N_GROUPS, 3).astype(F32)
    gmask = lane_i < N_GROUPS
    lg = jnp.where(gmask, logits, NEG)
    mg = jnp.max(lg, -1, keepdims=True)
    p_top = 1.0 / jnp.sum(jnp.where(gmask, jnp.exp(lg - mg), 0.0), -1, keepdims=True)
    grp = jnp.min(jnp.where(lg == mg, lane, float(LANES)), -1, keepdims=True)
    emask = (lane_i >= N_GROUPS) & (lane_i < N_GROUPS + N_EXPERTS) & (lane_grp == grp)
    le = jnp.where(emask, logits, NEG)
    m1 = jnp.max(le, -1, keepdims=True)
    i1 = jnp.min(jnp.where(le == m1, lane, float(LANES)), -1, keepdims=True)
    le2 = jnp.where(lane == i1, NEG, le)
    m2 = jnp.max(le2, -1, keepdims=True)
    i2 = jnp.min(jnp.where(le2 == m2, lane, float(LANES)), -1, keepdims=True)
    e2 = jnp.exp(m2 - m1)
    gate1 = p_top / (1.0 + e2)
    gate2 = p_top * e2 / (1.0 + e2)

    @pl.when(pl.program_id(0) == 0)
    def _():
        cnt_scr[...] = jnp.zeros_like(cnt_scr)

    tm = logits.shape[0]
    hit1 = lane == i1
    hit2 = lane == i2
    onehot = jnp.where(hit1 | hit2, 1.0, 0.0)
    r_i = lax.broadcasted_iota(jnp.int32, (tm, tm), 0)
    c_i = lax.broadcasted_iota(jnp.int32, (tm, tm), 1)
    before = jnp.where(c_i < r_i, 1.0, 0.0).astype(BF16)
    prior = _dot(before, onehot.astype(BF16)) + cnt_scr[...]
    rank1 = jnp.sum(jnp.where(hit1, prior, 0.0), -1, keepdims=True)
    rank2 = jnp.sum(jnp.where(hit2, prior, 0.0), -1, keepdims=True)
    cnt = cnt_scr[...] + jnp.sum(onehot, 0, keepdims=True)
    cnt_scr[...] = cnt
    cnt_ref[...] = jnp.broadcast_to(cnt, cnt_ref.shape)

    cols = (i1 - N_GROUPS, i2 - N_GROUPS, gate1, gate2, rank1, rank2)
    route = jnp.zeros_like(logits)
    for ci, val in enumerate(cols):
        route = jnp.where(lane_i == ci, val, route)
    route_ref[...] = route


MERGE_ROWS = 512


def _merge(x2d, o_ret, o_na, o_ml, proj, wb, bgb, wout, n2w, rw_hi, rw_lo, rb):
    T = x2d.shape[0]
    tm = MERGE_ROWS
    full = lambda shape: pl.BlockSpec(shape, lambda i: (0,) * len(shape), pipeline_mode=pl.Buffered(1))
    row = lambda w: pl.BlockSpec((tm, w), lambda i: (i, 0))
    gate = lambda s: pl.BlockSpec((tm, D_MODEL), lambda i: (i, s // 2))
    merged = pl.pallas_call(
        _branch_kernel,
        grid=(T // tm,),
        in_specs=[row(MIX_W), row(MIX_W), row(MIX_W), gate(SEG_GA), gate(SEG_GB), gate(SEG_GC),
                  full((N_BRANCH, MIX_W, D_MODEL)), full((N_BRANCH, D_MODEL))],
        out_specs=row(D_MODEL),
        out_shape=jax.ShapeDtypeStruct((T, D_MODEL), BF16),
        compiler_params=_cparams(("arbitrary",)),
        name="branch_merge",
    )(o_ret, o_na, o_ml, proj, proj, proj, wb, bgb)
    return pl.pallas_call(
        _outproj_router_kernel,
        grid=(T // tm,),
        in_specs=[row(D_MODEL), row(D_MODEL), full((D_MODEL, D_MODEL)), full((1, D_MODEL)),
                  full((D_MODEL, 2 * LANES)), full((1, LANES))],
        out_specs=[row(D_MODEL), row(D_MODEL // 2), row(LANES), pl.BlockSpec((8, LANES), lambda i: (0, 0))],
        out_shape=[jax.ShapeDtypeStruct((T, D_MODEL), F32), jax.ShapeDtypeStruct((T, D_MODEL // 2), jnp.uint32),
                   jax.ShapeDtypeStruct((T, LANES), F32), jax.ShapeDtypeStruct((8, LANES), F32)],
        scratch_shapes=[pltpu.VMEM((1, LANES), F32)],
        compiler_params=_cparams(("arbitrary",)),
        name="outproj_router",
    )(x2d, merged, wout, n2w, jnp.concatenate([rw_hi, rw_lo], 1), rb)


def _row_copy(src_hbm, row, dst, r, sem):
    return pltpu.make_async_copy(src_hbm.at[pl.ds(row, 1), :], dst.at[pl.ds(r, 1), :], sem)


def _ffn_kernel(blk_e_ref, n_used_ref, src_ref, hn_hbm, wg_ref, wu_ref, wd_ref, y_ref, xbuf0, xbuf1, a_scr, sem):
    i = pl.program_id(0)
    n_used = n_used_ref[0]
    slot = i % 2
    bufs = (xbuf0, xbuf1)

    def wait_block(s):
        pltpu.make_async_copy(hn_hbm.at[pl.ds(0, MOE_ROWS), :], bufs[s], sem.at[s]).wait()

    @pl.when(i == 0)
    def _():
        def body(r, carry):
            _row_copy(hn_hbm, src_ref[r], xbuf0, r, sem.at[0]).start()
            return carry
        lax.fori_loop(0, MOE_ROWS, body, 0, unroll=8)

    def load_x(s):
        w = bufs[s][...]
        lo = lax.bitcast_convert_type(w << 16, F32)
        hi = lax.bitcast_convert_type(w & jnp.uint32(0xFFFF0000), F32)
        return jnp.concatenate([lo, hi], 1).astype(BF16)

    def gate_and_gather(s):
        wait_block(s)
        for r in range(MOE_ROWS):
            _row_copy(hn_hbm, src_ref[(i + 1) * MOE_ROWS + r], bufs[1 - s], r, sem.at[1 - s]).start()
        a_scr[...] = _dot(load_x(s), wg_ref[0])

    def up_and_down(s):
        a = a_scr[...]
        hid = (a * _sigmoid(a)) * _dot(load_x(s), wu_ref[0])
        y_ref[...] = _dot(hid.astype(BF16), wd_ref[0])

    for s in range(2):
        @pl.when((i < n_used) & (slot == s))
        def _():
            gate_and_gather(s)

            def rest(_, carry):
                up_and_down(s)
                return carry
            lax.fori_loop(0, jnp.minimum(n_used, 1), rest, 0)

        @pl.when((i == n_used) & (slot == s))
        def _():
            wait_block(s)

    @pl.when(i >= n_used)
    def _():
        y_ref[...] = jnp.zeros_like(y_ref)


def _expert_ffn(hn, src, blk_e, n_used, wg, wu, wd):
    n_blocks = src.shape[0] // MOE_ROWS
    grid_spec = pltpu.PrefetchScalarGridSpec(
        num_scalar_prefetch=3,
        grid=(n_blocks,),
        in_specs=[
            pl.BlockSpec(memory_space=pl.ANY),
            pl.BlockSpec((1, D_MODEL, D_EXPERT), lambda i, be, nu, sr: (be[i], 0, 0)),
            pl.BlockSpec((1, D_MODEL, D_EXPERT), lambda i, be, nu, sr: (be[i], 0, 0)),
            pl.BlockSpec((1, D_EXPERT, D_MODEL), lambda i, be, nu, sr: (be[i], 0, 0)),
        ],
        out_specs=pl.BlockSpec((MOE_ROWS, D_MODEL), lambda i, be, nu, sr: (i, 0)),
        scratch_shapes=[pltpu.VMEM((MOE_ROWS, D_MODEL // 2), jnp.uint32),
                        pltpu.VMEM((MOE_ROWS, D_MODEL // 2), jnp.uint32),
                        pltpu.VMEM((MOE_ROWS, D_EXPERT), F32), pltpu.SemaphoreType.DMA((2,))],
    )
    return pl.pallas_call(
        _ffn_kernel,
        grid_spec=grid_spec,
        out_shape=jax.ShapeDtypeStruct((n_blocks * MOE_ROWS, D_MODEL), F32),
        compiler_params=_cparams(("arbitrary",), disable_bounds_checks=True),
        name="expert_ffn",
    )(blk_e, n_used, src, hn, wg, wu, wd)


COMBINE_ROWS = 256


def _combine_kernel(dest_ref, h_ref, route_ref, yb_hbm, out_ref, ybuf, sem):
    i = pl.program_id(0)
    slot = i % 2

    def start_row(tile, r, to_slot):
        a = (tile * COMBINE_ROWS + r) * TOP_K_INNER
        for k in range(TOP_K_INNER):
            _row_copy(yb_hbm, dest_ref[a + k], ybuf.at[to_slot, k], r, sem.at[to_slot]).start()

    @pl.when(i == 0)
    def _():
        def body(r, carry):
            start_row(0, r, 0)
            return carry
        lax.fori_loop(0, COMBINE_ROWS, body, 0, unroll=8)

    @pl.when(i + 1 < pl.num_programs(0))
    def _():
        for r in range(COMBINE_ROWS):
            start_row(i + 1, r, 1 - slot)

    for k in range(TOP_K_INNER):
        pltpu.make_async_copy(yb_hbm.at[pl.ds(0, COMBINE_ROWS), :], ybuf.at[slot, k], sem.at[slot]).wait()

    route = route_ref[...]
    out_ref[...] = h_ref[...] + route[:, 2:3] * ybuf[slot, 0] + route[:, 3:4] * ybuf[slot, 1]


def _combine(h, route, yb, dest):
    T = h.shape[0]
    grid_spec = pltpu.PrefetchScalarGridSpec(
        num_scalar_prefetch=1,
        grid=(T // COMBINE_ROWS,),
        in_specs=[
            pl.BlockSpec((COMBINE_ROWS, D_MODEL), lambda i, d: (i, 0)),
            pl.BlockSpec((COMBINE_ROWS, LANES), lambda i, d: (i, 0)),
            pl.BlockSpec(memory_space=pl.ANY),
        ],
        out_specs=pl.BlockSpec((COMBINE_ROWS, D_MODEL), lambda i, d: (i, 0)),
        scratch_shapes=[pltpu.VMEM((2, TOP_K_INNER, COMBINE_ROWS, D_MODEL), F32), pltpu.SemaphoreType.DMA((2,))],
    )
    return pl.pallas_call(
        _combine_kernel,
        grid_spec=grid_spec,
        out_shape=jax.ShapeDtypeStruct((T, D_MODEL), F32),
        compiler_params=_cparams(("arbitrary",), disable_bounds_checks=True),
        name="moe_combine",
    )(dest, h, route, yb)


def _moe(h, hn, route, counts, wg, wu, wd, layer):
    T = h.shape[0]
    n_assign = T * TOP_K_INNER
    expert = route[:, 0:2].astype(jnp.int32)
    rank = route[:, 4:6].astype(jnp.int32)
    cnt = counts[0, N_GROUPS:N_GROUPS + N_EXPERTS].astype(jnp.int32)
    padded = (cnt + MOE_ROWS - 1) // MOE_ROWS * MOE_ROWS
    pend = jnp.cumsum(padded)
    pstart = pend - padded
    dest = (pstart[expert] + rank).reshape(-1)
    n_blocks = (n_assign + N_EXPERTS * (MOE_ROWS - 1) + MOE_ROWS - 1) // MOE_ROWS + 1
    blk_start = jnp.arange(n_blocks, dtype=jnp.int32) * MOE_ROWS
    blk_e = jnp.minimum(jnp.sum((pend[None, :] <= blk_start[:, None]).astype(jnp.int32), 1), N_EXPERTS - 1)
    blk_e = blk_e + layer * N_EXPERTS
    n_used = (pend[-1] // MOE_ROWS).astype(jnp.int32).reshape(1)
    flat_t = jnp.repeat(jnp.arange(T, dtype=jnp.int32), TOP_K_INNER)
    src = jnp.zeros((n_blocks * MOE_ROWS,), jnp.int32).at[dest].set(flat_t)
    yb = _expert_ffn(hn, src, blk_e, n_used, wg, wu, wd)
    return _combine(h, route, yb, dest)


def _rope_tables(L):
    freqs = ROPE_BASE ** (-jnp.arange(0, HEAD_DIM, 2, dtype=F32) / HEAD_DIM)
    ang = jnp.arange(L, dtype=F32)[:, None] * freqs[None]
    cos, sin = jnp.cos(ang), jnp.sin(ang)
    return jnp.concatenate([cos, cos], -1), jnp.concatenate([-sin, sin], -1)


N_MIX_SEG = 11


REORDER_COLS = SEG_W // 2
N_BRANCH_BLK = N_BRANCH * D_MODEL // REORDER_COLS


def _reorder_kernel(wm_ref, wg_ref, o_ref):
    j = pl.program_id(1)

    @pl.when(j < N_BRANCH_BLK)
    def _():
        o_ref[...] = wg_ref[0].T.astype(BF16)

    @pl.when(j >= N_BRANCH_BLK)
    def _():
        o_ref[...] = wm_ref[...].T.astype(BF16)


def _reorder_w_in(w_in_t):
    depth = w_in_t.shape[0]
    gate_row0 = N_MIX_SEG * SEG_W + N_GATE_COLS
    blk = (None, REORDER_COLS, D_MODEL)
    return pl.pallas_call(
        _reorder_kernel,
        grid=(depth, N_SEG * SEG_W // REORDER_COLS),
        in_specs=[pl.BlockSpec(blk, lambda l, j: (l, jnp.maximum(j - N_BRANCH_BLK, 0), 0)),
                  pl.BlockSpec((pl.Element(1), pl.Element(REORDER_COLS), pl.Element(D_MODEL)),
                               lambda l, j: (l, pl.multiple_of(
                                   gate_row0 + jnp.minimum(j, N_BRANCH_BLK - 1) * REORDER_COLS, 32), 0))],
        out_specs=pl.BlockSpec((None, D_MODEL, REORDER_COLS), lambda l, j: (l, 0, j)),
        out_shape=jax.ShapeDtypeStruct((depth, D_MODEL, N_SEG * SEG_W), BF16),
        compiler_params=_cparams(("arbitrary", "arbitrary")),
        name="reorder_w_in",
    )(w_in_t, w_in_t)


def _cast_kernel(w_ref, o_ref):
    o_ref[...] = w_ref[...].astype(BF16)


def _cast_experts(w):
    depth, e, r, c = w.shape
    spec = pl.BlockSpec((1, r, c), lambda i: (i, 0, 0))
    return pl.pallas_call(
        _cast_kernel,
        grid=(depth * e,),
        in_specs=[spec],
        out_specs=spec,
        out_shape=jax.ShapeDtypeStruct((depth * e, r, c), BF16),
        compiler_params=_cparams(("arbitrary",)),
        name="cast_experts",
    )(w.reshape(depth * e, r, c))


def _prep_layer(p):
    wift = p["wif_t"]
    wif_pad = jnp.zeros((D_MODEL, LANES), F32).at[:, :N_GATE_COLS].set(wift.T)
    rw = jnp.zeros((D_MODEL, LANES), F32).at[:, :N_GROUPS].set(p["router_g_w"]) \
        .at[:, N_GROUPS:N_GROUPS + N_EXPERTS].set(p["router_e_w"])
    rb = jnp.zeros((1, LANES), F32).at[0, :N_GROUPS].set(p["router_g_b"]) \
        .at[0, N_GROUPS:N_GROUPS + N_EXPERTS].set(p["router_e_b"])
    return dict(
        nw=p["norm1_w"].astype(F32).reshape(1, D_MODEL), wif=wif_pad, wift=wift,
        qn=p["na_q_norm"].astype(F32).reshape(1, HEAD_DIM), kn=p["na_k_norm"].astype(F32).reshape(1, HEAD_DIM),
        wb=p["w_branch"].astype(BF16), bgb=p["branch_gate_b"].astype(F32), wout=p["w_out"].astype(BF16),
        n2w=p["norm2_w"].astype(F32).reshape(1, D_MODEL), rb=rb,
        rw_hi=rw.astype(BF16), rw_lo=(rw - rw.astype(BF16).astype(F32)).astype(BF16),
    )


def _layer(x2d, B, L, p, q, rope, w_main, experts, layer):
    proj, gates, gates_t = _inproj(x2d, L, q["nw"], w_main, layer, q["wif"], q["wift"], rope[0], rope[1],
                                   q["qn"], q["kn"])
    o_ret, o_na, o_ml = _mixers(proj, gates, gates_t, B, L, p)
    h, hn, route, counts = _merge(x2d, o_ret, o_na, o_ml, proj, q["wb"], q["bgb"], q["wout"], q["n2w"],
                                  q["rw_hi"], q["rw_lo"], q["rb"])
    return _moe(h, hn, route, counts, experts[0], experts[1], experts[2], layer)


_PARAM_NAMES = ("norm1_w", "w_in", "ret_decay", "ret_norm_w", "na_q_norm", "na_k_norm", "na_rpb", "ml_conv",
                "ml_igate_b", "ml_fgate_b", "ml_norm_w", "w_branch", "branch_gate_b", "w_out", "norm2_w",
                "router_g_w", "router_g_b", "router_e_w", "router_e_b", "exp_w_gate", "exp_w_up", "exp_w_down")


def _run(x, layers, preps, w_main, experts):
    B, L, D = x.shape
    rope = _rope_tables(L)
    x2d = x.reshape(B * L, D)
    for layer, (p, q) in enumerate(zip(layers, preps)):
        x2d = _layer(x2d, B, L, p, q, rope, w_main, experts, layer)
    return x2d.reshape(B, L, D)


def kernel(x_prompt, x_sample, norm1_w, w_in, ret_decay, ret_norm_w, na_q_norm, na_k_norm, na_rpb, ml_conv,
           ml_igate_b, ml_fgate_b, ml_norm_w, w_branch, branch_gate_b, w_out, norm2_w, router_g_w, router_g_b,
           router_e_w, router_e_b, exp_w_gate, exp_w_up, exp_w_down):
    stacked = (norm1_w, w_in, ret_decay, ret_norm_w, na_q_norm, na_k_norm, na_rpb, ml_conv, ml_igate_b,
               ml_fgate_b, ml_norm_w, w_branch, branch_gate_b, w_out, norm2_w, router_g_w, router_g_b,
               router_e_w, router_e_b, exp_w_gate, exp_w_up, exp_w_down)
    depth = w_in.shape[0]
    w_in_t = jnp.swapaxes(w_in, 1, 2)
    wif_t = w_in_t[:, N_MIX_SEG * SEG_W:N_MIX_SEG * SEG_W + N_GATE_COLS, :]
    layers = [dict(zip(_PARAM_NAMES, (None if a is w_in else a[l] for a in stacked)), wif_t=wif_t[l])
              for l in range(depth)]
    preps = [_prep_layer(p) for p in layers]
    experts = tuple(_cast_experts(w) for w in (exp_w_gate, exp_w_up, exp_w_down))
    w_main = _reorder_w_in(w_in_t)
    return (_run(x_prompt, layers, preps, w_main, experts), _run(x_sample, layers, preps, w_main, experts))
```

```python
import functools

import numpy as np
import jax
import jax.numpy as jnp
from jax import lax
from jax.experimental import pallas as pl
from jax.experimental.pallas import tpu as pltpu

D_MODEL = 2048
HEAD_DIM = 128
MIX_W = D_MODEL // 2
N_HEADS = MIX_W // HEAD_DIM
N_BRANCH = 3
CHUNK = 128
ROPE_BASE = 10000.0
GRID_W = 64
WIN_ROWS = 8
WIN_COLS = 16
ML_CONV_W = 3
N_GROUPS = 4
EXPERTS_PER_GROUP = 8
N_EXPERTS = N_GROUPS * EXPERTS_PER_GROUP
TOP_K_INNER = 2
D_EXPERT = D_MODEL // 2
EPS = 1e-6
NEG = -1e30

F32 = jnp.float32
BF16 = jnp.bfloat16

LANES = 128
SEG_W = MIX_W
SEG_GA, SEG_GB, SEG_GC = 0, 2, 4
SEG_RQ, SEG_RK, SEG_RV, SEG_RG = 6, 7, 8, 9
SEG_NQ, SEG_NK, SEG_NV = 10, 11, 12
SEG_MQ, SEG_MK, SEG_MV, SEG_MO = 13, 14, 15, 16
N_SEG = 17
N_GATE_COLS = 4 * N_HEADS

NA_QROWS = 2
NA_KBLKS = 5
NA_KW = NA_KBLKS * CHUNK
MOE_ROWS = 512
VMEM_LIMIT = 56 * 1024 * 1024


def _cparams(sem, **kw):
    return pltpu.CompilerParams(dimension_semantics=sem, vmem_limit_bytes=VMEM_LIMIT, **kw)


def _sigmoid(z):
    return 0.5 * jnp.tanh(0.5 * z) + 0.5


def _hs(h):
    return slice(h * HEAD_DIM, (h + 1) * HEAD_DIM)


def _heads(head, defer):
    if defer:
        return head
    for h in range(N_HEADS):
        head(h)


def _dot(a, b):
    return jnp.dot(a, b, preferred_element_type=F32)


def _dot_nt(a, b):
    return lax.dot_general(a, b, (((1,), (1,)), ((), ())), preferred_element_type=F32)


def _dot_tn(a, b):
    return lax.dot_general(a, b, (((0,), (0,)), ((), ())), preferred_element_type=F32)


def _inproj_kernel(x_ref, nw_ref, w_ref, wif_ref, wift_ref, cos_ref, sin_ref, qn_ref, kn_ref,
                   proj_ref, g_ref, gt_ref, xn_ref):
    j = pl.program_id(1)

    @pl.when(j == 0)
    def _():
        x = x_ref[...]
        y = x * lax.rsqrt(jnp.mean(x * x, -1, keepdims=True) + EPS) * nw_ref[...]
        xn = y.astype(BF16)
        xn_ref[...] = xn
        g_ref[...] = _dot(xn, wif_ref[...].astype(BF16))
        gt_ref[...] = _dot_nt(wift_ref[...].astype(BF16), xn)

    acc = _dot(xn_ref[...], w_ref[...])

    is_rope = (j == SEG_RQ) | (j == SEG_RK)
    is_norm = (j == SEG_NQ) | (j == SEG_NK)

    @pl.when(is_rope)
    def _():
        scale = jnp.where(j == SEG_RK, HEAD_DIM ** -0.5, 1.0).astype(F32)
        cos = cos_ref[...]
        sin = sin_ref[...]
        for h in range(N_HEADS):
            xh = acc[:, _hs(h)]
            r = xh * cos + pltpu.roll(xh, HEAD_DIM // 2, 1) * sin
            proj_ref[:, _hs(h)] = (r * scale).astype(BF16)

    @pl.when(is_norm)
    def _():
        w = jnp.where(j == SEG_NQ, qn_ref[...] * (HEAD_DIM ** -0.5), kn_ref[...])
        for h in range(N_HEADS):
            xh = acc[:, _hs(h)]
            y = xh * lax.rsqrt(jnp.mean(xh * xh, -1, keepdims=True) + EPS) * w
            proj_ref[:, _hs(h)] = y.astype(BF16)

    @pl.when(jnp.logical_not(is_rope | is_norm))
    def _():
        proj_ref[...] = acc.astype(BF16)


def _inproj(x2d, L, nw, w_main, layer, wif, wift, cos_t, sin_t, qn, kn):
    T = x2d.shape[0]
    tm = min(1024, L)
    nlt = L // tm
    return pl.pallas_call(
        _inproj_kernel,
        grid=(T // tm, N_SEG),
        in_specs=[
            pl.BlockSpec((tm, D_MODEL), lambda i, j: (i, 0)),
            pl.BlockSpec((1, D_MODEL), lambda i, j: (0, 0)),
            pl.BlockSpec((None, D_MODEL, SEG_W), lambda i, j: (layer, 0, j)),
            pl.BlockSpec((D_MODEL, LANES), lambda i, j: (0, 0)),
            pl.BlockSpec((N_GATE_COLS, D_MODEL), lambda i, j: (0, 0)),
            pl.BlockSpec((tm, HEAD_DIM), lambda i, j: (i % nlt, 0)),
            pl.BlockSpec((tm, HEAD_DIM), lambda i, j: (i % nlt, 0)),
            pl.BlockSpec((1, HEAD_DIM), lambda i, j: (0, 0)),
            pl.BlockSpec((1, HEAD_DIM), lambda i, j: (0, 0)),
        ],
        out_specs=[
            pl.BlockSpec((tm, SEG_W), lambda i, j: (i, j)),
            pl.BlockSpec((tm, LANES), lambda i, j: (i, 0)),
            pl.BlockSpec((N_GATE_COLS, tm), lambda i, j: (0, i)),
        ],
        out_shape=[
            jax.ShapeDtypeStruct((T, N_SEG * SEG_W), BF16),
            jax.ShapeDtypeStruct((T, LANES), F32),
            jax.ShapeDtypeStruct((N_GATE_COLS, T), F32),
        ],
        scratch_shapes=[pltpu.VMEM((tm, D_MODEL), BF16)],
        compiler_params=_cparams(("arbitrary", "arbitrary")),
        name="inproj",
    )(x2d, nw, w_main, wif, wift, cos_t, sin_t, qn, kn)


def _ret_state_kernel(k_ref, v_ref, kb_ref, cdb_ref, sb_ref, s_scr):
    c = pl.program_id(1)

    @pl.when(c == 0)
    def _():
        s_scr[...] = jnp.zeros_like(s_scr)

    sb_ref[0] = s_scr[...].astype(BF16)
    kk = (k_ref[...].astype(F32) * kb_ref[...]).astype(BF16)
    v = v_ref[...]
    for h in range(N_HEADS):
        kv = _dot_tn(kk[:, _hs(h)], v[:, _hs(h)])
        s_scr[h] = s_scr[h] * cdb_ref[h:h + 1, :] + kv


def _ret_out_kernel(q_ref, k_ref, v_ref, g_ref, sb_ref, dmat_ref, qf_ref, qb_ref, kf_ref, cdf_ref, nw_ref,
                    o_ref, s_scr, defer=False):
    c = pl.program_id(1)

    @pl.when(c == 0)
    def _():
        s_scr[...] = jnp.zeros_like(s_scr)

    q = q_ref[...]
    k = k_ref[...]
    v = v_ref[...]
    qf32 = q.astype(F32)
    q_fwd = (qf32 * qf_ref[...]).astype(BF16)
    q_bwd = (qf32 * qb_ref[...]).astype(BF16)
    k_end = (k.astype(F32) * kf_ref[...]).astype(BF16)
    g = g_ref[...].astype(F32)

    def head(h):
        hs = _hs(h)
        s = _dot_nt(q[:, hs], k[:, hs]) * dmat_ref[h]
        o = _dot(s.astype(BF16), v[:, hs])
        o = o + _dot(q_fwd[:, hs], s_scr[h].astype(BF16))
        o = o + _dot(q_bwd[:, hs], sb_ref[0, h])
        s_scr[h] = s_scr[h] * cdf_ref[h:h + 1, :] + _dot_tn(k_end[:, hs], v[:, hs])
        y = o * lax.rsqrt(jnp.mean(o * o, -1, keepdims=True) + EPS) * nw_ref[h:h + 1, :]
        gh = g[:, hs]
        o_ref[:, hs] = (y * (gh * _sigmoid(gh))).astype(BF16)

    return _heads(head, defer)


def _retention(proj, B, L, ret_decay, ret_norm_w):
    n = L // CHUNK
    T = B * L
    lg = jax.nn.log_sigmoid(ret_decay.astype(F32))
    idx = jnp.arange(CHUNK, dtype=F32)
    diff = idx[:, None] - idx[None, :]
    dmat = jnp.where(diff >= 0, jnp.exp(jnp.maximum(diff, 0.0) * lg[0][:, None, None]),
                     jnp.exp(jnp.maximum(-diff, 0.0) * lg[1][:, None, None]))

    def lane_tab(e):
        return jnp.repeat(jnp.exp(e).T, HEAD_DIM, axis=1)

    qf_tab = lane_tab((idx + 1.0)[None, :] * lg[0][:, None])
    qb_tab = lane_tab((CHUNK - idx)[None, :] * lg[1][:, None])
    kf_tab = lane_tab((CHUNK - 1.0 - idx)[None, :] * lg[0][:, None])
    kb_tab = lane_tab(idx[None, :] * lg[1][:, None])
    cdf = jnp.broadcast_to(jnp.exp(CHUNK * lg[0])[:, None], (N_HEADS, HEAD_DIM))
    cdb = jnp.broadcast_to(jnp.exp(CHUNK * lg[1])[:, None], (N_HEADS, HEAD_DIM))

    def seg(s, rev=False):
        if rev:
            return pl.BlockSpec((CHUNK, SEG_W), lambda b, c: (b * n + n - 1 - c, s))
        return pl.BlockSpec((CHUNK, SEG_W), lambda b, c: (b * n + c, s))

    full = lambda shape: pl.BlockSpec(shape, lambda b, c: (0,) * len(shape))
    st_shape = (1, N_HEADS, HEAD_DIM, HEAD_DIM)

    state = dict(
        kernel=_ret_state_kernel,
        in_specs=[seg(SEG_RK, True), seg(SEG_RV, True), full((CHUNK, MIX_W)), full((N_HEADS, HEAD_DIM))],
        args=[proj, proj, kb_tab, cdb],
        out_specs=[pl.BlockSpec(st_shape, lambda b, c: (b * n + n - 1 - c, 0, 0, 0))],
        out_shape=[jax.ShapeDtypeStruct((B * n, N_HEADS, HEAD_DIM, HEAD_DIM), BF16)],
        scratch=[pltpu.VMEM((N_HEADS, HEAD_DIM, HEAD_DIM), F32)])

    def out(sb):
        return dict(
            kernel=_ret_out_kernel, per_head=True,
            in_specs=[seg(SEG_RQ), seg(SEG_RK), seg(SEG_RV), seg(SEG_RG),
                      pl.BlockSpec(st_shape, lambda b, c: (b * n + c, 0, 0, 0)),
                      full((N_HEADS, CHUNK, CHUNK)), full((CHUNK, MIX_W)), full((CHUNK, MIX_W)),
                      full((CHUNK, MIX_W)), full((N_HEADS, HEAD_DIM)), full((N_HEADS, HEAD_DIM))],
            args=[proj, proj, proj, proj, sb, dmat, qf_tab, qb_tab, kf_tab, cdf, ret_norm_w.astype(F32)],
            out_specs=[pl.BlockSpec((CHUNK, MIX_W), lambda b, c: (b * n + c, 0))],
            out_shape=[jax.ShapeDtypeStruct((T, MIX_W), BF16)],
            scratch=[pltpu.VMEM((N_HEADS, HEAD_DIM, HEAD_DIM), F32)])

    return state, out


def _na_index_tables(nqb):
    rows = NA_QROWS * nqb
    qi = np.arange(CHUNK)
    ki = np.arange(NA_KW)

    def one(qb):
        kb = NA_QROWS * int(np.clip(qb - 2, 0, nqb - NA_KBLKS))
        r = (NA_QROWS * qb + qi // GRID_W)[:, None]
        qc = (qi % GRID_W)[:, None]
        kr = (kb + ki // GRID_W)[None, :]
        kc = (ki % GRID_W)[None, :]
        rs = np.clip(r - WIN_ROWS // 2, 0, rows - WIN_ROWS)
        cs = np.clip(qc - WIN_COLS // 2, 0, GRID_W - WIN_COLS)
        valid = (kr >= rs) & (kr < rs + WIN_ROWS) & (kc >= cs) & (kc < cs + WIN_COLS)
        dr = np.clip(kr - r + WIN_ROWS - 1, 0, 2 * WIN_ROWS - 2)
        dc = np.clip(kc - qc + WIN_COLS - 1, 0, 2 * WIN_COLS - 2)
        return dr + 0 * dc, dc + 0 * dr, valid

    reps = [0, 1, 2, nqb - 2, nqb - 1]
    tabs = [one(qb) for qb in reps]
    for qb in range(2, nqb - 2):
        t = one(qb)
        assert all(np.array_equal(a, b) for a, b in zip(t, tabs[2]))
    return tuple(np.stack([t[i] for t in tabs]) for i in range(3))


def _na_kernel(q_ref, k0, k1, k2, k3, k4, v0, v1, v2, v3, v4, bias_ref, o_ref, defer=False):
    k_refs = (k0, k1, k2, k3, k4)
    v_refs = (v0, v1, v2, v3, v4)

    def head(h):
        hs = _hs(h)
        q = q_ref[:, hs]
        k_all = jnp.concatenate([r[:, hs] for r in k_refs], 0)
        v_all = jnp.concatenate([r[:, hs] for r in v_refs], 0)
        s = _dot_nt(q, k_all) + bias_ref[0, h]
        p = jnp.exp(s - jnp.max(s, -1, keepdims=True))
        o = _dot(p.astype(BF16), v_all)
        o_ref[:, hs] = (o / jnp.sum(p, -1, keepdims=True)).astype(BF16)

    return _heads(head, defer)


def _neighborhood(proj, B, L, rpb):
    nqb = L // CHUNK
    assert nqb >= NA_KBLKS and L % (GRID_W * NA_QROWS) == 0
    T = B * L
    dr, dc, valid = _na_index_tables(nqb)
    n_cls = dr.shape[0]
    kr_n = NA_KW // GRID_W
    dr6 = dr.reshape(n_cls, NA_QROWS, GRID_W, kr_n, GRID_W)
    dc6 = dc.reshape(n_cls, NA_QROWS, GRID_W, kr_n, GRID_W)
    dr_s = dr6[:, :, 0, :, 0]
    dc_s = dc6[0, 0, :, 0, :]
    assert np.array_equal(dr6, np.broadcast_to(dr_s[:, :, None, :, None], dr6.shape))
    assert np.array_equal(dc6, np.broadcast_to(dc_s[None, None, :, None, :], dc6.shape))
    sel_r = (dr_s[..., None] == np.arange(2 * WIN_ROWS - 1)).astype(np.float32)
    sel_c = (np.arange(2 * WIN_COLS - 1)[:, None, None] == dc_s[None]).astype(np.float32)
    hp = lax.Precision.HIGHEST
    t = jnp.einsum('cajr,hrd->cajhd', sel_r, rpb.astype(F32), precision=hp)
    bias = jnp.einsum('cajhd,dqk->chaqjk', t, sel_c, precision=hp).reshape(n_cls, N_HEADS, CHUNK, NA_KW)
    bias = jnp.where(jnp.asarray(valid)[:, None], bias, NEG)

    def kspec(s, j):
        return pl.BlockSpec((CHUNK, SEG_W),
                            lambda b, qb: (b * nqb + jnp.clip(qb - 2, 0, nqb - NA_KBLKS) + j, s))

    def cls(qb):
        return jnp.where(qb < 2, qb, jnp.where(qb >= nqb - 2, qb - (nqb - 5), 2))

    return dict(
        kernel=_na_kernel, per_head=True,
        in_specs=[pl.BlockSpec((CHUNK, SEG_W), lambda b, qb: (b * nqb + qb, SEG_NQ))]
        + [kspec(SEG_NK, j) for j in range(NA_KBLKS)]
        + [kspec(SEG_NV, j) for j in range(NA_KBLKS)]
        + [pl.BlockSpec((1, N_HEADS, CHUNK, NA_KW), lambda b, qb: (cls(qb), 0, 0, 0))],
        args=[proj] * (1 + 2 * NA_KBLKS) + [bias],
        out_specs=[pl.BlockSpec((CHUNK, MIX_W), lambda b, qb: (b * nqb + qb, 0))],
        out_shape=[jax.ShapeDtypeStruct((T, MIX_W), BF16)],
        scratch=[])


HALO = 16


def _log_sigmoid(x):
    return jnp.minimum(x, 0.0) - jnp.log1p(jnp.exp(-jnp.abs(x)))


def _conv_silu(x_ref, prev_ref, next_ref, w_ref, col0, c, n):
    x = x_ref[...].astype(F32)
    row = lax.broadcasted_iota(jnp.int32, x.shape, 0)
    prev_row = prev_ref[HALO - 1:HALO, :].astype(F32) * (c > 0).astype(F32)
    next_row = next_ref[0:1, :].astype(F32) * (c < n - 1).astype(F32)
    x_prev = jnp.where(row == 0, prev_row, pltpu.roll(x, 1, 0))
    x_next = jnp.where(row == CHUNK - 1, next_row, pltpu.roll(x, CHUNK - 1, 0))
    cs = slice(col0, col0 + MIX_W)
    y = x_prev * w_ref[0:1, cs] + x * w_ref[1:2, cs] + x_next * w_ref[2:3, cs]
    return y * _sigmoid(y)


def _tri():
    r = lax.broadcasted_iota(jnp.int32, (CHUNK, CHUNK), 0)
    c = lax.broadcasted_iota(jnp.int32, (CHUNK, CHUNK), 1)
    return (c <= r).astype(F32), (c >= r).astype(F32)


def _hp_dot(a, b):
    return jnp.dot(a, b, preferred_element_type=F32, precision=lax.Precision.HIGHEST)


FWD_LANE = 2 * N_HEADS
BWD_LANE = 3 * N_HEADS


def _gate_dense(gc_ref, gr_ref, bc_ref, br_ref):
    low, up = _tri()
    g_col = gc_ref[...] + bc_ref[...]
    lf_col = _log_sigmoid(g_col)
    lf_row = _log_sigmoid(gr_ref[...] + br_ref[...])
    b_col = _hp_dot(low, lf_col)
    b_row = _hp_dot(lf_row, up)
    tot_row = b_col[CHUNK - 1:CHUNK, :]
    tot_col = b_row[:, CHUNK - 1:CHUNK]
    lane = lax.broadcasted_iota(jnp.int32, (CHUNK, LANES), 1)
    row = lax.broadcasted_iota(jnp.int32, (N_GATE_COLS, CHUNK), 0)
    bb_col = jnp.where(lane >= BWD_LANE, tot_row - b_col + lf_col, b_col)
    bb_row = jnp.where(row >= BWD_LANE, tot_col - b_row + lf_row, b_row)
    ck = pltpu.roll(g_col, FWD_LANE, 1) - bb_col
    a = tot_row + ck
    m_loc = jnp.max(a, 0, keepdims=True)
    ea = jnp.exp(a - m_loc)
    return bb_row, ck, tot_row, m_loc, ea


def _lane_bcast(x, l):
    return jnp.broadcast_to(x[:, l:l + 1], (x.shape[0], HEAD_DIM))


def _ml_state_update(kc_h, ct_loc_fn, ea, tot_row, m_loc, l, c_scr, n_scr, m_scr, h):
    m_p = m_scr[h:h + 1, 0:1]
    g_tot = tot_row[:, l:l + 1]
    m_l = m_loc[:, l:l + 1]
    kw = kc_h * _lane_bcast(ea, l)
    ct_loc = ct_loc_fn(kw.astype(BF16))
    n_loc = jnp.sum(kw, 0, keepdims=True)
    m_new = jnp.maximum(g_tot + m_p, m_l)
    sp = jnp.exp(g_tot + m_p - m_new)
    sl = jnp.exp(m_l - m_new)
    c_scr[h] = sp * c_scr[h] + sl * ct_loc
    n_scr[h:h + 1, :] = sp * n_scr[h:h + 1, :] + sl * n_loc
    m_scr[h:h + 1, :] = jnp.broadcast_to(m_new, (1, HEAD_DIM))


def _ml_init(c, c_scr, n_scr, m_scr):
    @pl.when(c == 0)
    def _():
        c_scr[...] = jnp.zeros_like(c_scr)
        n_scr[...] = jnp.zeros_like(n_scr)
        m_scr[...] = jnp.full_like(m_scr, NEG)


def _ml_state_kernel(k_ref, kp_ref, kn_ref, v_ref, gc_ref, gr_ref, cw_ref, bc_ref, br_ref,
                     cb_ref, nm_ref, c_scr, n_scr, m_scr, *, n):
    step = pl.program_id(1)
    _ml_init(step, c_scr, n_scr, m_scr)
    c = n - 1 - step
    cb_ref[0] = c_scr[...].astype(BF16)
    nm_ref[0, 0:N_HEADS, :] = n_scr[...]
    nm_ref[0, N_HEADS:2 * N_HEADS, :] = m_scr[...]

    kc = _conv_silu(k_ref, kp_ref, kn_ref, cw_ref, MIX_W, c, n) * (HEAD_DIM ** -0.5)
    v = v_ref[...]
    _, _, tot_row, m_loc, ea = _gate_dense(gc_ref, gr_ref, bc_ref, br_ref)
    for h in range(N_HEADS):
        v_h = v[:, _hs(h)]
        _ml_state_update(kc[:, _hs(h)], lambda kw: _dot_tn(v_h, kw), ea, tot_row, m_loc, BWD_LANE + h,
                         c_scr, n_scr, m_scr, h)


def _ml_direction(st, mask, b_q, ck_b, m_p, qn, vt_bf, ct_bf, q_bf):
    dmat = jnp.where(mask, b_q + ck_b, NEG)
    inter = b_q + m_p
    m_row = jnp.maximum(jnp.max(dmat, 0, keepdims=True), inter)
    s = st * jnp.exp(dmat - m_row)
    e_int = jnp.exp(inter - m_row)
    num = _dot(vt_bf, s.astype(BF16)) + e_int * _dot_nt(ct_bf, q_bf)
    den = jnp.sum(s, 0, keepdims=True) + e_int * qn
    return num / jnp.maximum(jnp.abs(den), jnp.exp(-m_row))


def _ml_out_kernel(q_ref, qp_ref, qn_ref, k_ref, kp_ref, kn_ref, v_ref, o_ref_in, gc_ref, gr_ref,
                   cb_ref, nm_ref, cw_ref, bc_ref, br_ref, nwt_ref,
                   out_ref, c_scr, n_scr, m_scr, *, n, defer=False):
    c = pl.program_id(1)
    _ml_init(c, c_scr, n_scr, m_scr)
    qc = _conv_silu(q_ref, qp_ref, qn_ref, cw_ref, 0, c, n)
    kc = _conv_silu(k_ref, kp_ref, kn_ref, cw_ref, MIX_W, c, n) * (HEAD_DIM ** -0.5)
    q_bf = qc.astype(BF16)
    k_bf = kc.astype(BF16)
    v = v_ref[...]
    og = o_ref_in[...].astype(F32)
    bb_row, ck, tot_row, m_loc, ea = _gate_dense(gc_ref, gr_ref, bc_ref, br_ref)
    key = lax.broadcasted_iota(jnp.int32, (CHUNK, CHUNK), 0)
    qry = lax.broadcasted_iota(jnp.int32, (CHUNK, CHUNK), 1)
    causal = key <= qry
    anti = key >= qry
    pad = jnp.zeros((8 - 2, HEAD_DIM), F32)

    def head(h):
        hs = _hs(h)
        lf, lb = FWD_LANE + h, BWD_LANE + h
        qh = q_bf[:, hs]
        vt_bf = v[:, hs].astype(F32).T.astype(BF16)
        st = _dot_nt(k_bf[:, hs], qh)
        n_prev = jnp.concatenate([n_scr[h:h + 1, :], nm_ref[0, h:h + 1, :], pad], 0).astype(BF16)
        qn = _dot_nt(n_prev, qh)
        h_f = _ml_direction(st, causal, bb_row[lf:lf + 1, :], _lane_bcast(ck, lf), m_scr[h:h + 1, 0:1],
                            qn[0:1, :], vt_bf, c_scr[h].astype(BF16), qh)
        h_b = _ml_direction(st, anti, bb_row[lb:lb + 1, :], _lane_bcast(ck, lb),
                            nm_ref[0, N_HEADS + h:N_HEADS + h + 1, 0:1], qn[1:2, :], vt_bf, cb_ref[0, h], qh)
        _ml_state_update(kc[:, hs], lambda kw: _dot(vt_bf, kw), ea, tot_row, m_loc, lf, c_scr, n_scr, m_scr, h)
        ot = h_f + h_b
        yt = ot * lax.rsqrt(jnp.mean(ot * ot, 0, keepdims=True) + EPS) * nwt_ref[h]
        out_ref[:, hs] = (yt.T * _sigmoid(og[:, hs])).astype(BF16)

    return _heads(head, defer)


def _mlstm(proj, gates, gates_t, B, L, conv_w, ig_b, fg_b, norm_w):
    n = L // CHUNK
    T = B * L
    hb = CHUNK // HALO
    bias = jnp.concatenate([ig_b.astype(F32).reshape(-1), fg_b.astype(F32).reshape(-1)])
    bias_col = jnp.zeros((1, LANES), F32).at[0, :N_GATE_COLS].set(bias)
    bias_row = jnp.broadcast_to(bias[:, None], (N_GATE_COLS, CHUNK))
    cw = conv_w.astype(F32)
    norm_wt = jnp.broadcast_to(norm_w.astype(F32)[:, :, None], (N_HEADS, HEAD_DIM, CHUNK))

    def seg(s, rev):
        if rev:
            return pl.BlockSpec((CHUNK, SEG_W), lambda b, c: (b * n + n - 1 - c, s))
        return pl.BlockSpec((CHUNK, SEG_W), lambda b, c: (b * n + c, s))

    def chunk_of(c, rev):
        return n - 1 - c if rev else c

    def prev_spec(s, rev):
        return pl.BlockSpec((HALO, SEG_W),
                            lambda b, c: (b * n * hb + jnp.maximum(chunk_of(c, rev) * hb - 1, 0), s))

    def next_spec(s, rev):
        return pl.BlockSpec((HALO, SEG_W),
                            lambda b, c: (b * n * hb + jnp.minimum((chunk_of(c, rev) + 1) * hb, n * hb - 1), s))

    def gcol_spec(rev):
        return pl.BlockSpec((CHUNK, LANES), lambda b, c: (b * n + chunk_of(c, rev), 0))

    def grow_spec(rev):
        return pl.BlockSpec((N_GATE_COLS, CHUNK), lambda b, c: (0, b * n + chunk_of(c, rev)))

    full = lambda shape: pl.BlockSpec(shape, lambda b, c: (0,) * len(shape))
    st_shape = (1, N_HEADS, HEAD_DIM, HEAD_DIM)
    nm_shape = (1, 2 * N_HEADS, HEAD_DIM)
    scratch = [pltpu.VMEM((N_HEADS, HEAD_DIM, HEAD_DIM), F32), pltpu.VMEM((N_HEADS, HEAD_DIM), F32),
               pltpu.VMEM((N_HEADS, HEAD_DIM), F32)]
    consts = [full((ML_CONV_W, 2 * MIX_W)), full((1, LANES)), full((N_GATE_COLS, CHUNK))]

    state = dict(
        kernel=functools.partial(_ml_state_kernel, n=n),
        in_specs=[seg(SEG_MK, True), prev_spec(SEG_MK, True), next_spec(SEG_MK, True), seg(SEG_MV, True),
                  gcol_spec(True), grow_spec(True)] + consts,
        args=[proj, proj, proj, proj, gates, gates_t, cw, bias_col, bias_row],
        out_specs=[pl.BlockSpec(st_shape, lambda b, c: (b * n + n - 1 - c, 0, 0, 0)),
                   pl.BlockSpec(nm_shape, lambda b, c: (b * n + n - 1 - c, 0, 0))],
        out_shape=[jax.ShapeDtypeStruct((B * n, N_HEADS, HEAD_DIM, HEAD_DIM), BF16),
                   jax.ShapeDtypeStruct((B * n, 2 * N_HEADS, HEAD_DIM), F32)],
        scratch=scratch)

    def out(cb, nm):
        return dict(
            kernel=functools.partial(_ml_out_kernel, n=n), per_head=True,
            in_specs=[seg(SEG_MQ, False), prev_spec(SEG_MQ, False), next_spec(SEG_MQ, False),
                      seg(SEG_MK, False), prev_spec(SEG_MK, False), next_spec(SEG_MK, False),
                      seg(SEG_MV, False), seg(SEG_MO, False), gcol_spec(False), grow_spec(False),
                      pl.BlockSpec(st_shape, lambda b, c: (b * n + c, 0, 0, 0)),
                      pl.BlockSpec(nm_shape, lambda b, c: (b * n + c, 0, 0))]
            + consts + [full((N_HEADS, HEAD_DIM, CHUNK))],
            args=[proj, proj, proj, proj, proj, proj, proj, proj, gates, gates_t, cb, nm,
                  cw, bias_col, bias_row, norm_wt],
            out_specs=[pl.BlockSpec((CHUNK, MIX_W), lambda b, c: (b * n + c, 0))],
            out_shape=[jax.ShapeDtypeStruct((T, MIX_W), BF16)],
            scratch=scratch)

    return state, out


def _fused_call(parts, grid, name):
    n_in = [len(p["in_specs"]) for p in parts]
    n_out = [len(p["out_specs"]) for p in parts]
    n_scr = [len(p["scratch"]) for p in parts]

    def body(*refs):
        ins, outs, scr = refs[:sum(n_in)], refs[sum(n_in):sum(n_in) + sum(n_out)], refs[sum(n_in) + sum(n_out):]
        i = o = k = 0
        heads = []
        for p, a, b, c in zip(parts, n_in, n_out, n_scr):
            args = (*ins[i:i + a], *outs[o:o + b], *scr[k:k + c])
            if p.get("per_head"):
                heads.append(p["kernel"](*args, defer=True))
            else:
                p["kernel"](*args)
            i, o, k = i + a, o + b, k + c
        for h in range(N_HEADS):
            for head in heads:
                head(h)

    return pl.pallas_call(
        body,
        grid=grid,
        in_specs=[s for p in parts for s in p["in_specs"]],
        out_specs=[s for p in parts for s in p["out_specs"]],
        out_shape=[s for p in parts for s in p["out_shape"]],
        scratch_shapes=[s for p in parts for s in p["scratch"]],
        compiler_params=_cparams(("arbitrary", "arbitrary")),
        name=name,
    )(*[a for p in parts for a in p["args"]])


def _mixers(proj, gates, gates_t, B, L, p):
    grid = (B, L // CHUNK)
    ret_state, ret_out = _retention(proj, B, L, p["ret_decay"], p["ret_norm_w"])
    ml_state, ml_out = _mlstm(proj, gates, gates_t, B, L, p["ml_conv"], p["ml_igate_b"], p["ml_fgate_b"],
                              p["ml_norm_w"])
    sb, cb, nm = _fused_call([ret_state, ml_state], grid, "mixer_states")
    na = _neighborhood(proj, B, L, p["na_rpb"])
    return _fused_call([ret_out(sb), na, ml_out(cb, nm)], grid, "mixer_outputs")


def _branch_kernel(oret_ref, ona_ref, oml_ref, ga_ref, gb_ref, gc_ref, wb_ref, bgb_ref, m_ref):
    o_refs = (oret_ref, ona_ref, oml_ref)
    g_refs = (ga_ref, gb_ref, gc_ref)
    for n in range(D_MODEL // SEG_W):
        cs = slice(n * SEG_W, (n + 1) * SEG_W)
        acc = None
        for i in range(N_BRANCH):
            z = g_refs[i][:, cs].astype(F32) + bgb_ref[i:i + 1, cs]
            term = _sigmoid(z) * _dot(o_refs[i][...], wb_ref[i, :, cs])
            acc = term if acc is None else acc + term
        m_ref[:, cs] = acc.astype(BF16)


def _outproj_router_kernel(x_ref, m_ref, wout_ref, n2w_ref, rwhl_ref, rb_ref,
                           h_ref, hn_ref, route_ref, cnt_ref, cnt_scr):
    h = x_ref[...] + _dot(m_ref[...], wout_ref[...])
    h_ref[...] = h
    hn = h * lax.rsqrt(jnp.mean(h * h, -1, keepdims=True) + EPS) * n2w_ref[...]
    hn_ref[...] = hn

    hn_hi = hn.astype(BF16)
    hn_lo = (hn - hn_hi.astype(F32)).astype(BF16)
    hi = _dot(hn_hi, rwhl_ref[...])
    logits = (hi[:, :LANES] + _dot(hn_lo, rwhl_ref[:, :LANES]) + hi[:, LANES:]) + rb_ref[...]
    lane_i = lax.broadcasted_iota(jnp.int32, logits.shape, 1)
    lane = lane_i.astype(F32)
    lane_grp = jnp.right_shift(lane_i - N_GROUPS, 3).astype(F32)
    gmask = lane_i < N_GROUPS
    lg = jnp.where(gmask, logits, NEG)
    mg = jnp.max(lg, -1, keepdims=True)
    p_top = 1.0 / jnp.sum(jnp.where(gmask, jnp.exp(lg - mg), 0.0), -1, keepdims=True)
    grp = jnp.min(jnp.where(lg == mg, lane, float(LANES)), -1, keepdims=True)
    emask = (lane_i >= N_GROUPS) & (lane_i < N_GROUPS + N_EXPERTS) & (lane_grp == grp)
    le = jnp.where(emask, logits, NEG)
    m1 = jnp.max(le, -1, keepdims=True)
    i1 = jnp.min(jnp.where(le == m1, lane, float(LANES)), -1, keepdims=True)
    le2 = jnp.where(lane == i1, NEG, le)
    m2 = jnp.max(le2, -1, keepdims=True)
    i2 = jnp.min(jnp.where(le2 == m2, lane, float(LANES)), -1, keepdims=True)
    e2 = jnp.exp(m2 - m1)
    gate1 = p_top / (1.0 + e2)
    gate2 = p_top * e2 / (1.0 + e2)

    @pl.when(pl.program_id(0) == 0)
    def _():
        cnt_scr[...] = jnp.zeros_like(cnt_scr)

    tm = logits.shape[0]
    hit1 = lane == i1
    hit2 = lane == i2
    onehot = jnp.where(hit1 | hit2, 1.0, 0.0)
    r_i = lax.broadcasted_iota(jnp.int32, (tm, tm), 0)
    c_i = lax.broadcasted_iota(jnp.int32, (tm, tm), 1)
    before = jnp.where(c_i < r_i, 1.0, 0.0).astype(BF16)
    prior = _dot(before, onehot.astype(BF16)) + cnt_scr[...]
    rank1 = jnp.sum(jnp.where(hit1, prior, 0.0), -1, keepdims=True)
    rank2 = jnp.sum(jnp.where(hit2, prior, 0.0), -1, keepdims=True)
    cnt = cnt_scr[...] + jnp.sum(onehot, 0, keepdims=True)
    cnt_scr[...] = cnt
    cnt_ref[...] = jnp.broadcast_to(cnt, cnt_ref.shape)

    cols = (i1 - N_GROUPS, i2 - N_GROUPS, gate1, gate2, rank1, rank2)
    route = jnp.zeros_like(logits)
    for ci, val in enumerate(cols):
        route = jnp.where(lane_i == ci, val, route)
    route_ref[...] = route


MERGE_ROWS = 512


def _merge(x2d, o_ret, o_na, o_ml, proj, wb, bgb, wout, n2w, rw_hi, rw_lo, rb):
    T = x2d.shape[0]
    tm = MERGE_ROWS
    full = lambda shape: pl.BlockSpec(shape, lambda i: (0,) * len(shape), pipeline_mode=pl.Buffered(1))
    row = lambda w: pl.BlockSpec((tm, w), lambda i: (i, 0))
    gate = lambda s: pl.BlockSpec((tm, D_MODEL), lambda i: (i, s // 2))
    merged = pl.pallas_call(
        _branch_kernel,
        grid=(T // tm,),
        in_specs=[row(MIX_W), row(MIX_W), row(MIX_W), gate(SEG_GA), gate(SEG_GB), gate(SEG_GC),
                  full((N_BRANCH, MIX_W, D_MODEL)), full((N_BRANCH, D_MODEL))],
        out_specs=row(D_MODEL),
        out_shape=jax.ShapeDtypeStruct((T, D_MODEL), BF16),
        compiler_params=_cparams(("arbitrary",)),
        name="branch_merge",
    )(o_ret, o_na, o_ml, proj, proj, proj, wb, bgb)
    return pl.pallas_call(
        _outproj_router_kernel,
        grid=(T // tm,),
        in_specs=[row(D_MODEL), row(D_MODEL), full((D_MODEL, D_MODEL)), full((1, D_MODEL)),
                  full((D_MODEL, 2 * LANES)), full((1, LANES))],
        out_specs=[row(D_MODEL), row(D_MODEL), row(LANES), pl.BlockSpec((8, LANES), lambda i: (0, 0))],
        out_shape=[jax.ShapeDtypeStruct((T, D_MODEL), F32), jax.ShapeDtypeStruct((T, D_MODEL), F32),
                   jax.ShapeDtypeStruct((T, LANES), F32), jax.ShapeDtypeStruct((8, LANES), F32)],
        scratch_shapes=[pltpu.VMEM((1, LANES), F32)],
        compiler_params=_cparams(("arbitrary",)),
        name="outproj_router",
    )(x2d, merged, wout, n2w, jnp.concatenate([rw_hi, rw_lo], 1), rb)


def _row_copy(src_hbm, row, dst, r, sem):
    return pltpu.make_async_copy(src_hbm.at[pl.ds(row, 1), :], dst.at[pl.ds(r, 1), :], sem)


def _ffn_kernel(blk_e_ref, n_used_ref, src_ref, hn_hbm, wg_ref, wu_ref, wd_ref, y_ref, xbuf, sem):
    i = pl.program_id(0)
    n_used = n_used_ref[0]
    slot = i % 2

    def wait_block(s):
        pltpu.make_async_copy(hn_hbm.at[pl.ds(0, MOE_ROWS), :], xbuf.at[s], sem.at[s]).wait()

    @pl.when(i == 0)
    def _():
        def body(r, carry):
            _row_copy(hn_hbm, src_ref[r], xbuf.at[0], r, sem.at[0]).start()
            return carry
        lax.fori_loop(0, MOE_ROWS, body, 0, unroll=8)

    @pl.when(i < n_used)
    def _():
        for r in range(MOE_ROWS):
            _row_copy(hn_hbm, src_ref[(i + 1) * MOE_ROWS + r], xbuf.at[1 - slot], r, sem.at[1 - slot]).start()
        wait_block(slot)
        x = xbuf[slot].astype(BF16)
        a = _dot(x, wg_ref[0])
        hid = (a * _sigmoid(a)) * _dot(x, wu_ref[0])
        y_ref[...] = _dot(hid.astype(BF16), wd_ref[0])

    @pl.when(i >= n_used)
    def _():
        y_ref[...] = jnp.zeros_like(y_ref)

    @pl.when(i == n_used)
    def _():
        wait_block(slot)


def _expert_ffn(hn, src, blk_e, n_used, wg, wu, wd):
    n_blocks = src.shape[0] // MOE_ROWS
    grid_spec = pltpu.PrefetchScalarGridSpec(
        num_scalar_prefetch=3,
        grid=(n_blocks,),
        in_specs=[
            pl.BlockSpec(memory_space=pl.ANY),
            pl.BlockSpec((1, D_MODEL, D_EXPERT), lambda i, be, nu, sr: (be[i], 0, 0)),
            pl.BlockSpec((1, D_MODEL, D_EXPERT), lambda i, be, nu, sr: (be[i], 0, 0)),
            pl.BlockSpec((1, D_EXPERT, D_MODEL), lambda i, be, nu, sr: (be[i], 0, 0)),
        ],
        out_specs=pl.BlockSpec((MOE_ROWS, D_MODEL), lambda i, be, nu, sr: (i, 0)),
        scratch_shapes=[pltpu.VMEM((2, MOE_ROWS, D_MODEL), F32), pltpu.SemaphoreType.DMA((2,))],
    )
    return pl.pallas_call(
        _ffn_kernel,
        grid_spec=grid_spec,
        out_shape=jax.ShapeDtypeStruct((n_blocks * MOE_ROWS, D_MODEL), F32),
        compiler_params=_cparams(("arbitrary",), disable_bounds_checks=True),
        name="expert_ffn",
    )(blk_e, n_used, src, hn, wg, wu, wd)


COMBINE_ROWS = 256


def _combine_kernel(dest_ref, h_ref, route_ref, yb_hbm, out_ref, ybuf, sem):
    i = pl.program_id(0)
    slot = i % 2

    def start_row(tile, r, to_slot):
        a = (tile * COMBINE_ROWS + r) * TOP_K_INNER
        for k in range(TOP_K_INNER):
            _row_copy(yb_hbm, dest_ref[a + k], ybuf.at[to_slot, k], r, sem.at[to_slot]).start()

    @pl.when(i == 0)
    def _():
        def body(r, carry):
            start_row(0, r, 0)
            return carry
        lax.fori_loop(0, COMBINE_ROWS, body, 0, unroll=8)

    @pl.when(i + 1 < pl.num_programs(0))
    def _():
        for r in range(COMBINE_ROWS):
            start_row(i + 1, r, 1 - slot)

    for k in range(TOP_K_INNER):
        pltpu.make_async_copy(yb_hbm.at[pl.ds(0, COMBINE_ROWS), :], ybuf.at[slot, k], sem.at[slot]).wait()

    route = route_ref[...]
    out_ref[...] = h_ref[...] + route[:, 2:3] * ybuf[slot, 0] + route[:, 3:4] * ybuf[slot, 1]


def _combine(h, route, yb, dest):
    T = h.shape[0]
    grid_spec = pltpu.PrefetchScalarGridSpec(
        num_scalar_prefetch=1,
        grid=(T // COMBINE_ROWS,),
        in_specs=[
            pl.BlockSpec((COMBINE_ROWS, D_MODEL), lambda i, d: (i, 0)),
            pl.BlockSpec((COMBINE_ROWS, LANES), lambda i, d: (i, 0)),
            pl.BlockSpec(memory_space=pl.ANY),
        ],
        out_specs=pl.BlockSpec((COMBINE_ROWS, D_MODEL), lambda i, d: (i, 0)),
        scratch_shapes=[pltpu.VMEM((2, TOP_K_INNER, COMBINE_ROWS, D_MODEL), F32), pltpu.SemaphoreType.DMA((2,))],
    )
    return pl.pallas_call(
        _combine_kernel,
        grid_spec=grid_spec,
        out_shape=jax.ShapeDtypeStruct((T, D_MODEL), F32),
        compiler_params=_cparams(("arbitrary",), disable_bounds_checks=True),
        name="moe_combine",
    )(dest, h, route, yb)


def _moe(h, hn, route, counts, wg, wu, wd, layer):
    T = h.shape[0]
    n_assign = T * TOP_K_INNER
    expert = route[:, 0:2].astype(jnp.int32)
    rank = route[:, 4:6].astype(jnp.int32)
    cnt = counts[0, N_GROUPS:N_GROUPS + N_EXPERTS].astype(jnp.int32)
    padded = (cnt + MOE_ROWS - 1) // MOE_ROWS * MOE_ROWS
    pend = jnp.cumsum(padded)
    pstart = pend - padded
    dest = (pstart[expert] + rank).reshape(-1)
    n_blocks = (n_assign + N_EXPERTS * (MOE_ROWS - 1) + MOE_ROWS - 1) // MOE_ROWS + 1
    blk_start = jnp.arange(n_blocks, dtype=jnp.int32) * MOE_ROWS
    blk_e = jnp.minimum(jnp.sum((pend[None, :] <= blk_start[:, None]).astype(jnp.int32), 1), N_EXPERTS - 1)
    blk_e = blk_e + layer * N_EXPERTS
    n_used = (pend[-1] // MOE_ROWS).astype(jnp.int32).reshape(1)
    flat_t = jnp.repeat(jnp.arange(T, dtype=jnp.int32), TOP_K_INNER)
    src = jnp.zeros((n_blocks * MOE_ROWS,), jnp.int32).at[dest].set(flat_t)
    yb = _expert_ffn(hn, src, blk_e, n_used, wg, wu, wd)
    return _combine(h, route, yb, dest)


def _rope_tables(L):
    freqs = ROPE_BASE ** (-jnp.arange(0, HEAD_DIM, 2, dtype=F32) / HEAD_DIM)
    ang = jnp.arange(L, dtype=F32)[:, None] * freqs[None]
    cos, sin = jnp.cos(ang), jnp.sin(ang)
    return jnp.concatenate([cos, cos], -1), jnp.concatenate([-sin, sin], -1)


N_MIX_SEG = 11


REORDER_COLS = SEG_W // 2
N_BRANCH_BLK = N_BRANCH * D_MODEL // REORDER_COLS


def _reorder_kernel(wm_ref, wg_ref, o_ref):
    j = pl.program_id(1)

    @pl.when(j < N_BRANCH_BLK)
    def _():
        o_ref[...] = wg_ref[0].T.astype(BF16)

    @pl.when(j >= N_BRANCH_BLK)
    def _():
        o_ref[...] = wm_ref[...].T.astype(BF16)


def _reorder_w_in(w_in_t):
    depth = w_in_t.shape[0]
    gate_row0 = N_MIX_SEG * SEG_W + N_GATE_COLS
    blk = (None, REORDER_COLS, D_MODEL)
    return pl.pallas_call(
        _reorder_kernel,
        grid=(depth, N_SEG * SEG_W // REORDER_COLS),
        in_specs=[pl.BlockSpec(blk, lambda l, j: (l, jnp.maximum(j - N_BRANCH_BLK, 0), 0)),
                  pl.BlockSpec((pl.Element(1), pl.Element(REORDER_COLS), pl.Element(D_MODEL)),
                               lambda l, j: (l, pl.multiple_of(
                                   gate_row0 + jnp.minimum(j, N_BRANCH_BLK - 1) * REORDER_COLS, 32), 0))],
        out_specs=pl.BlockSpec((None, D_MODEL, REORDER_COLS), lambda l, j: (l, 0, j)),
        out_shape=jax.ShapeDtypeStruct((depth, D_MODEL, N_SEG * SEG_W), BF16),
        compiler_params=_cparams(("arbitrary", "arbitrary")),
        name="reorder_w_in",
    )(w_in_t, w_in_t)


def _cast_kernel(w_ref, o_ref):
    o_ref[...] = w_ref[...].astype(BF16)


def _cast_experts(w):
    depth, e, r, c = w.shape
    spec = pl.BlockSpec((1, r, c), lambda i: (i, 0, 0))
    return pl.pallas_call(
        _cast_kernel,
        grid=(depth * e,),
        in_specs=[spec],
        out_specs=spec,
        out_shape=jax.ShapeDtypeStruct((depth * e, r, c), BF16),
        compiler_params=_cparams(("arbitrary",)),
        name="cast_experts",
    )(w.reshape(depth * e, r, c))


def _prep_layer(p):
    wift = p["wif_t"]
    wif_pad = jnp.zeros((D_MODEL, LANES), F32).at[:, :N_GATE_COLS].set(wift.T)
    rw = jnp.zeros((D_MODEL, LANES), F32).at[:, :N_GROUPS].set(p["router_g_w"]) \
        .at[:, N_GROUPS:N_GROUPS + N_EXPERTS].set(p["router_e_w"])
    rb = jnp.zeros((1, LANES), F32).at[0, :N_GROUPS].set(p["router_g_b"]) \
        .at[0, N_GROUPS:N_GROUPS + N_EXPERTS].set(p["router_e_b"])
    return dict(
        nw=p["norm1_w"].astype(F32).reshape(1, D_MODEL), wif=wif_pad, wift=wift,
        qn=p["na_q_norm"].astype(F32).reshape(1, HEAD_DIM), kn=p["na_k_norm"].astype(F32).reshape(1, HEAD_DIM),
        wb=p["w_branch"].astype(BF16), bgb=p["branch_gate_b"].astype(F32), wout=p["w_out"].astype(BF16),
        n2w=p["norm2_w"].astype(F32).reshape(1, D_MODEL), rb=rb,
        rw_hi=rw.astype(BF16), rw_lo=(rw - rw.astype(BF16).astype(F32)).astype(BF16),
    )


def _layer(x2d, B, L, p, q, rope, w_main, experts, layer):
    proj, gates, gates_t = _inproj(x2d, L, q["nw"], w_main, layer, q["wif"], q["wift"], rope[0], rope[1],
                                   q["qn"], q["kn"])
    o_ret, o_na, o_ml = _mixers(proj, gates, gates_t, B, L, p)
    h, hn, route, counts = _merge(x2d, o_ret, o_na, o_ml, proj, q["wb"], q["bgb"], q["wout"], q["n2w"],
                                  q["rw_hi"], q["rw_lo"], q["rb"])
    return _moe(h, hn, route, counts, experts[0], experts[1], experts[2], layer)


_PARAM_NAMES = ("norm1_w", "w_in", "ret_decay", "ret_norm_w", "na_q_norm", "na_k_norm", "na_rpb", "ml_conv",
                "ml_igate_b", "ml_fgate_b", "ml_norm_w", "w_branch", "branch_gate_b", "w_out", "norm2_w",
                "router_g_w", "router_g_b", "router_e_w", "router_e_b", "exp_w_gate", "exp_w_up", "exp_w_down")


def _run(x, layers, preps, w_main, experts):
    B, L, D = x.shape
    rope = _rope_tables(L)
    x2d = x.reshape(B * L, D)
    for layer, (p, q) in enumerate(zip(layers, preps)):
        x2d = _layer(x2d, B, L, p, q, rope, w_main, experts, layer)
    return x2d.reshape(B, L, D)


def kernel(x_prompt, x_sample, norm1_w, w_in, ret_decay, ret_norm_w, na_q_norm, na_k_norm, na_rpb, ml_conv,
           ml_igate_b, ml_fgate_b, ml_norm_w, w_branch, branch_gate_b, w_out, norm2_w, router_g_w, router_g_b,
           router_e_w, router_e_b, exp_w_gate, exp_w_up, exp_w_down):
    stacked = (norm1_w, w_in, ret_decay, ret_norm_w, na_q_norm, na_k_norm, na_rpb, ml_conv, ml_igate_b,
               ml_fgate_b, ml_norm_w, w_branch, branch_gate_b, w_out, norm2_w, router_g_w, router_g_b,
               router_e_w, router_e_b, exp_w_gate, exp_w_up, exp_w_down)
    depth = w_in.shape[0]
    w_in_t = jnp.swapaxes(w_in, 1, 2)
    wif_t = w_in_t[:, N_MIX_SEG * SEG_W:N_MIX_SEG * SEG_W + N_GATE_COLS, :]
    layers = [dict(zip(_PARAM_NAMES, (None if a is w_in else a[l] for a in stacked)), wif_t=wif_t[l])
              for l in range(depth)]
    preps = [_prep_layer(p) for p in layers]
    experts = tuple(_cast_experts(w) for w in (exp_w_gate, exp_w_up, exp_w_down))
    w_main = _reorder_w_in(w_in_t)
    return (_run(x_prompt, layers, preps, w_main, experts), _run(x_sample, layers, preps, w_main, experts))
```
